```python
import jax, jax.numpy as jnp
from jax import lax
import numpy as np

D_MODEL = 2048
BATCH = 4
SEQ = 2048
DEPTH = 1
DEC_BATCH = 32
DEC_SEQ = 32
PAST_LEN = 1024

CHUNK = 64
N_HEADS = 16
QK_NOPE = 128
QK_ROPE = 64
V_HEAD = 128
Q_LORA = 512
KV_LORA = 512
POOL_WINDOWS = (2, 4, 8, 16)
POOL_GROUPS = len(POOL_WINDOWS)
POOL_WIDTH = D_MODEL // 2
POOL_GC = POOL_WIDTH // POOL_GROUPS
POOL_HIST = max(POOL_WINDOWS) - 1
D_FF = 5632
N_BRANCH = 2
D_IN = POOL_WIDTH + Q_LORA + KV_LORA + QK_ROPE + N_BRANCH * D_MODEL
ROPE_THETA = 10000.0
EPS = 1e-6
Q_BLOCK = 128
SM_SCALE = (QK_NOPE + QK_ROPE) ** -0.5
NEG_INF = -1e30

kernel_name = 'hybrid_pool_mla_macaron_stream_step'


def rmsnorm(x, g):
    xf = x.astype(jnp.float32)
    out = xf * lax.rsqrt(jnp.mean(xf * xf, axis=-1, keepdims=True) + EPS) * g.astype(jnp.float32)
    return out.astype(x.dtype)


def swiglu(x, w_gate, w_up, w_down):
    return (jax.nn.silu(x @ w_gate) * (x @ w_up)) @ w_down


def rope(x, pos):
    half = QK_ROPE // 2
    inv = ROPE_THETA ** (-jnp.arange(half, dtype=jnp.float32) * 2.0 / QK_ROPE)
    ang = pos.astype(jnp.float32)[:, None] * inv[None, :]
    if x.ndim == 4:
        ang = ang[:, None, :]
    c, s = jnp.cos(ang), jnp.sin(ang)
    xf = x.astype(jnp.float32)
    x1, x2 = xf[..., :half], xf[..., half:]
    return jnp.concatenate([x1 * c - x2 * s, x2 * c + x1 * s], axis=-1).astype(x.dtype)


def pool_mix(z_hist, z, pos, w_pool, pool_scale):
    B, T, P = z.shape
    L = z_hist.shape[1]
    z_ext = jnp.concatenate([z_hist, z], axis=1)
    cs = jnp.cumsum(z_ext.astype(jnp.float32), axis=1)
    cs = jnp.pad(cs, ((0, 0), (1, 0), (0, 0)))
    means = []
    for g, w in enumerate(POOL_WINDOWS):
        sl = slice(g * POOL_GC, (g + 1) * POOL_GC)
        s = cs[:, L + 1:L + 1 + T, sl] - cs[:, L + 1 - w:L + 1 - w + T, sl]
        cnt = jnp.minimum(pos + 1, w).astype(jnp.float32)[None, :, None]
        means.append(s / cnt)
    pooled = jnp.concatenate(means, axis=-1) - z.astype(jnp.float32)
    pooled = pooled.astype(z.dtype).reshape(B, T, POOL_GROUPS, POOL_GC)
    mixed = jnp.einsum('btgc,gcd->btgd', pooled, w_pool).reshape(B, T, P)
    return mixed * pool_scale, z_ext[:, -POOL_HIST:]


def mla_attend(q_nope, q_rope, ckv, k_rope, q_pos, k_pos, w_uk, w_uv):
    B, Tq = q_nope.shape[:2]
    k_nope = jnp.einsum('bsc,chd->bshd', ckv, w_uk)
    v = jnp.einsum('bsc,chd->bshd', ckv, w_uv)
    k_chunk = k_pos // CHUNK

    def attend_block(args):
        qn, qr, qp = args
        s = (jnp.einsum('bqhd,bshd->bhqs', qn, k_nope)
             + jnp.einsum('bqhr,bsr->bhqs', qr, k_rope)).astype(jnp.float32) * SM_SCALE
        mask = k_chunk[None, :] <= (qp // CHUNK)[:, None]
        s = jnp.where(mask[None, None], s, NEG_INF)
        p = jax.nn.softmax(s, axis=-1).astype(v.dtype)
        return jnp.einsum('bhqs,bshd->bqhd', p, v)

    if Tq > Q_BLOCK and Tq % Q_BLOCK == 0:
        nb = Tq // Q_BLOCK
        qn_b = q_nope.reshape(B, nb, Q_BLOCK, N_HEADS, QK_NOPE).transpose(1, 0, 2, 3, 4)
        qr_b = q_rope.reshape(B, nb, Q_BLOCK, N_HEADS, QK_ROPE).transpose(1, 0, 2, 3, 4)
        qp_b = q_pos.reshape(nb, Q_BLOCK)
        out = lax.map(attend_block, (qn_b, qr_b, qp_b))
        return out.transpose(1, 0, 2, 3, 4).reshape(B, Tq, N_HEADS, V_HEAD)
    return attend_block((q_nope, q_rope, q_pos))


def token_mix(u, pos, past_pos, pool_hist, ckv_past, krope_past, p):
    B, T, _ = u.shape
    proj = u @ p['w_in']
    o1 = POOL_WIDTH
    o2 = o1 + Q_LORA
    o3 = o2 + KV_LORA
    o4 = o3 + QK_ROPE
    z_pool = proj[..., :o1]
    q_lat = rmsnorm(proj[..., o1:o2], p['g_q_lat'])
    ckv = rmsnorm(proj[..., o2:o3], p['g_kv_lat'])
    krope = rope(proj[..., o3:o4], pos)
    gates = jax.nn.sigmoid(proj[..., o4:].astype(jnp.float32)).astype(u.dtype).reshape(B, T, N_BRANCH, D_MODEL)

    pooled, pool_tail = pool_mix(pool_hist, z_pool, pos, p['w_pool'], p['pool_scale'])
    a = pooled @ p['w_pool_out']

    q = jnp.einsum('btc,chd->bthd', q_lat, p['w_uq'])
    q_nope = q[..., :QK_NOPE]
    q_rope = rope(q[..., QK_NOPE:], pos)
    ckv_all = jnp.concatenate([ckv_past, ckv], axis=1)
    kr_all = jnp.concatenate([krope_past, krope], axis=1)
    k_pos = jnp.concatenate([past_pos, pos])
    o = mla_attend(q_nope, q_rope, ckv_all, kr_all, pos, k_pos, p['w_uk'], p['w_uv'])
    b = o.reshape(B, T, N_HEADS * V_HEAD) @ p['w_o_attn']

    merged = gates[:, :, 0] * a + gates[:, :, 1] * b
    return merged @ p['w_out'], ckv, krope, pool_tail


def layer(x, pos, past_pos, pool_hist, ckv_past, krope_past, p):
    h = x + 0.5 * swiglu(rmsnorm(x, p['g_ffn1']), p['w1_gate'], p['w1_up'], p['w1_down'])
    mix, ckv_new, kr_new, pool_tail = token_mix(rmsnorm(h, p['g_mix']), pos, past_pos,
                                                pool_hist, ckv_past, krope_past, p)
    h = h + mix
    h = h + 0.5 * swiglu(rmsnorm(h, p['g_ffn2']), p['w2_gate'], p['w2_up'], p['w2_down'])
    return h, ckv_new, kr_new, pool_tail


def setup_inputs(seed: int = 0) -> dict:
    key = jax.random.key(seed)
    ks = iter(jax.random.split(key, 40))
    f32 = jnp.float32

    def nrm(shape, scale):
        return jax.random.normal(next(ks), shape, f32) * scale

    def gain(shape):
        return 1.0 + 0.01 * jax.random.normal(next(ks), shape, f32)

    return {
        'x_prompt': nrm((BATCH, SEQ, D_MODEL), 1.0),
        'x_sample': nrm((DEC_BATCH, DEC_SEQ, D_MODEL), 1.0),
        'cache_ckv': nrm((DEPTH, DEC_BATCH, PAST_LEN, KV_LORA), 1.0),
        'cache_krope': nrm((DEPTH, DEC_BATCH, PAST_LEN, QK_ROPE), 1.0),
        'state_pool': nrm((DEPTH, DEC_BATCH, POOL_HIST, POOL_WIDTH), 1.0),
        'g_ffn1': gain((DEPTH, D_MODEL)),
        'w1_gate': nrm((DEPTH, D_MODEL, D_FF), D_MODEL ** -0.5),
        'w1_up': nrm((DEPTH, D_MODEL, D_FF), D_MODEL ** -0.5),
        'w1_down': nrm((DEPTH, D_FF, D_MODEL), D_FF ** -0.5),
        'g_mix': gain((DEPTH, D_MODEL)),
        'w_in': nrm((DEPTH, D_MODEL, D_IN), D_MODEL ** -0.5),
        'g_q_lat': gain((DEPTH, Q_LORA)),
        'g_kv_lat': gain((DEPTH, KV_LORA)),
        'w_uq': nrm((DEPTH, Q_LORA, N_HEADS, QK_NOPE + QK_ROPE), Q_LORA ** -0.5),
        'w_uk': nrm((DEPTH, KV_LORA, N_HEADS, QK_NOPE), KV_LORA ** -0.5),
        'w_uv': nrm((DEPTH, KV_LORA, N_HEADS, V_HEAD), KV_LORA ** -0.5),
        'w_o_attn': nrm((DEPTH, N_HEADS * V_HEAD, D_MODEL), (N_HEADS * V_HEAD) ** -0.5),
        'w_pool': nrm((DEPTH, POOL_GROUPS, POOL_GC, POOL_GC), POOL_GC ** -0.5),
        'pool_scale': gain((DEPTH, POOL_WIDTH)),
        'w_pool_out': nrm((DEPTH, POOL_WIDTH, D_MODEL), POOL_WIDTH ** -0.5),
        'w_out': nrm((DEPTH, D_MODEL, D_MODEL), D_MODEL ** -0.5),
        'g_ffn2': gain((DEPTH, D_MODEL)),
        'w2_gate': nrm((DEPTH, D_MODEL, D_FF), D_MODEL ** -0.5),
        'w2_up': nrm((DEPTH, D_MODEL, D_FF), D_MODEL ** -0.5),
        'w2_down': nrm((DEPTH, D_FF, D_MODEL), D_FF ** -0.5),
        'g_final': gain((D_MODEL,)),
    }


def reference(x_prompt, x_sample, cache_ckv, cache_krope, state_pool,
              g_ffn1, w1_gate, w1_up, w1_down, g_mix, w_in, g_q_lat, g_kv_lat,
              w_uq, w_uk, w_uv, w_o_attn, w_pool, pool_scale, w_pool_out, w_out,
              g_ffn2, w2_gate, w2_up, w2_down, g_final):
    Bp, Tp, _ = x_prompt.shape
    Bs, Ts, _ = x_sample.shape
    past = cache_ckv.shape[2]
    dt = x_prompt.dtype

    pos_p = jnp.arange(Tp, dtype=jnp.int32)
    past_pos_p = jnp.zeros((0,), jnp.int32)
    pos_s = past + jnp.arange(Ts, dtype=jnp.int32)
    past_pos_s = jnp.arange(past, dtype=jnp.int32)

    hp, hs = x_prompt, x_sample
    ckv_p, kr_p, pool_p, ckv_s, kr_s, pool_s = [], [], [], [], [], []
    for l in range(DEPTH):
        lw = {
            'g_ffn1': g_ffn1[l], 'w1_gate': w1_gate[l], 'w1_up': w1_up[l], 'w1_down': w1_down[l],
            'g_mix': g_mix[l], 'w_in': w_in[l], 'g_q_lat': g_q_lat[l], 'g_kv_lat': g_kv_lat[l],
            'w_uq': w_uq[l], 'w_uk': w_uk[l], 'w_uv': w_uv[l], 'w_o_attn': w_o_attn[l],
            'w_pool': w_pool[l], 'pool_scale': pool_scale[l], 'w_pool_out': w_pool_out[l],
            'w_out': w_out[l], 'g_ffn2': g_ffn2[l], 'w2_gate': w2_gate[l], 'w2_up': w2_up[l],
            'w2_down': w2_down[l],
        }
        hp, c1, k1, t1 = layer(hp, pos_p, past_pos_p,
                               jnp.zeros((Bp, POOL_HIST, POOL_WIDTH), dt),
                               jnp.zeros((Bp, 0, KV_LORA), dt),
                               jnp.zeros((Bp, 0, QK_ROPE), dt), lw)
        hs, c2, k2, t2 = layer(hs, pos_s, past_pos_s, state_pool[l].astype(hs.dtype),
                               cache_ckv[l].astype(hs.dtype), cache_krope[l].astype(hs.dtype), lw)
        ckv_p.append(c1); kr_p.append(k1); pool_p.append(t1)
        ckv_s.append(c2); kr_s.append(k2); pool_s.append(t2)

    y_prompt = rmsnorm(hp, g_final)
    y_sample = rmsnorm(hs, g_final)
    return (y_prompt, y_sample,
            jnp.stack(ckv_p), jnp.stack(kr_p), jnp.stack(pool_p),
            jnp.stack(ckv_s), jnp.stack(kr_s), jnp.stack(pool_s))
```

```python
import functools

import jax
import jax.numpy as jnp
from jax import lax
from jax.experimental import pallas as pl
from jax.experimental.pallas import tpu as pltpu

F32 = jnp.float32
BF16 = jnp.bfloat16

CHUNK_SHIFT = 6
N_HEADS = 16
QK_NOPE = 128
QK_ROPE = 64
V_HEAD = 128
HEAD_W = 256
POOL_WINDOWS = (2, 4, 8, 16)
POOL_HIST = 15
POOL_PAD = 16
ROPE_THETA = 10000.0
EPS = 1e-6
SM_SCALE = (QK_NOPE + QK_ROPE) ** -0.5
NEG_INF = -1e30
MIB = 1024 * 1024


def _dot(a, b):
    return jnp.dot(a, b, preferred_element_type=F32)


def _dot_t(a, b):
    return lax.dot_general(a, b, (((1,), (1,)), ((), ())), preferred_element_type=F32)


def _rms(x, g):
    return x * lax.rsqrt(jnp.mean(x * x, axis=-1, keepdims=True) + EPS) * g


def _params(semantics, vmem_mib):
    return pltpu.CompilerParams(dimension_semantics=semantics, vmem_limit_bytes=vmem_mib * MIB)


def _ffn_kernel(x_ref, g_ref, wg_ref, wu_ref, wd_ref, gn_ref, acc_ref, *rest, emit_normed):
    if emit_normed:
        u_ref, xn_ref = rest
    else:
        (xn_ref,) = rest
    j = pl.program_id(1)

    @pl.when(j == 0)
    def _():
        xn_ref[...] = _rms(x_ref[...], g_ref[...]).astype(BF16)

    xn = xn_ref[...]
    gate = _dot(xn, wg_ref[...])
    up = _dot(xn, wu_ref[...])
    act = (gate * jax.nn.sigmoid(gate) * up).astype(BF16)
    part = _dot(act, wd_ref[...])

    @pl.when(j == 0)
    def _():
        acc_ref[...] = part

    @pl.when(j > 0)
    def _():
        acc_ref[...] += part

    @pl.when(j == pl.num_programs(1) - 1)
    def _():
        h = x_ref[...] + 0.5 * acc_ref[...]
        if emit_normed:
            acc_ref[...] = h
            u_ref[...] = _rms(h, gn_ref[...]).astype(BF16)
        else:
            acc_ref[...] = _rms(h, gn_ref[...])


def _ffn(x, g, wg, wu, wd, g_next, *, emit_normed, tm=512, tf=512):
    n, d = x.shape
    f = wg.shape[1]
    row = lambda i, j: (i, 0)
    out_shape = [jax.ShapeDtypeStruct((n, d), F32)]
    out_specs = [pl.BlockSpec((tm, d), row)]
    if emit_normed:
        out_shape.append(jax.ShapeDtypeStruct((n, d), BF16))
        out_specs.append(pl.BlockSpec((tm, d), row))
    return pl.pallas_call(
        functools.partial(_ffn_kernel, emit_normed=emit_normed),
        grid=(n // tm, f // tf),
        in_specs=[
            pl.BlockSpec((tm, d), row),
            pl.BlockSpec((1, d), lambda i, j: (0, 0)),
            pl.BlockSpec((d, tf), lambda i, j: (0, j)),
            pl.BlockSpec((d, tf), lambda i, j: (0, j)),
            pl.BlockSpec((tf, d), lambda i, j: (j, 0)),
            pl.BlockSpec((1, d), lambda i, j: (0, 0)),
        ],
        out_specs=out_specs,
        out_shape=out_shape,
        scratch_shapes=[pltpu.VMEM((tm, d), BF16)],
        compiler_params=_params(("parallel", "arbitrary"), 48),
        name="ffn_norm" if emit_normed else "ffn_final",
    )(x, g, wg, wu, wd, g_next)


def _inproj_kernel(u_ref, w_ref, gq_ref, gkv_ref, cs_ref, z_ref, q_ref, ckv_ref, kr_ref, *, pw, ql, kl):
    proj = _dot(u_ref[...], w_ref[...])
    z_ref[...] = proj[:, :pw]
    q_ref[...] = _rms(proj[:, pw:pw + ql], gq_ref[...]).astype(BF16)
    ckv_ref[...] = _rms(proj[:, pw + ql:pw + ql + kl], gkv_ref[...])
    y = proj[:, pw + ql + kl:] * cs_ref[...]
    kr_ref[...] = y + pltpu.roll(y, QK_ROPE, axis=1)


def _inproj(u, w_small, g_q, g_kv, cs_tab, *, pw, ql, kl, tm=512):
    n, d = u.shape
    wn = w_small.shape[1]
    period = cs_tab.shape[0] // tm
    row = lambda i: (i, 0)
    fixed = lambda i: (0, 0)
    return pl.pallas_call(
        functools.partial(_inproj_kernel, pw=pw, ql=ql, kl=kl),
        grid=(n // tm,),
        in_specs=[
            pl.BlockSpec((tm, d), row),
            pl.BlockSpec((d, wn), fixed),
            pl.BlockSpec((1, ql), fixed),
            pl.BlockSpec((1, kl), fixed),
            pl.BlockSpec((tm, 2 * QK_ROPE), lambda i: (i % period, 0)),
        ],
        out_specs=[
            pl.BlockSpec((tm, pw), row),
            pl.BlockSpec((tm, ql), row),
            pl.BlockSpec((tm, kl), row),
            pl.BlockSpec((tm, 2 * QK_ROPE), row),
        ],
        out_shape=[
            jax.ShapeDtypeStruct((n, pw), F32),
            jax.ShapeDtypeStruct((n, ql), BF16),
            jax.ShapeDtypeStruct((n, kl), F32),
            jax.ShapeDtypeStruct((n, 2 * QK_ROPE), F32),
        ],
        compiler_params=_params(("parallel",), 48),
        name="in_proj",
    )(u, w_small, g_q, g_kv, cs_tab)


def _pool_rows(zext, pos0, w_ref, scale_ref, out_ref, out_row0, seg_rows, n_seg):
    p = zext.shape[1]
    gc = p // len(POOL_WINDOWS)
    ext = POOL_PAD + seg_rows
    pos = pos0 + lax.broadcasted_iota(jnp.int32, (seg_rows, 1), 0)
    for g, w in enumerate(POOL_WINDOWS):
        cols = slice(g * gc, (g + 1) * gc)
        s = zext[:, cols]
        win = s
        step = 1
        while step < w:
            win = win + pltpu.roll(win, step, axis=0)
            step *= 2
        cnt = jnp.minimum(pos + 1, w).astype(F32)
        for b in range(n_seg):
            lo = b * ext + POOL_PAD
            pooled = win[lo:lo + seg_rows] / cnt - s[lo:lo + seg_rows]
            mixed = _dot(pooled.astype(BF16), w_ref[g]) * scale_ref[:, cols]
            out_ref[pl.ds(out_row0 + b * seg_rows, seg_rows), cols] = mixed.astype(BF16)


def _pool_prompt_kernel(cur_ref, prev_ref, w_ref, scale_ref, out_ref, ext_ref, *, tp):
    t = pl.program_id(1)
    hist = jnp.where(t == 0, 0.0, prev_ref[0])
    ext_ref[0:POOL_PAD, :] = hist
    ext_ref[POOL_PAD:, :] = cur_ref[0]
    _pool_rows(ext_ref[...], t * tp, w_ref, scale_ref, out_ref.at[0], 0, tp, 1)


def _pool_prompt(z, w_pool, pool_scale, *, tp=512):
    b, t, p = z.shape
    hist_blocks = tp // POOL_PAD
    return pl.pallas_call(
        functools.partial(_pool_prompt_kernel, tp=tp),
        grid=(b, t // tp),
        in_specs=[
            pl.BlockSpec((1, tp, p), lambda i, j: (i, j, 0)),
            pl.BlockSpec((1, POOL_PAD, p), lambda i, j: (i, jnp.maximum(j * hist_blocks - 1, 0), 0)),
            pl.BlockSpec(w_pool.shape, lambda i, j: (0, 0, 0)),
            pl.BlockSpec((1, p), lambda i, j: (0, 0)),
        ],
        out_specs=pl.BlockSpec((1, tp, p), lambda i, j: (i, j, 0)),
        out_shape=jax.ShapeDtypeStruct((b, t, p), BF16),
        scratch_shapes=[pltpu.VMEM((POOL_PAD + tp, p), F32)],
        compiler_params=_params(("parallel", "arbitrary"), 32),
        name="pool_prompt",
    )(z, z, w_pool, pool_scale)


def _pool_sample_kernel(cur_ref, hist_ref, w_ref, scale_ref, out_ref, ext_ref, *, nb, ts, pos0):
    ext = POOL_PAD + ts
    for b in range(nb):
        ext_ref[b * ext:b * ext + POOL_PAD, :] = hist_ref[b]
        ext_ref[b * ext + POOL_PAD:(b + 1) * ext, :] = cur_ref[b]
    _pool_rows(ext_ref[...], pos0, w_ref, scale_ref, out_ref, 0, ts, nb)


def _pool_sample(z, hist, w_pool, pool_scale, *, pos0, nb=8):
    b, ts, p = z.shape
    return pl.pallas_call(
        functools.partial(_pool_sample_kernel, nb=nb, ts=ts, pos0=pos0),
        grid=(b // nb,),
        in_specs=[
            pl.BlockSpec((nb, ts, p), lambda i: (i, 0, 0)),
            pl.BlockSpec((nb, POOL_PAD, p), lambda i: (i, 0, 0)),
            pl.BlockSpec(w_pool.shape, lambda i: (0, 0, 0)),
            pl.BlockSpec((1, p), lambda i: (0, 0)),
        ],
        out_specs=pl.BlockSpec((nb * ts, p), lambda i: (i, 0)),
        out_shape=jax.ShapeDtypeStruct((b * ts, p), BF16),
        scratch_shapes=[pltpu.VMEM((nb * (POOL_PAD + ts), p), F32)],
        compiler_params=_params(("parallel",), 32),
        name="pool_sample",
    )(z, hist, w_pool, pool_scale)


def _qkv_prompt_kernel(ql_ref, ckv_ref, kr_ref, wq_ref, wk_ref, wv_ref, qt_ref, q_ref, k_ref, v_ref):
    ql = ql_ref[...]
    ckv = ckv_ref[...].astype(BF16)
    kr = kr_ref[...].astype(BF16)
    qt = qt_ref[...]
    for h in range(N_HEADS):
        q_ref[:, h * HEAD_W:(h + 1) * HEAD_W] = (_dot(ql, wq_ref[:, h * HEAD_W:(h + 1) * HEAD_W]) * qt).astype(BF16)
    kn = _dot(ckv, wk_ref[...])
    for h in range(N_HEADS):
        k_ref[:, h * HEAD_W:h * HEAD_W + QK_NOPE] = kn[:, h * QK_NOPE:(h + 1) * QK_NOPE].astype(BF16)
        k_ref[:, h * HEAD_W + QK_NOPE:(h + 1) * HEAD_W] = kr
    v_ref[...] = _dot(ckv, wv_ref[...]).astype(BF16)


def _qkv_prompt(q_lat, ckv, krblk, wq_cat, w_uk, w_uv, qt_tab, *, tm=512):
    n = q_lat.shape[0]
    period = qt_tab.shape[0] // tm
    row = lambda i: (i, 0)
    fixed = lambda i: (0, 0)
    return pl.pallas_call(
        _qkv_prompt_kernel,
        grid=(n // tm,),
        in_specs=[
            pl.BlockSpec((tm, q_lat.shape[1]), row),
            pl.BlockSpec((tm, ckv.shape[1]), row),
            pl.BlockSpec((tm, krblk.shape[1]), row),
            pl.BlockSpec(wq_cat.shape, fixed),
            pl.BlockSpec(w_uk.shape, fixed),
            pl.BlockSpec(w_uv.shape, fixed),
            pl.BlockSpec((tm, HEAD_W), lambda i: (i % period, 0)),
        ],
        out_specs=[
            pl.BlockSpec((tm, N_HEADS * HEAD_W), row),
            pl.BlockSpec((tm, N_HEADS * HEAD_W), row),
            pl.BlockSpec((tm, N_HEADS * V_HEAD), row),
        ],
        out_shape=[
            jax.ShapeDtypeStruct((n, N_HEADS * HEAD_W), BF16),
            jax.ShapeDtypeStruct((n, N_HEADS * HEAD_W), BF16),
            jax.ShapeDtypeStruct((n, N_HEADS * V_HEAD), BF16),
        ],
        compiler_params=_params(("parallel",), 48),
        name="qkv_prompt",
    )(q_lat, ckv, krblk, wq_cat, w_uk, w_uv, qt_tab)


def _attn_prompt_kernel(q_ref, k_ref, v_ref, o_ref, *, blk, hb):
    qi = pl.program_id(2)
    row_chunk = (qi * blk + lax.broadcasted_iota(jnp.int32, (blk, blk), 0)) >> CHUNK_SHIFT
    col = lax.broadcasted_iota(jnp.int32, (blk, blk), 1)
    for hh in range(hb):
        q = q_ref[0, :, hh * HEAD_W:(hh + 1) * HEAD_W]

        def body(j, carry, hh=hh, q=q):
            m, l, acc = carry
            off = pl.multiple_of(j * blk, blk)
            k = k_ref[0, pl.ds(off, blk), hh * HEAD_W:(hh + 1) * HEAD_W]
            v = v_ref[0, pl.ds(off, blk), hh * V_HEAD:(hh + 1) * V_HEAD]
            s = _dot_t(q, k)
            s = jnp.where(((off + col) >> CHUNK_SHIFT) <= row_chunk, s, NEG_INF)
            m_new = jnp.maximum(m, jnp.max(s, axis=-1, keepdims=True))
            alpha = jnp.exp(m - m_new)
            p = jnp.exp(s - m_new)
            l = alpha * l + jnp.sum(p, axis=-1, keepdims=True)
            acc = alpha * acc + _dot(p.astype(BF16), v)
            return m_new, l, acc

        init = (jnp.full((blk, 1), NEG_INF, F32), jnp.zeros((blk, 1), F32), jnp.zeros((blk, V_HEAD), F32))
        _, l, acc = lax.fori_loop(0, qi + 1, body, init)
        o_ref[0, :, hh * V_HEAD:(hh + 1) * V_HEAD] = (acc * (1.0 / l)).astype(BF16)


def _attn_prompt(q_cat, k_cat, v, *, blk=512, hb=4):
    b, t, _ = q_cat.shape
    return pl.pallas_call(
        functools.partial(_attn_prompt_kernel, blk=blk, hb=hb),
        grid=(b, N_HEADS // hb, t // blk),
        in_specs=[
            pl.BlockSpec((1, blk, hb * HEAD_W), lambda i, g, q: (i, q, g)),
            pl.BlockSpec((1, t, hb * HEAD_W), lambda i, g, q: (i, 0, g)),
            pl.BlockSpec((1, t, hb * V_HEAD), lambda i, g, q: (i, 0, g)),
        ],
        out_specs=pl.BlockSpec((1, blk, hb * V_HEAD), lambda i, g, q: (i, q, g)),
        out_shape=jax.ShapeDtypeStruct((b, t, N_HEADS * V_HEAD), BF16),
        compiler_params=_params(("parallel", "parallel", "arbitrary"), 48),
        name="attn_prompt",
    )(q_cat, k_cat, v)


def _q_sample_kernel(ql_ref, wq_ref, wk_ref, qt_ref, qa_ref, qr_ref, *, kl):
    ql = ql_ref[...]
    qt = qt_ref[...]
    for h in range(N_HEADS):
        q = _dot(ql, wq_ref[:, h * HEAD_W:(h + 1) * HEAD_W]) * qt
        qn = q[:, :QK_NOPE].astype(BF16)
        qa_ref[:, h * kl:(h + 1) * kl] = _dot_t(qn, wk_ref[:, h * QK_NOPE:(h + 1) * QK_NOPE]).astype(BF16)
        y = q[:, QK_NOPE:]
        qr_ref[:, h * 2 * QK_ROPE:(h + 1) * 2 * QK_ROPE] = (y + pltpu.roll(y, QK_ROPE, axis=1)).astype(BF16)


def _q_sample(q_lat, wq_cat, w_uk, qt_tab, *, tm=512):
    n, ql = q_lat.shape
    kl = w_uk.shape[0]
    row = lambda i: (i, 0)
    fixed = lambda i: (0, 0)
    return pl.pallas_call(
        functools.partial(_q_sample_kernel, kl=kl),
        grid=(n // tm,),
        in_specs=[
            pl.BlockSpec((tm, ql), row),
            pl.BlockSpec(wq_cat.shape, fixed),
            pl.BlockSpec(w_uk.shape, fixed),
            pl.BlockSpec((tm, HEAD_W), row),
        ],
        out_specs=[
            pl.BlockSpec((tm, N_HEADS * kl), row),
            pl.BlockSpec((tm, N_HEADS * 2 * QK_ROPE), row),
        ],
        out_shape=[
            jax.ShapeDtypeStruct((n, N_HEADS * kl), BF16),
            jax.ShapeDtypeStruct((n, N_HEADS * 2 * QK_ROPE), BF16),
        ],
        compiler_params=_params(("parallel",), 48),
        name="q_sample",
    )(q_lat, wq_cat, w_uk, qt_tab)


def _attn_sample_kernel(qa_ref, qr_ref, cc_ref, ck_ref, nc_ref, nk_ref, wv_ref, o_ref, *, ts, past, kl):
    rows = N_HEADS * ts
    qs = jnp.concatenate([qa_ref[:, h * kl:(h + 1) * kl] for h in range(N_HEADS)], axis=0)
    qr = jnp.concatenate(
        [qr_ref[:, h * 2 * QK_ROPE:h * 2 * QK_ROPE + QK_ROPE] for h in range(N_HEADS)], axis=0)
    kc = cc_ref[0].astype(BF16)
    krc = ck_ref[0].astype(BF16)
    kn = nc_ref[...].astype(BF16)
    krn = nk_ref[:, :QK_ROPE].astype(BF16)
    s_c = _dot_t(qs, kc) + _dot_t(qr, krc)
    s_n = _dot_t(qs, kn) + _dot_t(qr, krn)
    q_chunk = (past + lax.broadcasted_iota(jnp.int32, (rows, 1), 0) % ts) >> CHUNK_SHIFT
    s_c = jnp.where((lax.broadcasted_iota(jnp.int32, (1, past), 1) >> CHUNK_SHIFT) <= q_chunk, s_c, NEG_INF)
    s_n = jnp.where(((past + lax.broadcasted_iota(jnp.int32, (1, ts), 1)) >> CHUNK_SHIFT) <= q_chunk, s_n, NEG_INF)
    m = jnp.maximum(jnp.max(s_c, axis=-1, keepdims=True), jnp.max(s_n, axis=-1, keepdims=True))
    p_c = jnp.exp(s_c - m)
    p_n = jnp.exp(s_n - m)
    l = jnp.sum(p_c, axis=-1, keepdims=True) + jnp.sum(p_n, axis=-1, keepdims=True)
    o_lat = ((_dot(p_c.astype(BF16), kc) + _dot(p_n.astype(BF16), kn)) * (1.0 / l)).astype(BF16)
    for h in range(N_HEADS):
        o_ref[:, h * V_HEAD:(h + 1) * V_HEAD] = _dot(
            o_lat[h * ts:(h + 1) * ts], wv_ref[:, h * V_HEAD:(h + 1) * V_HEAD]).astype(BF16)


def _attn_sample(q_abs, q_rope, cache_ckv, cache_kr, ckv_new, kr_new, w_uv, *, ts):
    b, past, kl = cache_ckv.shape
    row = lambda i: (i, 0)
    return pl.pallas_call(
        functools.partial(_attn_sample_kernel, ts=ts, past=past, kl=kl),
        grid=(b,),
        in_specs=[
            pl.BlockSpec((ts, q_abs.shape[1]), row),
            pl.BlockSpec((ts, q_rope.shape[1]), row),
            pl.BlockSpec((1, past, kl), lambda i: (i, 0, 0)),
            pl.BlockSpec((1, past, cache_kr.shape[2]), lambda i: (i, 0, 0)),
            pl.BlockSpec((ts, kl), row),
            pl.BlockSpec((ts, kr_new.shape[1]), row),
            pl.BlockSpec(w_uv.shape, lambda i: (0, 0)),
        ],
        out_specs=pl.BlockSpec((ts, N_HEADS * V_HEAD), row),
        out_shape=jax.ShapeDtypeStruct((b * ts, N_HEADS * V_HEAD), BF16),
        compiler_params=_params(("parallel",), 48),
        name="attn_sample",
    )(q_abs, q_rope, cache_ckv, cache_kr, ckv_new, kr_new, w_uv)


def _mix_out_kernel(u_ref, pa_ref, ob_ref, h_ref, wga_ref, wgb_ref, wpo_ref, woa_ref, wout_ref, acc_ref):
    j = pl.program_id(1)
    u = u_ref[...]
    gate_a = jax.nn.sigmoid(_dot(u, wga_ref[...]))
    gate_b = jax.nn.sigmoid(_dot(u, wgb_ref[...]))
    a = _dot(pa_ref[...], wpo_ref[...])
    b = _dot(ob_ref[...], woa_ref[...])
    merged = (gate_a * a + gate_b * b).astype(BF16)
    part = _dot(merged, wout_ref[...])

    @pl.when(j == 0)
    def _():
        acc_ref[...] = part

    @pl.when(j > 0)
    def _():
        acc_ref[...] += part

    @pl.when(j == pl.num_programs(1) - 1)
    def _():
        acc_ref[...] = h_ref[...] + acc_ref[...]


def _mix_out(u, pooled, o_attn, h, w_gate, w_pool_out, w_o_attn, w_out, *, tm=512, tc=256):
    n, d = h.shape
    nc = d // tc
    row = lambda i, j: (i, 0)
    col = lambda i, j: (0, j)
    return pl.pallas_call(
        _mix_out_kernel,
        grid=(n // tm, nc),
        in_specs=[
            pl.BlockSpec((tm, d), row),
            pl.BlockSpec((tm, pooled.shape[1]), row),
            pl.BlockSpec((tm, o_attn.shape[1]), row),
            pl.BlockSpec((tm, d), row),
            pl.BlockSpec((d, tc), col),
            pl.BlockSpec((d, tc), lambda i, j: (0, j + nc)),
            pl.BlockSpec((w_pool_out.shape[0], tc), col),
            pl.BlockSpec((w_o_attn.shape[0], tc), col),
            pl.BlockSpec((tc, d), lambda i, j: (j, 0)),
        ],
        out_specs=pl.BlockSpec((tm, d), row),
        out_shape=jax.ShapeDtypeStruct((n, d), F32),
        compiler_params=_params(("parallel", "arbitrary"), 48),
        name="mix_out",
    )(u, pooled, o_attn, h, w_gate, w_gate, w_pool_out, w_o_attn, w_out)


def _rope_tables(pos):
    half = QK_ROPE // 2
    inv = ROPE_THETA ** (-jnp.arange(half, dtype=F32) * 2.0 / QK_ROPE)
    ang = pos.astype(F32)[:, None] * inv[None, :]
    c, s = jnp.cos(ang), jnp.sin(ang)
    cs = jnp.concatenate([c, c, -s, s], axis=-1)
    qt = SM_SCALE * jnp.concatenate([jnp.ones((pos.shape[0], QK_NOPE), F32), cs], axis=-1)
    return cs, qt


def _dup_rope_cols(w):
    half = QK_ROPE // 2
    x1, x2 = w[..., :half], w[..., half:]
    return jnp.concatenate([x1, x2, x2, x1], axis=-1)


def kernel(x_prompt, x_sample, cache_ckv, cache_krope, state_pool, g_ffn1, w1_gate, w1_up, w1_down, g_mix, w_in, g_q_lat, g_kv_lat, w_uq, w_uk, w_uv, w_o_attn, w_pool, pool_scale, w_pool_out, w_out, g_ffn2, w2_gate, w2_up, w2_down, g_final):
    bp, tp, d = x_prompt.shape
    bs, ts, _ = x_sample.shape
    assert g_ffn1.shape[0] == 1, "single-layer stack only"
    assert ts >= POOL_HIST and tp >= POOL_HIST
    past = cache_ckv.shape[2]
    pw = pool_scale.shape[1]
    ql = g_q_lat.shape[1]
    kl = g_kv_lat.shape[1]
    o3 = pw + ql + kl
    o4 = o3 + QK_ROPE

    cs_p, qt_p = _rope_tables(jnp.arange(tp, dtype=jnp.int32))
    cs_s, qt_s = _rope_tables(past + jnp.arange(ts, dtype=jnp.int32))
    cs_s = jnp.tile(cs_s, (bs, 1))
    qt_s = jnp.tile(qt_s, (bs, 1))

    bf = lambda w: w[0].astype(BF16)
    vec = lambda g: g[0].reshape(1, -1)
    w_small = jnp.concatenate([w_in[0][:, :o3], _dup_rope_cols(w_in[0][:, o3:o4])], axis=1).astype(BF16)
    w_gate = w_in[0][:, o4:].astype(BF16)
    wq_cat = jnp.concatenate(
        [w_uq[0][..., :QK_NOPE], _dup_rope_cols(w_uq[0][..., QK_NOPE:])], axis=-1
    ).reshape(ql, N_HEADS * HEAD_W).astype(BF16)
    wk = w_uk[0].reshape(kl, N_HEADS * QK_NOPE).astype(BF16)
    wv = w_uv[0].reshape(kl, N_HEADS * V_HEAD).astype(BF16)
    w1g, w1u, w1d = bf(w1_gate), bf(w1_up), bf(w1_down)
    w2g, w2u, w2d = bf(w2_gate), bf(w2_up), bf(w2_down)
    wpool, wpo, woa, wout = bf(w_pool), bf(w_pool_out), bf(w_o_attn), bf(w_out)
    hist_s = jnp.pad(state_pool[0], ((0, 0), (POOL_PAD - POOL_HIST, 0), (0, 0)))

    def per_token_front(x, cs_tab):
        h1, u = _ffn(x, vec(g_ffn1), w1g, w1u, w1d, vec(g_mix), emit_normed=True)
        z, q_lat, ckv, krblk = _inproj(u, w_small, vec(g_q_lat), vec(g_kv_lat), cs_tab, pw=pw, ql=ql, kl=kl)
        return h1, u, z, q_lat, ckv, krblk

    def per_token_back(u, pooled, o, h1):
        h2 = _mix_out(u, pooled, o, h1, w_gate, wpo, woa, wout)
        (y,) = _ffn(h2, vec(g_ffn2), w2g, w2u, w2d, g_final.reshape(1, -1), emit_normed=False)
        return y

    h1, u, z, q_lat, ckv_p, kr_p = per_token_front(x_prompt.reshape(bp * tp, d), cs_p)
    z_p = z.reshape(bp, tp, pw)
    pooled = _pool_prompt(z_p, wpool, vec(pool_scale)).reshape(bp * tp, pw)
    q_cat, k_cat, v = _qkv_prompt(q_lat, ckv_p, kr_p, wq_cat, wk, wv, qt_p)
    o = _attn_prompt(q_cat.reshape(bp, tp, -1), k_cat.reshape(bp, tp, -1), v.reshape(bp, tp, -1))
    y_p = per_token_back(u, pooled, o.reshape(bp * tp, -1), h1)

    h1, u, z, q_lat, ckv_s, kr_s = per_token_front(x_sample.reshape(bs * ts, d), cs_s)
    z_s = z.reshape(bs, ts, pw)
    pooled = _pool_sample(z_s, hist_s, wpool, vec(pool_scale), pos0=past)
    q_abs, q_rope = _q_sample(q_lat, wq_cat, wk, qt_s)
    o = _attn_sample(q_abs, q_rope, cache_ckv[0], cache_krope[0], ckv_s, kr_s, wv, ts=ts)
    y_s = per_token_back(u, pooled, o, h1)

    return (
        y_p.reshape(bp, tp, d),
        y_s.reshape(bs, ts, d),
        ckv_p.reshape(1, bp, tp, kl),
        kr_p[:, :QK_ROPE].reshape(1, bp, tp, QK_ROPE),
        z_p[None, :, tp - POOL_HIST:],
        ckv_s.reshape(1, bs, ts, kl),
        kr_s[:, :QK_ROPE].reshape(1, bs, ts, QK_ROPE),
        z_s[None, :, ts - POOL_HIST:],
    )
```

```python
import functools

import jax
import jax.numpy as jnp
from jax import lax
from jax.experimental import pallas as pl
from jax.experimental.pallas import tpu as pltpu

F32 = jnp.float32
BF16 = jnp.bfloat16

CHUNK_SHIFT = 6
N_HEADS = 16
QK_NOPE = 128
QK_ROPE = 64
V_HEAD = 128
HEAD_W = 256
LANES = 128
STRIP = 32
POOL_WINDOWS = (2, 4, 8, 16)
POOL_HIST = 15
POOL_PAD = 16
ROPE_THETA = 10000.0
EPS = 1e-6
SM_SCALE = (QK_NOPE + QK_ROPE) ** -0.5
LOG2_E = 1.4426950408889634
NEG_INF = -1e30
MIB = 1024 * 1024


def _dot(a, b):
    return jnp.dot(a, b, preferred_element_type=F32)


def _dot_t(a, b):
    return lax.dot_general(a, b, (((1,), (1,)), ((), ())), preferred_element_type=F32)


def _rms(x, g):
    return x * lax.rsqrt(jnp.mean(x * x, axis=-1, keepdims=True) + EPS) * g


def _params(semantics, vmem_mib):
    return pltpu.CompilerParams(dimension_semantics=semantics, vmem_limit_bytes=vmem_mib * MIB)


def _ffn_kernel(x_ref, g_ref, wg_ref, wu_ref, wd_ref, gn_ref, acc_ref, *rest, emit_normed):
    if emit_normed:
        u_ref, xn_ref = rest
    else:
        (xn_ref,) = rest
    j = pl.program_id(1)

    @pl.when(j == 0)
    def _():
        xn_ref[...] = _rms(x_ref[...], g_ref[...]).astype(BF16)
        acc_ref[...] = jnp.zeros_like(acc_ref)

    xn = xn_ref[...]
    gate = _dot(xn, wg_ref[...])
    up = _dot(xn, wu_ref[...])
    act = (gate * jax.nn.sigmoid(gate) * up).astype(BF16)
    acc_ref[...] += _dot(act, wd_ref[...])

    @pl.when(j == pl.num_programs(1) - 1)
    def _():
        h = x_ref[...] + 0.5 * acc_ref[...]
        if emit_normed:
            acc_ref[...] = h
            u_ref[...] = _rms(h, gn_ref[...]).astype(BF16)
        else:
            acc_ref[...] = _rms(h, gn_ref[...])


def _ffn(x, g, wg, wu, wd, g_next, *, emit_normed, tm=512, tf=512):
    n, d = x.shape
    f = wg.shape[1]
    row = lambda i, j: (i, 0)
    out_shape = [jax.ShapeDtypeStruct((n, d), F32)]
    out_specs = [pl.BlockSpec((tm, d), row)]
    if emit_normed:
        out_shape.append(jax.ShapeDtypeStruct((n, d), BF16))
        out_specs.append(pl.BlockSpec((tm, d), row))
    return pl.pallas_call(
        functools.partial(_ffn_kernel, emit_normed=emit_normed),
        grid=(n // tm, f // tf),
        in_specs=[
            pl.BlockSpec((tm, d), row),
            pl.BlockSpec((1, d), lambda i, j: (0, 0)),
            pl.BlockSpec((d, tf), lambda i, j: (0, j)),
            pl.BlockSpec((d, tf), lambda i, j: (0, j)),
            pl.BlockSpec((tf, d), lambda i, j: (j, 0)),
            pl.BlockSpec((1, d), lambda i, j: (0, 0)),
        ],
        out_specs=out_specs,
        out_shape=out_shape,
        scratch_shapes=[pltpu.VMEM((tm, d), BF16)],
        compiler_params=_params(("parallel", "arbitrary"), 48),
        name="ffn_norm" if emit_normed else "ffn_final",
    )(x, g, wg, wu, wd, g_next)


def _inproj_kernel(u_ref, w_ref, gq_ref, gkv_ref, cs_ref, z_ref, q_ref, ckv_ref, kr_ref, *, pw, ql, kl):
    proj = _dot(u_ref[...], w_ref[...])
    z_ref[...] = proj[:, :pw]
    q_ref[...] = _rms(proj[:, pw:pw + ql], gq_ref[...]).astype(BF16)
    ckv_ref[...] = _rms(proj[:, pw + ql:pw + ql + kl], gkv_ref[...])
    y = proj[:, pw + ql + kl:] * cs_ref[...]
    kr_ref[...] = y + pltpu.roll(y, QK_ROPE, axis=1)


def _inproj(u, w_small, g_q, g_kv, cs_tab, *, pw, ql, kl, tm=512):
    n, d = u.shape
    wn = w_small.shape[1]
    period = cs_tab.shape[0] // tm
    row = lambda i: (i, 0)
    fixed = lambda i: (0, 0)
    return pl.pallas_call(
        functools.partial(_inproj_kernel, pw=pw, ql=ql, kl=kl),
        grid=(n // tm,),
        in_specs=[
            pl.BlockSpec((tm, d), row),
            pl.BlockSpec((d, wn), fixed),
            pl.BlockSpec((1, ql), fixed),
            pl.BlockSpec((1, kl), fixed),
            pl.BlockSpec((tm, 2 * QK_ROPE), lambda i: (i % period, 0)),
        ],
        out_specs=[
            pl.BlockSpec((tm, pw), row),
            pl.BlockSpec((tm, ql), row),
            pl.BlockSpec((tm, kl), row),
            pl.BlockSpec((tm, 2 * QK_ROPE), row),
        ],
        out_shape=[
            jax.ShapeDtypeStruct((n, pw), F32),
            jax.ShapeDtypeStruct((n, ql), BF16),
            jax.ShapeDtypeStruct((n, kl), F32),
            jax.ShapeDtypeStruct((n, 2 * QK_ROPE), F32),
        ],
        compiler_params=_params(("parallel",), 48),
        name="in_proj",
    )(u, w_small, g_q, g_kv, cs_tab)


def _pool_rows(zext, pos0, w_ref, scale_ref, out_ref, out_row0, seg_rows, n_seg):
    p = zext.shape[1]
    gc = p // len(POOL_WINDOWS)
    ext = POOL_PAD + seg_rows
    pos = pos0 + lax.broadcasted_iota(jnp.int32, (seg_rows, 1), 0)
    for g, w in enumerate(POOL_WINDOWS):
        cols = slice(g * gc, (g + 1) * gc)
        s = zext[:, cols]
        win = s
        step = 1
        while step < w:
            win = win + pltpu.roll(win, step, axis=0)
            step *= 2
        cnt = jnp.minimum(pos + 1, w).astype(F32)
        for b in range(n_seg):
            lo = b * ext + POOL_PAD
            pooled = win[lo:lo + seg_rows] / cnt - s[lo:lo + seg_rows]
            mixed = _dot(pooled.astype(BF16), w_ref[g]) * scale_ref[:, cols]
            out_ref[pl.ds(out_row0 + b * seg_rows, seg_rows), cols] = mixed.astype(BF16)


def _pool_prompt_kernel(cur_ref, prev_ref, w_ref, scale_ref, out_ref, ext_ref, *, tp):
    t = pl.program_id(1)
    hist = jnp.where(t == 0, 0.0, prev_ref[0])
    ext_ref[0:POOL_PAD, :] = hist
    ext_ref[POOL_PAD:, :] = cur_ref[0]
    _pool_rows(ext_ref[...], t * tp, w_ref, scale_ref, out_ref.at[0], 0, tp, 1)


def _pool_prompt(z, w_pool, pool_scale, *, tp=512):
    b, t, p = z.shape
    hist_blocks = tp // POOL_PAD
    return pl.pallas_call(
        functools.partial(_pool_prompt_kernel, tp=tp),
        grid=(b, t // tp),
        in_specs=[
            pl.BlockSpec((1, tp, p), lambda i, j: (i, j, 0)),
            pl.BlockSpec((1, POOL_PAD, p), lambda i, j: (i, jnp.maximum(j * hist_blocks - 1, 0), 0)),
            pl.BlockSpec(w_pool.shape, lambda i, j: (0, 0, 0)),
            pl.BlockSpec((1, p), lambda i, j: (0, 0)),
        ],
        out_specs=pl.BlockSpec((1, tp, p), lambda i, j: (i, j, 0)),
        out_shape=jax.ShapeDtypeStruct((b, t, p), BF16),
        scratch_shapes=[pltpu.VMEM((POOL_PAD + tp, p), F32)],
        compiler_params=_params(("parallel", "arbitrary"), 32),
        name="pool_prompt",
    )(z, z, w_pool, pool_scale)


def _pool_sample_kernel(cur_ref, hist_ref, w_ref, scale_ref, out_ref, ext_ref, *, nb, ts, pos0):
    ext = POOL_PAD + ts
    for b in range(nb):
        ext_ref[b * ext:b * ext + POOL_PAD, :] = hist_ref[b]
        ext_ref[b * ext + POOL_PAD:(b + 1) * ext, :] = cur_ref[b]
    _pool_rows(ext_ref[...], pos0, w_ref, scale_ref, out_ref, 0, ts, nb)


def _pool_sample(z, hist, w_pool, pool_scale, *, pos0, nb=8):
    b, ts, p = z.shape
    return pl.pallas_call(
        functools.partial(_pool_sample_kernel, nb=nb, ts=ts, pos0=pos0),
        grid=(b // nb,),
        in_specs=[
            pl.BlockSpec((nb, ts, p), lambda i: (i, 0, 0)),
            pl.BlockSpec((nb, POOL_PAD, p), lambda i: (i, 0, 0)),
            pl.BlockSpec(w_pool.shape, lambda i: (0, 0, 0)),
            pl.BlockSpec((1, p), lambda i: (0, 0)),
        ],
        out_specs=pl.BlockSpec((nb * ts, p), lambda i: (i, 0)),
        out_shape=jax.ShapeDtypeStruct((b * ts, p), BF16),
        scratch_shapes=[pltpu.VMEM((nb * (POOL_PAD + ts), p), F32)],
        compiler_params=_params(("parallel",), 32),
        name="pool_sample",
    )(z, hist, w_pool, pool_scale)


def _qkv_prompt_kernel(ql_ref, ckv_ref, kr_ref, wq_ref, wk_ref, wv_ref, qt_ref, q_ref, k_ref, v_ref):
    ql = ql_ref[...]
    ckv = ckv_ref[...].astype(BF16)
    kr = kr_ref[...].astype(BF16)
    qt = qt_ref[...]
    for h in range(N_HEADS):
        q_ref[:, h * HEAD_W:(h + 1) * HEAD_W] = (_dot(ql, wq_ref[:, h * HEAD_W:(h + 1) * HEAD_W]) * qt).astype(BF16)
    kn = _dot(ckv, wk_ref[...])
    for h in range(N_HEADS):
        k_ref[:, h * HEAD_W:h * HEAD_W + QK_NOPE] = kn[:, h * QK_NOPE:(h + 1) * QK_NOPE].astype(BF16)
        k_ref[:, h * HEAD_W + QK_NOPE:(h + 1) * HEAD_W] = kr
    v_ref[...] = _dot(ckv, wv_ref[...]).astype(BF16)


def _qkv_prompt(q_lat, ckv, krblk, wq_cat, w_uk, w_uv, qt_tab, *, tm=512):
    n = q_lat.shape[0]
    period = qt_tab.shape[0] // tm
    row = lambda i: (i, 0)
    fixed = lambda i: (0, 0)
    return pl.pallas_call(
        _qkv_prompt_kernel,
        grid=(n // tm,),
        in_specs=[
            pl.BlockSpec((tm, q_lat.shape[1]), row),
            pl.BlockSpec((tm, ckv.shape[1]), row),
            pl.BlockSpec((tm, krblk.shape[1]), row),
            pl.BlockSpec(wq_cat.shape, fixed),
            pl.BlockSpec(w_uk.shape, fixed),
            pl.BlockSpec(w_uv.shape, fixed),
            pl.BlockSpec((tm, HEAD_W), lambda i: (i % period, 0)),
        ],
        out_specs=[
            pl.BlockSpec((tm, N_HEADS * HEAD_W), row),
            pl.BlockSpec((tm, N_HEADS * HEAD_W), row),
            pl.BlockSpec((tm, N_HEADS * V_HEAD), row),
        ],
        out_shape=[
            jax.ShapeDtypeStruct((n, N_HEADS * HEAD_W), BF16),
            jax.ShapeDtypeStruct((n, N_HEADS * HEAD_W), BF16),
            jax.ShapeDtypeStruct((n, N_HEADS * V_HEAD), BF16),
        ],
        compiler_params=_params(("parallel",), 48),
        name="qkv_prompt",
    )(q_lat, ckv, krblk, wq_cat, w_uk, w_uv, qt_tab)


def _softmax_strips(s_ref, p_ref, m_ref, mc_ref, a_ref, masked):
    blk = s_ref.shape[1]

    def strip(r0):
        if not masked:
            return s_ref[r0:r0 + STRIP, :], blk
        visible = ((r0 >> CHUNK_SHIFT) + 1) << CHUNK_SHIFT
        width = -(-visible // LANES) * LANES
        s = s_ref[r0:r0 + STRIP, :width]
        return jnp.where(lax.broadcasted_iota(jnp.int32, s.shape, 1) < visible, s, NEG_INF), width

    for r0 in range(0, blk, STRIP):
        s, _ = strip(r0)
        mc_ref[r0:r0 + STRIP, :] = jnp.broadcast_to(jnp.max(s, axis=-1, keepdims=True), (STRIP, LANES))
    m_old = m_ref[...]
    m_new = jnp.maximum(m_old, mc_ref[...])
    m_ref[...] = m_new
    a_ref[...] = jnp.exp2(m_old - m_new)
    for r0 in range(0, blk, STRIP):
        s, width = strip(r0)
        p_ref[r0:r0 + STRIP, :width] = jnp.exp2(s - jnp.tile(m_ref[r0:r0 + STRIP, :], (1, width // LANES))).astype(BF16)
        if width < blk:
            p_ref[r0:r0 + STRIP, width:] = jnp.zeros((STRIP, blk - width), BF16)


def _attn_prompt_kernel(q_ref, k_ref, v_ref, o_ref, s_ref, p_ref, m_ref, mc_ref, a_ref, acc_ref, *, blk, hb, hp):
    qi = pl.program_id(2)
    ones = jnp.ones((blk, V_HEAD), BF16)
    for h0 in range(0, hb, hp):
        heads = range(h0, h0 + hp)
        m_ref[...] = jnp.full(m_ref.shape, NEG_INF, F32)
        acc_ref[...] = jnp.zeros_like(acc_ref)

        def block(off, masked, heads=heads):
            for n, h in enumerate(heads):
                q = q_ref[0, :, h * HEAD_W:(h + 1) * HEAD_W]
                s_ref[n] = _dot_t(q, k_ref[0, pl.ds(off, blk), h * HEAD_W:(h + 1) * HEAD_W])
            for n in range(hp):
                _softmax_strips(s_ref.at[n], p_ref.at[n], m_ref.at[n], mc_ref.at[n], a_ref.at[n], masked)
            for n, h in enumerate(heads):
                v = v_ref[0, pl.ds(off, blk), h * V_HEAD:(h + 1) * V_HEAD]
                pv = _dot(p_ref[n], jnp.concatenate([v, ones], axis=1))
                acc_ref[n] = jnp.tile(a_ref[n], (1, 2)) * acc_ref[n] + pv

        def below_diagonal(j, carry):
            block(pl.multiple_of(j * blk, blk), masked=False)
            return carry

        lax.fori_loop(0, qi, below_diagonal, 0)
        block(pl.multiple_of(qi * blk, blk), masked=True)
        for n, h in enumerate(heads):
            acc = acc_ref[n]
            o_ref[0, :, h * V_HEAD:(h + 1) * V_HEAD] = (acc[:, :V_HEAD] * (1.0 / acc[:, V_HEAD:])).astype(BF16)


def _attn_prompt(q_cat, k_cat, v, *, blk=512, hb=4, hp=4):
    b, t, _ = q_cat.shape
    assert STRIP <= (1 << CHUNK_SHIFT) and blk % (1 << CHUNK_SHIFT) == 0
    return pl.pallas_call(
        functools.partial(_attn_prompt_kernel, blk=blk, hb=hb, hp=hp),
        grid=(b, N_HEADS // hb, t // blk),
        in_specs=[
            pl.BlockSpec((1, blk, hb * HEAD_W), lambda i, g, q: (i, q, g)),
            pl.BlockSpec((1, t, hb * HEAD_W), lambda i, g, q: (i, 0, g)),
            pl.BlockSpec((1, t, hb * V_HEAD), lambda i, g, q: (i, 0, g)),
        ],
        out_specs=pl.BlockSpec((1, blk, hb * V_HEAD), lambda i, g, q: (i, q, g)),
        out_shape=jax.ShapeDtypeStruct((b, t, N_HEADS * V_HEAD), BF16),
        scratch_shapes=[
            pltpu.VMEM((hp, blk, blk), F32),
            pltpu.VMEM((hp, blk, blk), BF16),
            pltpu.VMEM((hp, blk, LANES), F32),
            pltpu.VMEM((hp, blk, LANES), F32),
            pltpu.VMEM((hp, blk, LANES), F32),
            pltpu.VMEM((hp, blk, 2 * V_HEAD), F32),
        ],
        compiler_params=_params(("parallel", "parallel", "arbitrary"), 48),
        name="attn_prompt",
    )(q_cat, k_cat, v)


def _q_sample_kernel(ql_ref, wq_ref, wk_ref, qt_ref, qa_ref, qr_ref, *, kl):
    ql = ql_ref[...]
    qt = qt_ref[...]
    for h in range(N_HEADS):
        q = _dot(ql, wq_ref[:, h * HEAD_W:(h + 1) * HEAD_W]) * qt
        qn = q[:, :QK_NOPE].astype(BF16)
        qa_ref[:, h * kl:(h + 1) * kl] = _dot_t(qn, wk_ref[:, h * QK_NOPE:(h + 1) * QK_NOPE]).astype(BF16)
        y = q[:, QK_NOPE:]
        qr_ref[:, h * 2 * QK_ROPE:(h + 1) * 2 * QK_ROPE] = (y + pltpu.roll(y, QK_ROPE, axis=1)).astype(BF16)


def _q_sample(q_lat, wq_cat, w_uk, qt_tab, *, tm=512):
    n, ql = q_lat.shape
    kl = w_uk.shape[0]
    row = lambda i: (i, 0)
    fixed = lambda i: (0, 0)
    return pl.pallas_call(
        functools.partial(_q_sample_kernel, kl=kl),
        grid=(n // tm,),
        in_specs=[
            pl.BlockSpec((tm, ql), row),
            pl.BlockSpec(wq_cat.shape, fixed),
            pl.BlockSpec(w_uk.shape, fixed),
            pl.BlockSpec((tm, HEAD_W), row),
        ],
        out_specs=[
            pl.BlockSpec((tm, N_HEADS * kl), row),
            pl.BlockSpec((tm, N_HEADS * 2 * QK_ROPE), row),
        ],
        out_shape=[
            jax.ShapeDtypeStruct((n, N_HEADS * kl), BF16),
            jax.ShapeDtypeStruct((n, N_HEADS * 2 * QK_ROPE), BF16),
        ],
        compiler_params=_params(("parallel",), 48),
        name="q_sample",
    )(q_lat, wq_cat, w_uk, qt_tab)


def _attn_sample_kernel(qa_ref, qr_ref, cc_ref, ck_ref, nc_ref, nk_ref, wv_ref, o_ref, *, ts, past, kl):
    rows = N_HEADS * ts
    qs = jnp.concatenate([qa_ref[:, h * kl:(h + 1) * kl] for h in range(N_HEADS)], axis=0)
    qr = jnp.concatenate(
        [qr_ref[:, h * 2 * QK_ROPE:h * 2 * QK_ROPE + QK_ROPE] for h in range(N_HEADS)], axis=0)
    kc = cc_ref[0].astype(BF16)
    krc = ck_ref[0].astype(BF16)
    kn = nc_ref[...].astype(BF16)
    krn = nk_ref[:, :QK_ROPE].astype(BF16)
    s_c = _dot_t(qs, kc) + _dot_t(qr, krc)
    s_n = _dot_t(qs, kn) + _dot_t(qr, krn)
    q_chunk = (past + lax.broadcasted_iota(jnp.int32, (rows, 1), 0) % ts) >> CHUNK_SHIFT
    s_c = jnp.where((lax.broadcasted_iota(jnp.int32, (1, past), 1) >> CHUNK_SHIFT) <= q_chunk, s_c, NEG_INF)
    s_n = jnp.where(((past + lax.broadcasted_iota(jnp.int32, (1, ts), 1)) >> CHUNK_SHIFT) <= q_chunk, s_n, NEG_INF)
    m = jnp.maximum(jnp.max(s_c, axis=-1, keepdims=True), jnp.max(s_n, axis=-1, keepdims=True))
    p_c = jnp.exp2(s_c - m)
    p_n = jnp.exp2(s_n - m)
    l = jnp.sum(p_c, axis=-1, keepdims=True) + jnp.sum(p_n, axis=-1, keepdims=True)
    o_lat = ((_dot(p_c.astype(BF16), kc) + _dot(p_n.astype(BF16), kn)) * (1.0 / l)).astype(BF16)
    for h in range(N_HEADS):
        o_ref[:, h * V_HEAD:(h + 1) * V_HEAD] = _dot(
            o_lat[h * ts:(h + 1) * ts], wv_ref[:, h * V_HEAD:(h + 1) * V_HEAD]).astype(BF16)


def _attn_sample(q_abs, q_rope, cache_ckv, cache_kr, ckv_new, kr_new, w_uv, *, ts):
    b, past, kl = cache_ckv.shape
    row = lambda i: (i, 0)
    return pl.pallas_call(
        functools.partial(_attn_sample_kernel, ts=ts, past=past, kl=kl),
        grid=(b,),
        in_specs=[
            pl.BlockSpec((ts, q_abs.shape[1]), row),
            pl.BlockSpec((ts, q_rope.shape[1]), row),
            pl.BlockSpec((1, past, kl), lambda i: (i, 0, 0)),
            pl.BlockSpec((1, past, cache_kr.shape[2]), lambda i: (i, 0, 0)),
            pl.BlockSpec((ts, kl), row),
            pl.BlockSpec((ts, kr_new.shape[1]), row),
            pl.BlockSpec(w_uv.shape, lambda i: (0, 0)),
        ],
        out_specs=pl.BlockSpec((ts, N_HEADS * V_HEAD), row),
        out_shape=jax.ShapeDtypeStruct((b * ts, N_HEADS * V_HEAD), BF16),
        compiler_params=_params(("parallel",), 48),
        name="attn_sample",
    )(q_abs, q_rope, cache_ckv, cache_kr, ckv_new, kr_new, w_uv)


def _mix_out_kernel(u_ref, pa_ref, ob_ref, h_ref, wga_ref, wgb_ref, wpo_ref, woa_ref, wout_ref, acc_ref):
    j = pl.program_id(1)
    u = u_ref[...]
    gate_a = jax.nn.sigmoid(_dot(u, wga_ref[...]))
    gate_b = jax.nn.sigmoid(_dot(u, wgb_ref[...]))
    a = _dot(pa_ref[...], wpo_ref[...])
    b = _dot(ob_ref[...], woa_ref[...])
    merged = (gate_a * a + gate_b * b).astype(BF16)

    @pl.when(j == 0)
    def _():
        acc_ref[...] = h_ref[...]

    acc_ref[...] += _dot(merged, wout_ref[...])


def _mix_out(u, pooled, o_attn, h, w_gate, w_pool_out, w_o_attn, w_out, *, tm=512, tc=512):
    n, d = h.shape
    nc = d // tc
    row = lambda i, j: (i, 0)
    col = lambda i, j: (0, j)
    return pl.pallas_call(
        _mix_out_kernel,
        grid=(n // tm, nc),
        in_specs=[
            pl.BlockSpec((tm, d), row),
            pl.BlockSpec((tm, pooled.shape[1]), row),
            pl.BlockSpec((tm, o_attn.shape[1]), row),
            pl.BlockSpec((tm, d), row),
            pl.BlockSpec((d, tc), col),
            pl.BlockSpec((d, tc), lambda i, j: (0, j + nc)),
            pl.BlockSpec((w_pool_out.shape[0], tc), col),
            pl.BlockSpec((w_o_attn.shape[0], tc), col),
            pl.BlockSpec((tc, d), lambda i, j: (j, 0)),
        ],
        out_specs=pl.BlockSpec((tm, d), row),
        out_shape=jax.ShapeDtypeStruct((n, d), F32),
        compiler_params=_params(("parallel", "arbitrary"), 56),
        name="mix_out",
    )(u, pooled, o_attn, h, w_gate, w_gate, w_pool_out, w_o_attn, w_out)


def _rope_tables(pos):
    half = QK_ROPE // 2
    inv = ROPE_THETA ** (-jnp.arange(half, dtype=F32) * 2.0 / QK_ROPE)
    ang = pos.astype(F32)[:, None] * inv[None, :]
    c, s = jnp.cos(ang), jnp.sin(ang)
    cs = jnp.concatenate([c, c, -s, s], axis=-1)
    qt = (SM_SCALE * LOG2_E) * jnp.concatenate([jnp.ones((pos.shape[0], QK_NOPE), F32), cs], axis=-1)
    return cs, qt


def _dup_rope_cols(w):
    half = QK_ROPE // 2
    x1, x2 = w[..., :half], w[..., half:]
    return jnp.concatenate([x1, x2, x2, x1], axis=-1)


def kernel(x_prompt, x_sample, cache_ckv, cache_krope, state_pool, g_ffn1, w1_gate, w1_up, w1_down, g_mix, w_in, g_q_lat, g_kv_lat, w_uq, w_uk, w_uv, w_o_attn, w_pool, pool_scale, w_pool_out, w_out, g_ffn2, w2_gate, w2_up, w2_down, g_final):
    bp, tp, d = x_prompt.shape
    bs, ts, _ = x_sample.shape
    assert g_ffn1.shape[0] == 1, "single-layer stack only"
    assert ts >= POOL_HIST and tp >= POOL_HIST
    past = cache_ckv.shape[2]
    pw = pool_scale.shape[1]
    ql = g_q_lat.shape[1]
    kl = g_kv_lat.shape[1]
    o3 = pw + ql + kl
    o4 = o3 + QK_ROPE

    cs_p, qt_p = _rope_tables(jnp.arange(tp, dtype=jnp.int32))
    cs_s, qt_s = _rope_tables(past + jnp.arange(ts, dtype=jnp.int32))
    cs_s = jnp.tile(cs_s, (bs, 1))
    qt_s = jnp.tile(qt_s, (bs, 1))

    bf = lambda w: w[0].astype(BF16)
    vec = lambda g: g[0].reshape(1, -1)
    w_small = jnp.concatenate([w_in[0][:, :o3], _dup_rope_cols(w_in[0][:, o3:o4])], axis=1).astype(BF16)
    w_gate = w_in[0][:, o4:].astype(BF16)
    wq_cat = jnp.concatenate(
        [w_uq[0][..., :QK_NOPE], _dup_rope_cols(w_uq[0][..., QK_NOPE:])], axis=-1
    ).reshape(ql, N_HEADS * HEAD_W).astype(BF16)
    wk = w_uk[0].reshape(kl, N_HEADS * QK_NOPE).astype(BF16)
    wv = w_uv[0].reshape(kl, N_HEADS * V_HEAD).astype(BF16)
    w1g, w1u, w1d = bf(w1_gate), bf(w1_up), bf(w1_down)
    w2g, w2u, w2d = bf(w2_gate), bf(w2_up), bf(w2_down)
    wpool, wpo, woa, wout = bf(w_pool), bf(w_pool_out), bf(w_o_attn), bf(w_out)
    hist_s = jnp.pad(state_pool[0], ((0, 0), (POOL_PAD - POOL_HIST, 0), (0, 0)))

    def per_token_front(x, cs_tab):
        h1, u = _ffn(x, vec(g_ffn1), w1g, w1u, w1d, vec(g_mix), emit_normed=True)
        z, q_lat, ckv, krblk = _inproj(u, w_small, vec(g_q_lat), vec(g_kv_lat), cs_tab, pw=pw, ql=ql, kl=kl)
        return h1, u, z, q_lat, ckv, krblk

    def per_token_back(u, pooled, o, h1):
        h2 = _mix_out(u, pooled, o, h1, w_gate, wpo, woa, wout)
        (y,) = _ffn(h2, vec(g_ffn2), w2g, w2u, w2d, g_final.reshape(1, -1), emit_normed=False)
        return y

    h1, u, z, q_lat, ckv_p, kr_p = per_token_front(x_prompt.reshape(bp * tp, d), cs_p)
    z_p = z.reshape(bp, tp, pw)
    pooled = _pool_prompt(z_p, wpool, vec(pool_scale)).reshape(bp * tp, pw)
    q_cat, k_cat, v = _qkv_prompt(q_lat, ckv_p, kr_p, wq_cat, wk, wv, qt_p)
    o = _attn_prompt(q_cat.reshape(bp, tp, -1), k_cat.reshape(bp, tp, -1), v.reshape(bp, tp, -1))
    y_p = per_token_back(u, pooled, o.reshape(bp * tp, -1), h1)

    h1, u, z, q_lat, ckv_s, kr_s = per_token_front(x_sample.reshape(bs * ts, d), cs_s)
    z_s = z.reshape(bs, ts, pw)
    pooled = _pool_sample(z_s, hist_s, wpool, vec(pool_scale), pos0=past)
    q_abs, q_rope = _q_sample(q_lat, wq_cat, wk, qt_s)
    o = _attn_sample(q_abs, q_rope, cache_ckv[0], cache_krope[0], ckv_s, kr_s, wv, ts=ts)
    y_s = per_token_back(u, pooled, o, h1)

    return (
        y_p.reshape(bp, tp, d),
        y_s.reshape(bs, ts, d),
        ckv_p.reshape(1, bp, tp, kl),
        kr_p[:, :QK_ROPE].reshape(1, bp, tp, QK_ROPE),
        z_p[None, :, tp - POOL_HIST:],
        ckv_s.reshape(1, bs, ts, kl),
        kr_s[:, :QK_ROPE].reshape(1, bs, ts, QK_ROPE),
        z_s[None, :, ts - POOL_HIST:],
    )
```

```python
import functools

import jax
import jax.numpy as jnp
from jax import lax
from jax.experimental import pallas as pl
from jax.experimental.pallas import tpu as pltpu

F32 = jnp.float32
BF16 = jnp.bfloat16

CHUNK_SHIFT = 6
N_HEADS = 16
QK_NOPE = 128
QK_ROPE = 64
V_HEAD = 128
HEAD_W = 256
LANES = 128
STRIP = 32
POOL_WINDOWS = (2, 4, 8, 16)
POOL_HIST = 15
POOL_PAD = 16
ROPE_THETA = 10000.0
EPS = 1e-6
SM_SCALE = (QK_NOPE + QK_ROPE) ** -0.5
LOG2_E = 1.4426950408889634
NEG_INF = -1e30
MIB = 1024 * 1024


def _dot(a, b):
    return jnp.dot(a, b, preferred_element_type=F32)


def _dot_t(a, b):
    return lax.dot_general(a, b, (((1,), (1,)), ((), ())), preferred_element_type=F32)


def _rms(x, g):
    return x * lax.rsqrt(jnp.mean(x * x, axis=-1, keepdims=True) + EPS) * g


def _params(semantics, vmem_mib):
    return pltpu.CompilerParams(dimension_semantics=semantics, vmem_limit_bytes=vmem_mib * MIB)


def _ffn_kernel(x_ref, g_ref, wg_ref, wu_ref, wd_ref, gn_ref, acc_ref, *rest, emit_normed):
    if emit_normed:
        u_ref, xn_ref = rest
    else:
        (xn_ref,) = rest
    j = pl.program_id(1)

    @pl.when(j == 0)
    def _():
        xn_ref[...] = _rms(x_ref[...], g_ref[...]).astype(BF16)
        acc_ref[...] = jnp.zeros_like(acc_ref)

    xn = xn_ref[...]
    gate = _dot(xn, wg_ref[...])
    up = _dot(xn, wu_ref[...])
    act = (gate * jax.nn.sigmoid(gate) * up).astype(BF16)
    acc_ref[...] += _dot(act, wd_ref[...])

    @pl.when(j == pl.num_programs(1) - 1)
    def _():
        h = x_ref[...] + 0.5 * acc_ref[...]
        if emit_normed:
            acc_ref[...] = h
            u_ref[...] = _rms(h, gn_ref[...]).astype(BF16)
        else:
            acc_ref[...] = _rms(h, gn_ref[...])


def _ffn(x, g, wg, wu, wd, g_next, *, emit_normed, tm=1024, tf=256):
    n, d = x.shape
    f = wg.shape[1]
    row = lambda i, j: (i, 0)
    out_shape = [jax.ShapeDtypeStruct((n, d), F32)]
    out_specs = [pl.BlockSpec((tm, d), row)]
    if emit_normed:
        out_shape.append(jax.ShapeDtypeStruct((n, d), BF16))
        out_specs.append(pl.BlockSpec((tm, d), row))
    return pl.pallas_call(
        functools.partial(_ffn_kernel, emit_normed=emit_normed),
        grid=(n // tm, f // tf),
        in_specs=[
            pl.BlockSpec((tm, d), row),
            pl.BlockSpec((1, d), lambda i, j: (0, 0)),
            pl.BlockSpec((d, tf), lambda i, j: (0, j)),
            pl.BlockSpec((d, tf), lambda i, j: (0, j)),
            pl.BlockSpec((tf, d), lambda i, j: (j, 0)),
            pl.BlockSpec((1, d), lambda i, j: (0, 0)),
        ],
        out_specs=out_specs,
        out_shape=out_shape,
        scratch_shapes=[pltpu.VMEM((tm, d), BF16)],
        compiler_params=_params(("parallel", "arbitrary"), 62),
        name="ffn_norm" if emit_normed else "ffn_final",
    )(x, g, wg, wu, wd, g_next)


def _inproj_kernel(u_ref, w_ref, wkr_ref, gq_ref, gkv_ref, cs_ref, z_ref, q_ref, ckv_ref, kr_ref, *, pw, ql, kl):
    u = u_ref[...]
    proj = _dot(u, w_ref[...])
    z_ref[...] = proj[:, :pw]
    q_ref[...] = _rms(proj[:, pw:pw + ql], gq_ref[...]).astype(BF16)
    ckv_ref[...] = _rms(proj[:, pw + ql:pw + ql + kl], gkv_ref[...])
    y = _dot(u, wkr_ref[...]) * cs_ref[...]
    kr_ref[...] = y + pltpu.roll(y, QK_ROPE, axis=1)


def _inproj(u, w_main, w_kr, g_q, g_kv, cs_tab, *, pw, ql, kl, tm=1024):
    n, d = u.shape
    period = cs_tab.shape[0] // tm
    row = lambda i: (i, 0)
    fixed = lambda i: (0, 0)
    return pl.pallas_call(
        functools.partial(_inproj_kernel, pw=pw, ql=ql, kl=kl),
        grid=(n // tm,),
        in_specs=[
            pl.BlockSpec((tm, d), row),
            pl.BlockSpec((d, pw + ql + kl), fixed),
            pl.BlockSpec(w_kr.shape, fixed),
            pl.BlockSpec((1, ql), fixed),
            pl.BlockSpec((1, kl), fixed),
            pl.BlockSpec((tm, 2 * QK_ROPE), lambda i: (i % period, 0)),
        ],
        out_specs=[
            pl.BlockSpec((tm, pw), row),
            pl.BlockSpec((tm, ql), row),
            pl.BlockSpec((tm, kl), row),
            pl.BlockSpec((tm, 2 * QK_ROPE), row),
        ],
        out_shape=[
            jax.ShapeDtypeStruct((n, pw), F32),
            jax.ShapeDtypeStruct((n, ql), BF16),
            jax.ShapeDtypeStruct((n, kl), F32),
            jax.ShapeDtypeStruct((n, 2 * QK_ROPE), F32),
        ],
        compiler_params=_params(("parallel",), 48),
        name="in_proj",
    )(u, w_main, w_kr, g_q, g_kv, cs_tab)


def _pool_rows(zext, pos0, w_ref, scale_ref, out_ref, out_row0, seg_rows, n_seg):
    p = zext.shape[1]
    gc = p // len(POOL_WINDOWS)
    ext = POOL_PAD + seg_rows
    pos = pos0 + lax.broadcasted_iota(jnp.int32, (seg_rows, 1), 0)
    for g, w in enumerate(POOL_WINDOWS):
        cols = slice(g * gc, (g + 1) * gc)
        s = zext[:, cols]
        win = s
        step = 1
        while step < w:
            win = win + pltpu.roll(win, step, axis=0)
            step *= 2
        cnt = jnp.minimum(pos + 1, w).astype(F32)
        for b in range(n_seg):
            lo = b * ext + POOL_PAD
            pooled = win[lo:lo + seg_rows] / cnt - s[lo:lo + seg_rows]
            mixed = _dot(pooled.astype(BF16), w_ref[g]) * scale_ref[:, cols]
            out_ref[pl.ds(out_row0 + b * seg_rows, seg_rows), cols] = mixed.astype(BF16)


def _pool_prompt_kernel(cur_ref, prev_ref, w_ref, scale_ref, out_ref, ext_ref, *, tp):
    t = pl.program_id(1)
    hist = jnp.where(t == 0, 0.0, prev_ref[0])
    ext_ref[0:POOL_PAD, :] = hist
    ext_ref[POOL_PAD:, :] = cur_ref[0]
    _pool_rows(ext_ref[...], t * tp, w_ref, scale_ref, out_ref.at[0], 0, tp, 1)


def _pool_prompt(z, w_pool, pool_scale, *, tp=512):
    b, t, p = z.shape
    hist_blocks = tp // POOL_PAD
    return pl.pallas_call(
        functools.partial(_pool_prompt_kernel, tp=tp),
        grid=(b, t // tp),
        in_specs=[
            pl.BlockSpec((1, tp, p), lambda i, j: (i, j, 0)),
            pl.BlockSpec((1, POOL_PAD, p), lambda i, j: (i, jnp.maximum(j * hist_blocks - 1, 0), 0)),
            pl.BlockSpec(w_pool.shape, lambda i, j: (0, 0, 0)),
            pl.BlockSpec((1, p), lambda i, j: (0, 0)),
        ],
        out_specs=pl.BlockSpec((1, tp, p), lambda i, j: (i, j, 0)),
        out_shape=jax.ShapeDtypeStruct((b, t, p), BF16),
        scratch_shapes=[pltpu.VMEM((POOL_PAD + tp, p), F32)],
        compiler_params=_params(("parallel", "arbitrary"), 32),
        name="pool_prompt",
    )(z, z, w_pool, pool_scale)


def _pool_sample_kernel(cur_ref, hist_ref, w_ref, scale_ref, out_ref, ext_ref, *, nb, ts, pos0):
    ext = POOL_PAD + ts
    for b in range(nb):
        ext_ref[b * ext:b * ext + POOL_PAD, :] = hist_ref[b]
        ext_ref[b * ext + POOL_PAD:(b + 1) * ext, :] = cur_ref[b]
    _pool_rows(ext_ref[...], pos0, w_ref, scale_ref, out_ref, 0, ts, nb)


def _pool_sample(z, hist, w_pool, pool_scale, *, pos0, nb=8):
    b, ts, p = z.shape
    return pl.pallas_call(
        functools.partial(_pool_sample_kernel, nb=nb, ts=ts, pos0=pos0),
        grid=(b // nb,),
        in_specs=[
            pl.BlockSpec((nb, ts, p), lambda i: (i, 0, 0)),
            pl.BlockSpec((nb, POOL_PAD, p), lambda i: (i, 0, 0)),
            pl.BlockSpec(w_pool.shape, lambda i: (0, 0, 0)),
            pl.BlockSpec((1, p), lambda i: (0, 0)),
        ],
        out_specs=pl.BlockSpec((nb * ts, p), lambda i: (i, 0)),
        out_shape=jax.ShapeDtypeStruct((b * ts, p), BF16),
        scratch_shapes=[pltpu.VMEM((nb * (POOL_PAD + ts), p), F32)],
        compiler_params=_params(("parallel",), 32),
        name="pool_sample",
    )(z, hist, w_pool, pool_scale)


def _qkv_prompt_kernel(ql_ref, ckv_ref, kr_ref, wq_ref, wk_ref, wv_ref, qt_ref, q_ref, k_ref, v_ref):
    ql = ql_ref[...]
    ckv = ckv_ref[...].astype(BF16)
    kr = kr_ref[...].astype(BF16)
    qt = qt_ref[...]
    for h in range(N_HEADS):
        q_ref[:, h * HEAD_W:(h + 1) * HEAD_W] = (_dot(ql, wq_ref[:, h * HEAD_W:(h + 1) * HEAD_W]) * qt).astype(BF16)
    kn = _dot(ckv, wk_ref[...])
    for h in range(N_HEADS):
        k_ref[:, h * HEAD_W:h * HEAD_W + QK_NOPE] = kn[:, h * QK_NOPE:(h + 1) * QK_NOPE].astype(BF16)
        k_ref[:, h * HEAD_W + QK_NOPE:(h + 1) * HEAD_W] = kr
    v_ref[...] = _dot(ckv, wv_ref[...]).astype(BF16)


def _qkv_prompt(q_lat, ckv, krblk, wq_cat, w_uk, w_uv, qt_tab, *, tm=512):
    n = q_lat.shape[0]
    period = qt_tab.shape[0] // tm
    row = lambda i: (i, 0)
    fixed = lambda i: (0, 0)
    return pl.pallas_call(
        _qkv_prompt_kernel,
        grid=(n // tm,),
        in_specs=[
            pl.BlockSpec((tm, q_lat.shape[1]), row),
            pl.BlockSpec((tm, ckv.shape[1]), row),
            pl.BlockSpec((tm, krblk.shape[1]), row),
            pl.BlockSpec(wq_cat.shape, fixed),
            pl.BlockSpec(w_uk.shape, fixed),
            pl.BlockSpec(w_uv.shape, fixed),
            pl.BlockSpec((tm, HEAD_W), lambda i: (i % period, 0)),
        ],
        out_specs=[
            pl.BlockSpec((tm, N_HEADS * HEAD_W), row),
            pl.BlockSpec((tm, N_HEADS * HEAD_W), row),
            pl.BlockSpec((tm, N_HEADS * V_HEAD), row),
        ],
        out_shape=[
            jax.ShapeDtypeStruct((n, N_HEADS * HEAD_W), BF16),
            jax.ShapeDtypeStruct((n, N_HEADS * HEAD_W), BF16),
            jax.ShapeDtypeStruct((n, N_HEADS * V_HEAD), BF16),
        ],
        compiler_params=_params(("parallel",), 48),
        name="qkv_prompt",
    )(q_lat, ckv, krblk, wq_cat, w_uk, w_uv, qt_tab)


def _softmax_strips(s_ref, p_ref, m_ref, mc_ref, a_ref, masked):
    blk = s_ref.shape[1]

    def strip(r0):
        if not masked:
            return s_ref[r0:r0 + STRIP, :], blk
        visible = ((r0 >> CHUNK_SHIFT) + 1) << CHUNK_SHIFT
        width = -(-visible // LANES) * LANES
        s = s_ref[r0:r0 + STRIP, :width]
        return jnp.where(lax.broadcasted_iota(jnp.int32, s.shape, 1) < visible, s, NEG_INF), width

    for r0 in range(0, blk, STRIP):
        s, _ = strip(r0)
        mc_ref[r0:r0 + STRIP, :] = jnp.broadcast_to(jnp.max(s, axis=-1, keepdims=True), (STRIP, LANES))
    m_old = m_ref[...]
    m_new = jnp.maximum(m_old, mc_ref[...])
    m_ref[...] = m_new
    a_ref[...] = jnp.exp2(m_old - m_new)
    for r0 in range(0, blk, STRIP):
        s, width = strip(r0)
        p_ref[r0:r0 + STRIP, :width] = jnp.exp2(s - jnp.tile(m_ref[r0:r0 + STRIP, :], (1, width // LANES))).astype(BF16)
        if width < blk:
            p_ref[r0:r0 + STRIP, width:] = jnp.zeros((STRIP, blk - width), BF16)


def _attn_prompt_kernel(q_ref, k_ref, v_ref, o_ref, s_ref, p_ref, m_ref, mc_ref, a_ref, acc_ref, *, blk, hb, hp):
    qi = pl.program_id(2)
    ones = jnp.ones((blk, V_HEAD), BF16)
    for h0 in range(0, hb, hp):
        heads = range(h0, h0 + hp)
        m_ref[...] = jnp.full(m_ref.shape, NEG_INF, F32)
        acc_ref[...] = jnp.zeros_like(acc_ref)

        def block(off, masked, heads=heads):
            for n, h in enumerate(heads):
                q = q_ref[0, :, h * HEAD_W:(h + 1) * HEAD_W]
                s_ref[n] = _dot_t(q, k_ref[0, pl.ds(off, blk), h * HEAD_W:(h + 1) * HEAD_W])
            for n in range(hp):
                _softmax_strips(s_ref.at[n], p_ref.at[n], m_ref.at[n], mc_ref.at[n], a_ref.at[n], masked)
            for n, h in enumerate(heads):
                v = v_ref[0, pl.ds(off, blk), h * V_HEAD:(h + 1) * V_HEAD]
                pv = _dot(p_ref[n], jnp.concatenate([v, ones], axis=1))
                acc_ref[n] = jnp.tile(a_ref[n], (1, 2)) * acc_ref[n] + pv

        def below_diagonal(j, carry):
            block(pl.multiple_of(j * blk, blk), masked=False)
            return carry

        lax.fori_loop(0, qi, below_diagonal, 0)
        block(pl.multiple_of(qi * blk, blk), masked=True)
        for n, h in enumerate(heads):
            acc = acc_ref[n]
            o_ref[0, :, h * V_HEAD:(h + 1) * V_HEAD] = (acc[:, :V_HEAD] * (1.0 / acc[:, V_HEAD:])).astype(BF16)


def _attn_prompt(q_cat, k_cat, v, *, blk=512, hb=4, hp=4):
    b, t, _ = q_cat.shape
    assert STRIP <= (1 << CHUNK_SHIFT) and blk % (1 << CHUNK_SHIFT) == 0
    return pl.pallas_call(
        functools.partial(_attn_prompt_kernel, blk=blk, hb=hb, hp=hp),
        grid=(b, N_HEADS // hb, t // blk),
        in_specs=[
            pl.BlockSpec((1, blk, hb * HEAD_W), lambda i, g, q: (i, q, g)),
            pl.BlockSpec((1, t, hb * HEAD_W), lambda i, g, q: (i, 0, g)),
            pl.BlockSpec((1, t, hb * V_HEAD), lambda i, g, q: (i, 0, g)),
        ],
        out_specs=pl.BlockSpec((1, blk, hb * V_HEAD), lambda i, g, q: (i, q, g)),
        out_shape=jax.ShapeDtypeStruct((b, t, N_HEADS * V_HEAD), BF16),
        scratch_shapes=[
            pltpu.VMEM((hp, blk, blk), F32),
            pltpu.VMEM((hp, blk, blk), BF16),
            pltpu.VMEM((hp, blk, LANES), F32),
            pltpu.VMEM((hp, blk, LANES), F32),
            pltpu.VMEM((hp, blk, LANES), F32),
            pltpu.VMEM((hp, blk, 2 * V_HEAD), F32),
        ],
        compiler_params=_params(("parallel", "parallel", "arbitrary"), 48),
        name="attn_prompt",
    )(q_cat, k_cat, v)


def _q_sample_kernel(ql_ref, wq_ref, wk_ref, qt_ref, qa_ref, qr_ref, *, kl):
    ql = ql_ref[...]
    qt = qt_ref[...]
    for h in range(N_HEADS):
        q = _dot(ql, wq_ref[:, h * HEAD_W:(h + 1) * HEAD_W]) * qt
        qn = q[:, :QK_NOPE].astype(BF16)
        qa_ref[:, h * kl:(h + 1) * kl] = _dot_t(qn, wk_ref[:, h * QK_NOPE:(h + 1) * QK_NOPE]).astype(BF16)
        y = q[:, QK_NOPE:]
        qr_ref[:, h * 2 * QK_ROPE:(h + 1) * 2 * QK_ROPE] = (y + pltpu.roll(y, QK_ROPE, axis=1)).astype(BF16)


def _q_sample(q_lat, wq_cat, w_uk, qt_tab, *, tm=512):
    n, ql = q_lat.shape
    kl = w_uk.shape[0]
    row = lambda i: (i, 0)
    fixed = lambda i: (0, 0)
    return pl.pallas_call(
        functools.partial(_q_sample_kernel, kl=kl),
        grid=(n // tm,),
        in_specs=[
            pl.BlockSpec((tm, ql), row),
            pl.BlockSpec(wq_cat.shape, fixed),
            pl.BlockSpec(w_uk.shape, fixed),
            pl.BlockSpec((tm, HEAD_W), row),
        ],
        out_specs=[
            pl.BlockSpec((tm, N_HEADS * kl), row),
            pl.BlockSpec((tm, N_HEADS * 2 * QK_ROPE), row),
        ],
        out_shape=[
            jax.ShapeDtypeStruct((n, N_HEADS * kl), BF16),
            jax.ShapeDtypeStruct((n, N_HEADS * 2 * QK_ROPE), BF16),
        ],
        compiler_params=_params(("parallel",), 48),
        name="q_sample",
    )(q_lat, wq_cat, w_uk, qt_tab)


def _attn_sample_kernel(qa_ref, qr_ref, cc_ref, ck_ref, nc_ref, nk_ref, wv_ref, o_ref, *, ts, past, kl):
    rows = N_HEADS * ts
    qs = jnp.concatenate([qa_ref[:, h * kl:(h + 1) * kl] for h in range(N_HEADS)], axis=0)
    qr = jnp.concatenate(
        [qr_ref[:, h * 2 * QK_ROPE:h * 2 * QK_ROPE + QK_ROPE] for h in range(N_HEADS)], axis=0)
    kc = cc_ref[0].astype(BF16)
    krc = ck_ref[0].astype(BF16)
    kn = nc_ref[...].astype(BF16)
    krn = nk_ref[:, :QK_ROPE].astype(BF16)
    s_c = _dot_t(qs, kc) + _dot_t(qr, krc)
    s_n = _dot_t(qs, kn) + _dot_t(qr, krn)
    q_chunk = (past + lax.broadcasted_iota(jnp.int32, (rows, 1), 0) % ts) >> CHUNK_SHIFT
    s_c = jnp.where((lax.broadcasted_iota(jnp.int32, (1, past), 1) >> CHUNK_SHIFT) <= q_chunk, s_c, NEG_INF)
    s_n = jnp.where(((past + lax.broadcasted_iota(jnp.int32, (1, ts), 1)) >> CHUNK_SHIFT) <= q_chunk, s_n, NEG_INF)
    m = jnp.maximum(jnp.max(s_c, axis=-1, keepdims=True), jnp.max(s_n, axis=-1, keepdims=True))
    p_c = jnp.exp2(s_c - m)
    p_n = jnp.exp2(s_n - m)
    l = jnp.sum(p_c, axis=-1, keepdims=True) + jnp.sum(p_n, axis=-1, keepdims=True)
    o_lat = ((_dot(p_c.astype(BF16), kc) + _dot(p_n.astype(BF16), kn)) * (1.0 / l)).astype(BF16)
    for h in range(N_HEADS):
        o_ref[:, h * V_HEAD:(h + 1) * V_HEAD] = _dot(
            o_lat[h * ts:(h + 1) * ts], wv_ref[:, h * V_HEAD:(h + 1) * V_HEAD]).astype(BF16)


def _attn_sample(q_abs, q_rope, cache_ckv, cache_kr, ckv_new, kr_new, w_uv, *, ts):
    b, past, kl = cache_ckv.shape
    row = lambda i: (i, 0)
    return pl.pallas_call(
        functools.partial(_attn_sample_kernel, ts=ts, past=past, kl=kl),
        grid=(b,),
        in_specs=[
            pl.BlockSpec((ts, q_abs.shape[1]), row),
            pl.BlockSpec((ts, q_rope.shape[1]), row),
            pl.BlockSpec((1, past, kl), lambda i: (i, 0, 0)),
            pl.BlockSpec((1, past, cache_kr.shape[2]), lambda i: (i, 0, 0)),
            pl.BlockSpec((ts, kl), row),
            pl.BlockSpec((ts, kr_new.shape[1]), row),
            pl.BlockSpec(w_uv.shape, lambda i: (0, 0)),
        ],
        out_specs=pl.BlockSpec((ts, N_HEADS * V_HEAD), row),
        out_shape=jax.ShapeDtypeStruct((b * ts, N_HEADS * V_HEAD), BF16),
        compiler_params=_params(("parallel",), 48),
        name="attn_sample",
    )(q_abs, q_rope, cache_ckv, cache_kr, ckv_new, kr_new, w_uv)


def _mix_out_kernel(u_ref, pa_ref, ob_ref, h_ref, wga_ref, wgb_ref, wpo_ref, woa_ref, wout_ref, acc_ref):
    j = pl.program_id(1)
    u = u_ref[...]
    gate_a = jax.nn.sigmoid(_dot(u, wga_ref[...]))
    gate_b = jax.nn.sigmoid(_dot(u, wgb_ref[...]))
    a = _dot(pa_ref[...], wpo_ref[...])
    b = _dot(ob_ref[...], woa_ref[...])
    merged = (gate_a * a + gate_b * b).astype(BF16)

    @pl.when(j == 0)
    def _():
        acc_ref[...] = h_ref[...]

    acc_ref[...] += _dot(merged, wout_ref[...])


def _mix_out(u, pooled, o_attn, h, w_gate, w_pool_out, w_o_attn, w_out, *, tm=512, tc=512):
    n, d = h.shape
    nc = d // tc
    row = lambda i, j: (i, 0)
    col = lambda i, j: (0, j)
    return pl.pallas_call(
        _mix_out_kernel,
        grid=(n // tm, nc),
        in_specs=[
            pl.BlockSpec((tm, d), row),
            pl.BlockSpec((tm, pooled.shape[1]), row),
            pl.BlockSpec((tm, o_attn.shape[1]), row),
            pl.BlockSpec((tm, d), row),
            pl.BlockSpec((d, tc), col),
            pl.BlockSpec((d, tc), lambda i, j: (0, j + nc)),
            pl.BlockSpec((w_pool_out.shape[0], tc), col),
            pl.BlockSpec((w_o_attn.shape[0], tc), col),
            pl.BlockSpec((tc, d), lambda i, j: (j, 0)),
        ],
        out_specs=pl.BlockSpec((tm, d), row),
        out_shape=jax.ShapeDtypeStruct((n, d), F32),
        compiler_params=_params(("parallel", "arbitrary"), 56),
        name="mix_out",
    )(u, pooled, o_attn, h, w_gate, w_gate, w_pool_out, w_o_attn, w_out)


def _rope_tables(pos):
    half = QK_ROPE // 2
    inv = ROPE_THETA ** (-jnp.arange(half, dtype=F32) * 2.0 / QK_ROPE)
    ang = pos.astype(F32)[:, None] * inv[None, :]
    c, s = jnp.cos(ang), jnp.sin(ang)
    cs = jnp.concatenate([c, c, -s, s], axis=-1)
    qt = (SM_SCALE * LOG2_E) * jnp.concatenate([jnp.ones((pos.shape[0], QK_NOPE), F32), cs], axis=-1)
    return cs, qt


def _dup_rope_cols(w):
    half = QK_ROPE // 2
    x1, x2 = w[..., :half], w[..., half:]
    return jnp.concatenate([x1, x2, x2, x1], axis=-1)


def kernel(x_prompt, x_sample, cache_ckv, cache_krope, state_pool, g_ffn1, w1_gate, w1_up, w1_down, g_mix, w_in, g_q_lat, g_kv_lat, w_uq, w_uk, w_uv, w_o_attn, w_pool, pool_scale, w_pool_out, w_out, g_ffn2, w2_gate, w2_up, w2_down, g_final):
    bp, tp, d = x_prompt.shape
    bs, ts, _ = x_sample.shape
    assert g_ffn1.shape[0] == 1, "single-layer stack only"
    assert ts >= POOL_HIST and tp >= POOL_HIST
    past = cache_ckv.shape[2]
    pw = pool_scale.shape[1]
    ql = g_q_lat.shape[1]
    kl = g_kv_lat.shape[1]
    o3 = pw + ql + kl
    o4 = o3 + QK_ROPE

    cs_p, qt_p = _rope_tables(jnp.arange(tp, dtype=jnp.int32))
    cs_s, qt_s = _rope_tables(past + jnp.arange(ts, dtype=jnp.int32))
    cs_s = jnp.tile(cs_s, (bs, 1))
    qt_s = jnp.tile(qt_s, (bs, 1))

    bf = lambda w: w[0].astype(BF16)
    vec = lambda g: g[0].reshape(1, -1)
    w_main = w_in[0][:, :o3].astype(BF16)
    w_kr = _dup_rope_cols(w_in[0][:, o3:o4]).astype(BF16)
    w_gate = w_in[0][:, o4:].astype(BF16)
    wq_cat = jnp.concatenate(
        [w_uq[0][..., :QK_NOPE], _dup_rope_cols(w_uq[0][..., QK_NOPE:])], axis=-1
    ).reshape(ql, N_HEADS * HEAD_W).astype(BF16)
    wk = w_uk[0].reshape(kl, N_HEADS * QK_NOPE).astype(BF16)
    wv = w_uv[0].reshape(kl, N_HEADS * V_HEAD).astype(BF16)
    w1g, w1u, w1d = bf(w1_gate), bf(w1_up), bf(w1_down)
    w2g, w2u, w2d = bf(w2_gate), bf(w2_up), bf(w2_down)
    wpool, wpo, woa, wout = bf(w_pool), bf(w_pool_out), bf(w_o_attn), bf(w_out)
    hist_s = jnp.pad(state_pool[0], ((0, 0), (POOL_PAD - POOL_HIST, 0), (0, 0)))

    def per_token_front(x, cs_tab):
        h1, u = _ffn(x, vec(g_ffn1), w1g, w1u, w1d, vec(g_mix), emit_normed=True)
        z, q_lat, ckv, krblk = _inproj(u, w_main, w_kr, vec(g_q_lat), vec(g_kv_lat), cs_tab, pw=pw, ql=ql, kl=kl)
        return h1, u, z, q_lat, ckv, krblk

    def per_token_back(u, pooled, o, h1):
        h2 = _mix_out(u, pooled, o, h1, w_gate, wpo, woa, wout)
        (y,) = _ffn(h2, vec(g_ffn2), w2g, w2u, w2d, g_final.reshape(1, -1), emit_normed=False)
        return y

    h1, u, z, q_lat, ckv_p, kr_p = per_token_front(x_prompt.reshape(bp * tp, d), cs_p)
    z_p = z.reshape(bp, tp, pw)
    pooled = _pool_prompt(z_p, wpool, vec(pool_scale)).reshape(bp * tp, pw)
    q_cat, k_cat, v = _qkv_prompt(q_lat, ckv_p, kr_p, wq_cat, wk, wv, qt_p)
    o = _attn_prompt(q_cat.reshape(bp, tp, -1), k_cat.reshape(bp, tp, -1), v.reshape(bp, tp, -1))
    y_p = per_token_back(u, pooled, o.reshape(bp * tp, -1), h1)

    h1, u, z, q_lat, ckv_s, kr_s = per_token_front(x_sample.reshape(bs * ts, d), cs_s)
    z_s = z.reshape(bs, ts, pw)
    pooled = _pool_sample(z_s, hist_s, wpool, vec(pool_scale), pos0=past)
    q_abs, q_rope = _q_sample(q_lat, wq_cat, wk, qt_s)
    o = _attn_sample(q_abs, q_rope, cache_ckv[0], cache_krope[0], ckv_s, kr_s, wv, ts=ts)
    y_s = per_token_back(u, pooled, o, h1)

    return (
        y_p.reshape(bp, tp, d),
        y_s.reshape(bs, ts, d),
        ckv_p.reshape(1, bp, tp, kl),
        kr_p[:, :QK_ROPE].reshape(1, bp, tp, QK_ROPE),
        z_p[None, :, tp - POOL_HIST:],
        ckv_s.reshape(1, bs, ts, kl),
        kr_s[:, :QK_ROPE].reshape(1, bs, ts, QK_ROPE),
        z_s[None, :, ts - POOL_HIST:],
    )
```

```python
import functools

import jax
import jax.numpy as jnp
from jax import lax
from jax.experimental import pallas as pl
from jax.experimental.pallas import tpu as pltpu

F32 = jnp.float32
BF16 = jnp.bfloat16

CHUNK_SHIFT = 6
N_HEADS = 16
QK_NOPE = 128
QK_ROPE = 64
V_HEAD = 128
HEAD_W = 256
LANES = 128
BF16_ROWS = 16
STRIP = 32
POOL_WINDOWS = (2, 4, 8, 16)
POOL_HIST = 15
POOL_PAD = 16
ROPE_THETA = 10000.0
EPS = 1e-6
SM_SCALE = (QK_NOPE + QK_ROPE) ** -0.5
LOG2_E = 1.4426950408889634
NEG_INF = -1e30
MIB = 1024 * 1024


def _dot(a, b):
    return jnp.dot(a, b, preferred_element_type=F32)


def _dot_t(a, b):
    return lax.dot_general(a, b, (((1,), (1,)), ((), ())), preferred_element_type=F32)


def _rms(x, g):
    return x * lax.rsqrt(jnp.mean(x * x, axis=-1, keepdims=True) + EPS) * g


def _params(semantics, vmem_mib):
    return pltpu.CompilerParams(dimension_semantics=semantics, vmem_limit_bytes=vmem_mib * MIB)


def _ffn_kernel(x_ref, g_ref, wg_ref, wu_ref, wd_ref, gn_ref, acc_ref, *rest, emit_normed, cast_weights):
    rest = list(rest)
    u_ref = rest.pop(0) if emit_normed else None
    xn_ref = rest.pop()
    j = pl.program_id(1)

    @pl.when(j == 0)
    def _():
        xn_ref[...] = _rms(x_ref[...], g_ref[...]).astype(BF16)
        acc_ref[...] = jnp.zeros_like(acc_ref)

    wg, wu, wd = wg_ref[...], wu_ref[...], wd_ref[...]
    if cast_weights:
        wg, wu, wd = wg.astype(BF16), wu.astype(BF16), wd.astype(BF16)
        for dst, w in zip(rest, (wg, wu, wd)):
            dst[...] = w
    xn = xn_ref[...]
    gate = _dot(xn, wg)
    up = _dot(xn, wu)
    act = (gate * jax.nn.sigmoid(gate) * up).astype(BF16)
    acc_ref[...] += _dot(act, wd)

    @pl.when(j == pl.num_programs(1) - 1)
    def _():
        h = x_ref[...] + 0.5 * acc_ref[...]
        if emit_normed:
            acc_ref[...] = h
            u_ref[...] = _rms(h, gn_ref[...]).astype(BF16)
        else:
            acc_ref[...] = _rms(h, gn_ref[...])


def _ffn(x, g, wg, wu, wd, g_next, *, emit_normed, cast_weights=False, tm=1024, tf=256):
    n, d = x.shape
    f = wg.shape[1]
    row = lambda i, j: (i, 0)
    assert not cast_weights or n == tm
    rows_mode = dict(pipeline_mode=pl.Buffered(1)) if n == tm else {}
    w_up_spec = pl.BlockSpec((d, tf), lambda i, j: (0, j))
    w_down_spec = pl.BlockSpec((tf, d), lambda i, j: (j, 0))
    out_shape = [jax.ShapeDtypeStruct((n, d), F32)]
    out_specs = [pl.BlockSpec((tm, d), row, **rows_mode)]
    if emit_normed:
        out_shape.append(jax.ShapeDtypeStruct((n, d), BF16))
        out_specs.append(pl.BlockSpec((tm, d), row, **rows_mode))
    if cast_weights:
        out_shape += [jax.ShapeDtypeStruct(w.shape, BF16) for w in (wg, wu, wd)]
        out_specs += [w_up_spec, w_up_spec, w_down_spec]
    return pl.pallas_call(
        functools.partial(_ffn_kernel, emit_normed=emit_normed, cast_weights=cast_weights),
        grid=(n // tm, f // tf),
        in_specs=[
            pl.BlockSpec((tm, d), row, **rows_mode),
            pl.BlockSpec((1, d), lambda i, j: (0, 0)),
            w_up_spec,
            w_up_spec,
            w_down_spec,
            pl.BlockSpec((1, d), lambda i, j: (0, 0)),
        ],
        out_specs=out_specs,
        out_shape=out_shape,
        scratch_shapes=[pltpu.VMEM((tm, d), BF16)],
        compiler_params=_params(("parallel", "arbitrary"), 62),
        name="ffn_norm" if emit_normed else "ffn_final",
    )(x, g, wg, wu, wd, g_next)


def _inproj_kernel(u_ref, w_ref, wkr_ref, gq_ref, gkv_ref, cs_ref, z_ref, q_ref, ckv_ref, kr_ref, *, pw, ql, kl):
    u = u_ref[...]
    proj = _dot(u, w_ref[...])
    z_ref[...] = proj[:, :pw]
    q_ref[...] = _rms(proj[:, pw:pw + ql], gq_ref[...]).astype(BF16)
    ckv_ref[...] = _rms(proj[:, pw + ql:pw + ql + kl], gkv_ref[...])
    y = _dot(u, wkr_ref[...]) * cs_ref[...]
    kr_ref[...] = y + pltpu.roll(y, QK_ROPE, axis=1)


def _inproj(u, w_main, w_kr, g_q, g_kv, cs_tab, *, pw, ql, kl, tm=1024):
    n, d = u.shape
    period = cs_tab.shape[0] // tm
    row = lambda i: (i, 0)
    fixed = lambda i: (0, 0)
    return pl.pallas_call(
        functools.partial(_inproj_kernel, pw=pw, ql=ql, kl=kl),
        grid=(n // tm,),
        in_specs=[
            pl.BlockSpec((tm, d), row),
            pl.BlockSpec((d, pw + ql + kl), fixed),
            pl.BlockSpec(w_kr.shape, fixed),
            pl.BlockSpec((1, ql), fixed),
            pl.BlockSpec((1, kl), fixed),
            pl.BlockSpec((tm, 2 * QK_ROPE), lambda i: (i % period, 0)),
        ],
        out_specs=[
            pl.BlockSpec((tm, pw), row),
            pl.BlockSpec((tm, ql), row),
            pl.BlockSpec((tm, kl), row),
            pl.BlockSpec((tm, 2 * QK_ROPE), row),
        ],
        out_shape=[
            jax.ShapeDtypeStruct((n, pw), F32),
            jax.ShapeDtypeStruct((n, ql), BF16),
            jax.ShapeDtypeStruct((n, kl), F32),
            jax.ShapeDtypeStruct((n, 2 * QK_ROPE), F32),
        ],
        compiler_params=_params(("parallel",), 48),
        name="in_proj",
    )(u, w_main, w_kr, g_q, g_kv, cs_tab)


def _pool_rows(zext, pos0, w_ref, scale_ref, out_ref, out_row0, seg_rows, n_seg):
    p = zext.shape[1]
    gc = p // len(POOL_WINDOWS)
    ext = POOL_PAD + seg_rows
    pos = pos0 + lax.broadcasted_iota(jnp.int32, (seg_rows, 1), 0)
    for g, w in enumerate(POOL_WINDOWS):
        cols = slice(g * gc, (g + 1) * gc)
        s = zext[:, cols]
        win = s
        step = 1
        while step < w:
            win = win + pltpu.roll(win, step, axis=0)
            step *= 2
        cnt = jnp.minimum(pos + 1, w).astype(F32)
        for b in range(n_seg):
            lo = b * ext + POOL_PAD
            pooled = win[lo:lo + seg_rows] / cnt - s[lo:lo + seg_rows]
            mixed = _dot(pooled.astype(BF16), w_ref[g]) * scale_ref[:, cols]
            out_ref[pl.ds(out_row0 + b * seg_rows, seg_rows), cols] = mixed.astype(BF16)


def _pool_prompt_kernel(cur_ref, prev_ref, w_ref, scale_ref, out_ref, ext_ref, *, tp):
    t = pl.program_id(1)
    hist = jnp.where(t == 0, 0.0, prev_ref[0])
    ext_ref[0:POOL_PAD, :] = hist
    ext_ref[POOL_PAD:, :] = cur_ref[0]
    _pool_rows(ext_ref[...], t * tp, w_ref, scale_ref, out_ref.at[0], 0, tp, 1)


def _pool_prompt(z, w_pool, pool_scale, *, tp=512):
    b, t, p = z.shape
    hist_blocks = tp // POOL_PAD
    return pl.pallas_call(
        functools.partial(_pool_prompt_kernel, tp=tp),
        grid=(b, t // tp),
        in_specs=[
            pl.BlockSpec((1, tp, p), lambda i, j: (i, j, 0)),
            pl.BlockSpec((1, POOL_PAD, p), lambda i, j: (i, jnp.maximum(j * hist_blocks - 1, 0), 0)),
            pl.BlockSpec(w_pool.shape, lambda i, j: (0, 0, 0)),
            pl.BlockSpec((1, p), lambda i, j: (0, 0)),
        ],
        out_specs=pl.BlockSpec((1, tp, p), lambda i, j: (i, j, 0)),
        out_shape=jax.ShapeDtypeStruct((b, t, p), BF16),
        scratch_shapes=[pltpu.VMEM((POOL_PAD + tp, p), F32)],
        compiler_params=_params(("parallel", "arbitrary"), 32),
        name="pool_prompt",
    )(z, z, w_pool, pool_scale)


def _pool_sample_kernel(cur_ref, hist_ref, w_ref, scale_ref, out_ref, ext_ref, *, nb, ts, pos0):
    ext = POOL_PAD + ts
    for b in range(nb):
        ext_ref[b * ext:b * ext + POOL_PAD, :] = hist_ref[b]
        ext_ref[b * ext + POOL_PAD:(b + 1) * ext, :] = cur_ref[b]
    _pool_rows(ext_ref[...], pos0, w_ref, scale_ref, out_ref, 0, ts, nb)


def _pool_sample(z, hist, w_pool, pool_scale, *, pos0, nb=8):
    b, ts, p = z.shape
    return pl.pallas_call(
        functools.partial(_pool_sample_kernel, nb=nb, ts=ts, pos0=pos0),
        grid=(b // nb,),
        in_specs=[
            pl.BlockSpec((nb, ts, p), lambda i: (i, 0, 0)),
            pl.BlockSpec((nb, POOL_PAD, p), lambda i: (i, 0, 0)),
            pl.BlockSpec(w_pool.shape, lambda i: (0, 0, 0)),
            pl.BlockSpec((1, p), lambda i: (0, 0)),
        ],
        out_specs=pl.BlockSpec((nb * ts, p), lambda i: (i, 0)),
        out_shape=jax.ShapeDtypeStruct((b * ts, p), BF16),
        scratch_shapes=[pltpu.VMEM((nb * (POOL_PAD + ts), p), F32)],
        compiler_params=_params(("parallel",), 32),
        name="pool_sample",
    )(z, hist, w_pool, pool_scale)


def _qkv_prompt_kernel(ql_ref, ckv_ref, kr_ref, wq_ref, wk_ref, wv_ref, qt_ref, q_ref, k_ref, v_ref):
    ql = ql_ref[...]
    ckv = ckv_ref[...].astype(BF16)
    kr = kr_ref[...].astype(BF16)
    qt = qt_ref[...]
    for h in range(N_HEADS):
        q_ref[:, h * HEAD_W:(h + 1) * HEAD_W] = (_dot(ql, wq_ref[:, h * HEAD_W:(h + 1) * HEAD_W]) * qt).astype(BF16)
    kn = _dot(ckv, wk_ref[...])
    for h in range(N_HEADS):
        k_ref[:, h * HEAD_W:h * HEAD_W + QK_NOPE] = kn[:, h * QK_NOPE:(h + 1) * QK_NOPE].astype(BF16)
        k_ref[:, h * HEAD_W + QK_NOPE:(h + 1) * HEAD_W] = kr
    v_ref[...] = _dot(ckv, wv_ref[...]).astype(BF16)


def _qkv_prompt(q_lat, ckv, krblk, wq_cat, w_uk, w_uv, qt_tab, *, tm=512):
    n = q_lat.shape[0]
    period = qt_tab.shape[0] // tm
    row = lambda i: (i, 0)
    fixed = lambda i: (0, 0)
    return pl.pallas_call(
        _qkv_prompt_kernel,
        grid=(n // tm,),
        in_specs=[
            pl.BlockSpec((tm, q_lat.shape[1]), row),
            pl.BlockSpec((tm, ckv.shape[1]), row),
            pl.BlockSpec((tm, krblk.shape[1]), row),
            pl.BlockSpec(wq_cat.shape, fixed),
            pl.BlockSpec(w_uk.shape, fixed),
            pl.BlockSpec(w_uv.shape, fixed),
            pl.BlockSpec((tm, HEAD_W), lambda i: (i % period, 0)),
        ],
        out_specs=[
            pl.BlockSpec((tm, N_HEADS * HEAD_W), row),
            pl.BlockSpec((tm, N_HEADS * HEAD_W), row),
            pl.BlockSpec((tm, N_HEADS * V_HEAD), row),
        ],
        out_shape=[
            jax.ShapeDtypeStruct((n, N_HEADS * HEAD_W), BF16),
            jax.ShapeDtypeStruct((n, N_HEADS * HEAD_W), BF16),
            jax.ShapeDtypeStruct((n, N_HEADS * V_HEAD), BF16),
        ],
        compiler_params=_params(("parallel",), 48),
        name="qkv_prompt",
    )(q_lat, ckv, krblk, wq_cat, w_uk, w_uv, qt_tab)


def _softmax_strips(s_ref, p_ref, m_ref, mc_ref, a_ref, masked):
    blk = s_ref.shape[1]

    def strip(r0):
        if not masked:
            return s_ref[r0:r0 + STRIP, :], blk
        visible = ((r0 >> CHUNK_SHIFT) + 1) << CHUNK_SHIFT
        width = -(-visible // LANES) * LANES
        s = s_ref[r0:r0 + STRIP, :width]
        return jnp.where(lax.broadcasted_iota(jnp.int32, s.shape, 1) < visible, s, NEG_INF), width

    for r0 in range(0, blk, STRIP):
        s, _ = strip(r0)
        mc_ref[r0:r0 + STRIP, :] = jnp.broadcast_to(jnp.max(s, axis=-1, keepdims=True), (STRIP, LANES))
    m_old = m_ref[...]
    m_new = jnp.maximum(m_old, mc_ref[...])
    m_ref[...] = m_new
    a_ref[...] = jnp.exp2(m_old - m_new)
    for r0 in range(0, blk, STRIP):
        s, width = strip(r0)
        p_ref[r0:r0 + STRIP, :width] = jnp.exp2(s - jnp.tile(m_ref[r0:r0 + STRIP, :], (1, width // LANES))).astype(BF16)
        if width < blk:
            p_ref[r0:r0 + STRIP, width:] = jnp.zeros((STRIP, blk - width), BF16)


def _attn_prompt_kernel(q_ref, k_ref, v_ref, o_ref, s_ref, p_ref, m_ref, mc_ref, a_ref, acc_ref, *, blk, hb, hp):
    qi = pl.program_id(2)
    ones = jnp.ones((blk, V_HEAD), BF16)
    for h0 in range(0, hb, hp):
        heads = range(h0, h0 + hp)
        m_ref[...] = jnp.full(m_ref.shape, NEG_INF, F32)
        acc_ref[...] = jnp.zeros_like(acc_ref)

        def block(off, masked, heads=heads):
            for n, h in enumerate(heads):
                q = q_ref[0, :, h * HEAD_W:(h + 1) * HEAD_W]
                s_ref[n] = _dot_t(q, k_ref[0, pl.ds(off, blk), h * HEAD_W:(h + 1) * HEAD_W])
            for n in range(hp):
                _softmax_strips(s_ref.at[n], p_ref.at[n], m_ref.at[n], mc_ref.at[n], a_ref.at[n], masked)
            for n, h in enumerate(heads):
                v = v_ref[0, pl.ds(off, blk), h * V_HEAD:(h + 1) * V_HEAD]
                pv = _dot(p_ref[n], jnp.concatenate([v, ones], axis=1))
                acc_ref[n] = jnp.tile(a_ref[n], (1, 2)) * acc_ref[n] + pv

        def below_diagonal(j, carry):
            block(pl.multiple_of(j * blk, blk), masked=False)
            return carry

        lax.fori_loop(0, qi, below_diagonal, 0)
        block(pl.multiple_of(qi * blk, blk), masked=True)
        for n, h in enumerate(heads):
            acc = acc_ref[n]
            o_ref[0, :, h * V_HEAD:(h + 1) * V_HEAD] = (acc[:, :V_HEAD] * (1.0 / acc[:, V_HEAD:])).astype(BF16)


def _attn_prompt(q_cat, k_cat, v, *, blk=512, hb=4, hp=4):
    b, t, _ = q_cat.shape
    assert STRIP <= (1 << CHUNK_SHIFT) and blk % (1 << CHUNK_SHIFT) == 0
    return pl.pallas_call(
        functools.partial(_attn_prompt_kernel, blk=blk, hb=hb, hp=hp),
        grid=(b, N_HEADS // hb, t // blk),
        in_specs=[
            pl.BlockSpec((1, blk, hb * HEAD_W), lambda i, g, q: (i, q, g)),
            pl.BlockSpec((1, t, hb * HEAD_W), lambda i, g, q: (i, 0, g)),
            pl.BlockSpec((1, t, hb * V_HEAD), lambda i, g, q: (i, 0, g)),
        ],
        out_specs=pl.BlockSpec((1, blk, hb * V_HEAD), lambda i, g, q: (i, q, g)),
        out_shape=jax.ShapeDtypeStruct((b, t, N_HEADS * V_HEAD), BF16),
        scratch_shapes=[
            pltpu.VMEM((hp, blk, blk), F32),
            pltpu.VMEM((hp, blk, blk), BF16),
            pltpu.VMEM((hp, blk, LANES), F32),
            pltpu.VMEM((hp, blk, LANES), F32),
            pltpu.VMEM((hp, blk, LANES), F32),
            pltpu.VMEM((hp, blk, 2 * V_HEAD), F32),
        ],
        compiler_params=_params(("parallel", "parallel", "arbitrary"), 48),
        name="attn_prompt",
    )(q_cat, k_cat, v)


def _q_sample_kernel(ql_ref, wq_ref, wk_ref, qt_ref, qa_ref, qr_ref, *, kl):
    ql = ql_ref[...]
    qt = qt_ref[...]
    for h in range(N_HEADS):
        q = _dot(ql, wq_ref[:, h * HEAD_W:(h + 1) * HEAD_W]) * qt
        qn = q[:, :QK_NOPE].astype(BF16)
        qa_ref[:, h * kl:(h + 1) * kl] = _dot_t(qn, wk_ref[:, h * QK_NOPE:(h + 1) * QK_NOPE]).astype(BF16)
        y = q[:, QK_NOPE:]
        qr_ref[:, h * 2 * QK_ROPE:(h + 1) * 2 * QK_ROPE] = (y + pltpu.roll(y, QK_ROPE, axis=1)).astype(BF16)


def _q_sample(q_lat, wq_cat, w_uk, qt_tab, *, tm=512):
    n, ql = q_lat.shape
    kl = w_uk.shape[0]
    row = lambda i: (i, 0)
    fixed = lambda i: (0, 0)
    return pl.pallas_call(
        functools.partial(_q_sample_kernel, kl=kl),
        grid=(n // tm,),
        in_specs=[
            pl.BlockSpec((tm, ql), row),
            pl.BlockSpec(wq_cat.shape, fixed),
            pl.BlockSpec(w_uk.shape, fixed),
            pl.BlockSpec((tm, HEAD_W), row),
        ],
        out_specs=[
            pl.BlockSpec((tm, N_HEADS * kl), row),
            pl.BlockSpec((tm, N_HEADS * 2 * QK_ROPE), row),
        ],
        out_shape=[
            jax.ShapeDtypeStruct((n, N_HEADS * kl), BF16),
            jax.ShapeDtypeStruct((n, N_HEADS * 2 * QK_ROPE), BF16),
        ],
        compiler_params=_params(("parallel",), 48),
        name="q_sample",
    )(q_lat, wq_cat, w_uk, qt_tab)


def _attn_sample_kernel(qa_ref, qr_ref, cc_ref, ck_ref, nc_ref, nk_ref, wv_ref, o_ref, *, ts, past, kl):
    rows = N_HEADS * ts
    qs = jnp.concatenate([qa_ref[:, h * kl:(h + 1) * kl] for h in range(N_HEADS)], axis=0)
    qr = jnp.concatenate(
        [qr_ref[:, h * 2 * QK_ROPE:h * 2 * QK_ROPE + QK_ROPE] for h in range(N_HEADS)], axis=0)
    kc = cc_ref[0].astype(BF16)
    krc = ck_ref[0].astype(BF16)
    kn = nc_ref[...].astype(BF16)
    krn = nk_ref[:, :QK_ROPE].astype(BF16)
    s_c = _dot_t(qs, kc) + _dot_t(qr, krc)
    s_n = _dot_t(qs, kn) + _dot_t(qr, krn)
    q_chunk = (past + lax.broadcasted_iota(jnp.int32, (rows, 1), 0) % ts) >> CHUNK_SHIFT
    s_c = jnp.where((lax.broadcasted_iota(jnp.int32, (1, past), 1) >> CHUNK_SHIFT) <= q_chunk, s_c, NEG_INF)
    s_n = jnp.where(((past + lax.broadcasted_iota(jnp.int32, (1, ts), 1)) >> CHUNK_SHIFT) <= q_chunk, s_n, NEG_INF)
    m = jnp.maximum(jnp.max(s_c, axis=-1, keepdims=True), jnp.max(s_n, axis=-1, keepdims=True))
    p_c = jnp.exp2(s_c - m)
    p_n = jnp.exp2(s_n - m)
    l = jnp.sum(p_c, axis=-1, keepdims=True) + jnp.sum(p_n, axis=-1, keepdims=True)
    o_lat = ((_dot(p_c.astype(BF16), kc) + _dot(p_n.astype(BF16), kn)) * (1.0 / l)).astype(BF16)
    for h in range(N_HEADS):
        o_ref[:, h * V_HEAD:(h + 1) * V_HEAD] = _dot(
            o_lat[h * ts:(h + 1) * ts], wv_ref[:, h * V_HEAD:(h + 1) * V_HEAD]).astype(BF16)


def _attn_sample(q_abs, q_rope, cache_ckv, cache_kr, ckv_new, kr_new, w_uv, *, ts):
    b, past, kl = cache_ckv.shape
    row = lambda i: (i, 0)
    return pl.pallas_call(
        functools.partial(_attn_sample_kernel, ts=ts, past=past, kl=kl),
        grid=(b,),
        in_specs=[
            pl.BlockSpec((ts, q_abs.shape[1]), row),
            pl.BlockSpec((ts, q_rope.shape[1]), row),
            pl.BlockSpec((1, past, kl), lambda i: (i, 0, 0)),
            pl.BlockSpec((1, past, cache_kr.shape[2]), lambda i: (i, 0, 0)),
            pl.BlockSpec((ts, kl), row),
            pl.BlockSpec((ts, kr_new.shape[1]), row),
            pl.BlockSpec(w_uv.shape, lambda i: (0, 0)),
        ],
        out_specs=pl.BlockSpec((ts, N_HEADS * V_HEAD), row),
        out_shape=jax.ShapeDtypeStruct((b * ts, N_HEADS * V_HEAD), BF16),
        compiler_params=_params(("parallel",), 48),
        name="attn_sample",
    )(q_abs, q_rope, cache_ckv, cache_kr, ckv_new, kr_new, w_uv)


def _mix_out_kernel(u_ref, pa_ref, ob_ref, h_ref, wga_ref, wgb_ref, wpo_ref, woa_ref, wout_ref, *rest, n_cast):
    cast_src, acc_ref, cast_dst = rest[:n_cast], rest[n_cast], rest[n_cast + 1:]
    for src, dst in zip(cast_src, cast_dst):
        dst[...] = src[...].astype(BF16)
    j = pl.program_id(1)
    u = u_ref[...]
    gate_a = jax.nn.sigmoid(_dot(u, wga_ref[...]))
    gate_b = jax.nn.sigmoid(_dot(u, wgb_ref[...]))
    a = _dot(pa_ref[...], wpo_ref[...])
    b = _dot(ob_ref[...], woa_ref[...])
    merged = (gate_a * a + gate_b * b).astype(BF16)

    @pl.when(j == 0)
    def _():
        acc_ref[...] = h_ref[...]

    acc_ref[...] += _dot(merged, wout_ref[...])


def _mix_out(u, pooled, o_attn, h, w_gate, w_pool_out, w_o_attn, w_out, *, cast=(), tm=512, tc=512):
    n, d = h.shape
    nc = d // tc
    ni = n // tm
    row = lambda i, j: (i, 0)
    col = lambda i, j: (0, j)
    cast_blocks = []
    for w in cast:
        rows, cols = w.shape
        assert rows % (ni * BF16_ROWS) == 0 and cols % (nc * LANES) == 0, w.shape
        cast_blocks.append(pl.BlockSpec((rows // ni, cols // nc), lambda i, j: (i, j)))
    return pl.pallas_call(
        functools.partial(_mix_out_kernel, n_cast=len(cast)),
        grid=(ni, nc),
        in_specs=[
            pl.BlockSpec((tm, d), row),
            pl.BlockSpec((tm, pooled.shape[1]), row),
            pl.BlockSpec((tm, o_attn.shape[1]), row),
            pl.BlockSpec((tm, d), row),
            pl.BlockSpec((d, tc), col),
            pl.BlockSpec((d, tc), lambda i, j: (0, j + nc)),
            pl.BlockSpec((w_pool_out.shape[0], tc), col),
            pl.BlockSpec((w_o_attn.shape[0], tc), col),
            pl.BlockSpec((tc, d), lambda i, j: (j, 0)),
            *cast_blocks,
        ],
        out_specs=[pl.BlockSpec((tm, d), row), *cast_blocks],
        out_shape=[jax.ShapeDtypeStruct((n, d), F32), *[jax.ShapeDtypeStruct(w.shape, BF16) for w in cast]],
        compiler_params=_params(("parallel", "arbitrary"), 58),
        name="mix_out",
    )(u, pooled, o_attn, h, w_gate, w_gate, w_pool_out, w_o_attn, w_out, *cast)


def _rope_tables(pos):
    half = QK_ROPE // 2
    inv = ROPE_THETA ** (-jnp.arange(half, dtype=F32) * 2.0 / QK_ROPE)
    ang = pos.astype(F32)[:, None] * inv[None, :]
    c, s = jnp.cos(ang), jnp.sin(ang)
    cs = jnp.concatenate([c, c, -s, s], axis=-1)
    qt = (SM_SCALE * LOG2_E) * jnp.concatenate([jnp.ones((pos.shape[0], QK_NOPE), F32), cs], axis=-1)
    return cs, qt


def _dup_rope_cols(w):
    half = QK_ROPE // 2
    x1, x2 = w[..., :half], w[..., half:]
    return jnp.concatenate([x1, x2, x2, x1], axis=-1)


def kernel(x_prompt, x_sample, cache_ckv, cache_krope, state_pool, g_ffn1, w1_gate, w1_up, w1_down, g_mix, w_in, g_q_lat, g_kv_lat, w_uq, w_uk, w_uv, w_o_attn, w_pool, pool_scale, w_pool_out, w_out, g_ffn2, w2_gate, w2_up, w2_down, g_final):
    bp, tp, d = x_prompt.shape
    bs, ts, _ = x_sample.shape
    assert g_ffn1.shape[0] == 1, "single-layer stack only"
    assert ts >= POOL_HIST and tp >= POOL_HIST
    past = cache_ckv.shape[2]
    pw = pool_scale.shape[1]
    ql = g_q_lat.shape[1]
    kl = g_kv_lat.shape[1]
    o3 = pw + ql + kl
    o4 = o3 + QK_ROPE

    cs_p, qt_p = _rope_tables(jnp.arange(tp, dtype=jnp.int32))
    cs_s, qt_s = _rope_tables(past + jnp.arange(ts, dtype=jnp.int32))
    cs_s = jnp.tile(cs_s, (bs, 1))
    qt_s = jnp.tile(qt_s, (bs, 1))

    bf = lambda w: w[0].astype(BF16)
    vec = lambda g: g[0].reshape(1, -1)
    w_main = w_in[0][:, :o3].astype(BF16)
    w_kr = _dup_rope_cols(w_in[0][:, o3:o4]).astype(BF16)
    w_gate = w_in[0][:, o4:].astype(BF16)
    wq_cat = jnp.concatenate(
        [w_uq[0][..., :QK_NOPE], _dup_rope_cols(w_uq[0][..., QK_NOPE:])], axis=-1
    ).reshape(ql, N_HEADS * HEAD_W).astype(BF16)
    wk = w_uk[0].reshape(kl, N_HEADS * QK_NOPE).astype(BF16)
    wv = w_uv[0].reshape(kl, N_HEADS * V_HEAD).astype(BF16)
    wpool, wpo, woa, wout = bf(w_pool), bf(w_pool_out), bf(w_o_attn), bf(w_out)
    hist_s = jnp.pad(state_pool[0], ((0, 0), (POOL_PAD - POOL_HIST, 0), (0, 0)))
    g_last = g_final.reshape(1, -1)

    def in_proj(u, cs_tab):
        return _inproj(u, w_main, w_kr, vec(g_q_lat), vec(g_kv_lat), cs_tab, pw=pw, ql=ql, kl=kl)


    h1_s, u_s, w1g, w1u, w1d = _ffn(
        x_sample.reshape(bs * ts, d), vec(g_ffn1), w1_gate[0], w1_up[0], w1_down[0], vec(g_mix),
        emit_normed=True, cast_weights=True)
    z, q_lat, ckv_s, kr_s = in_proj(u_s, cs_s)
    z_s = z.reshape(bs, ts, pw)
    pooled_s = _pool_sample(z_s, hist_s, wpool, vec(pool_scale), pos0=past)
    q_abs, q_rope = _q_sample(q_lat, wq_cat, wk, qt_s)
    o_s = _attn_sample(q_abs, q_rope, cache_ckv[0], cache_krope[0], ckv_s, kr_s, wv, ts=ts)

    h1, u = _ffn(x_prompt.reshape(bp * tp, d), vec(g_ffn1), w1g, w1u, w1d, vec(g_mix), emit_normed=True)
    z, q_lat, ckv_p, kr_p = in_proj(u, cs_p)
    z_p = z.reshape(bp, tp, pw)
    pooled = _pool_prompt(z_p, wpool, vec(pool_scale)).reshape(bp * tp, pw)
    q_cat, k_cat, v = _qkv_prompt(q_lat, ckv_p, kr_p, wq_cat, wk, wv, qt_p)
    o = _attn_prompt(q_cat.reshape(bp, tp, -1), k_cat.reshape(bp, tp, -1), v.reshape(bp, tp, -1))
    h2, w2g, w2u, w2d = _mix_out(
        u, pooled, o.reshape(bp * tp, -1), h1, w_gate, wpo, woa, wout, cast=(w2_gate[0], w2_up[0], w2_down[0]))
    (y_p,) = _ffn(h2, vec(g_ffn2), w2g, w2u, w2d, g_last, emit_normed=False)

    (h2,) = _mix_out(u_s, pooled_s, o_s, h1_s, w_gate, wpo, woa, wout)
    (y_s,) = _ffn(h2, vec(g_ffn2), w2g, w2u, w2d, g_last, emit_normed=False)

    return (
        y_p.reshape(bp, tp, d),
        y_s.reshape(bs, ts, d),
        ckv_p.reshape(1, bp, tp, kl),
        kr_p[:, :QK_ROPE].reshape(1, bp, tp, QK_ROPE),
        z_p[None, :, tp - POOL_HIST:],
        ckv_s.reshape(1, bs, ts, kl),
        kr_s[:, :QK_ROPE].reshape(1, bs, ts, QK_ROPE),
        z_s[None, :, ts - POOL_HIST:],
    )
```

```python
import functools

import jax
import jax.numpy as jnp
from jax import lax
from jax.experimental import pallas as pl
from jax.experimental.pallas import tpu as pltpu

F32 = jnp.float32
BF16 = jnp.bfloat16

CHUNK_SHIFT = 6
N_HEADS = 16
QK_NOPE = 128
QK_ROPE = 64
V_HEAD = 128
HEAD_W = 256
LANES = 128
BF16_ROWS = 16
STRIP = 32
POOL_WINDOWS = (2, 4, 8, 16)
POOL_HIST = 15
POOL_PAD = 16
ROPE_THETA = 10000.0
EPS = 1e-6
SM_SCALE = (QK_NOPE + QK_ROPE) ** -0.5
LOG2_E = 1.4426950408889634
NEG_INF = -1e30
MIB = 1024 * 1024


def _dot(a, b):
    return jnp.dot(a, b, preferred_element_type=F32)


def _dot_t(a, b):
    return lax.dot_general(a, b, (((1,), (1,)), ((), ())), preferred_element_type=F32)


def _rms(x, g):
    return x * lax.rsqrt(jnp.mean(x * x, axis=-1, keepdims=True) + EPS) * g


def _params(semantics, vmem_mib):
    return pltpu.CompilerParams(dimension_semantics=semantics, vmem_limit_bytes=vmem_mib * MIB)


def _ffn_kernel(x_ref, g_ref, wg_ref, wu_ref, wd_ref, gn_ref, acc_ref, *rest, emit_normed, cast_weights):
    rest = list(rest)
    u_ref = rest.pop(0) if emit_normed else None
    xn_ref = rest.pop()
    j = pl.program_id(1)

    @pl.when(j == 0)
    def _():
        xn_ref[...] = _rms(x_ref[...], g_ref[...]).astype(BF16)
        acc_ref[...] = jnp.zeros_like(acc_ref)

    wg, wu, wd = wg_ref[...], wu_ref[...], wd_ref[...]
    if cast_weights:
        wg, wu, wd = wg.astype(BF16), wu.astype(BF16), wd.astype(BF16)
        for dst, w in zip(rest, (wg, wu, wd)):
            dst[...] = w
    xn = xn_ref[...]
    gate = _dot(xn, wg)
    up = _dot(xn, wu)
    act = (gate * jax.nn.sigmoid(gate) * up).astype(BF16)
    acc_ref[...] += _dot(act, wd)

    @pl.when(j == pl.num_programs(1) - 1)
    def _():
        h = x_ref[...] + 0.5 * acc_ref[...]
        if emit_normed:
            acc_ref[...] = h
            u_ref[...] = _rms(h, gn_ref[...]).astype(BF16)
        else:
            acc_ref[...] = _rms(h, gn_ref[...])


def _ffn(x, g, wg, wu, wd, g_next, *, emit_normed, cast_weights=False, tm=1024, tf=256):
    n, d = x.shape
    f = wg.shape[1]
    row = lambda i, j: (i, 0)
    assert not cast_weights or n == tm
    rows_mode = dict(pipeline_mode=pl.Buffered(1)) if n == tm else {}
    w_up_spec = pl.BlockSpec((d, tf), lambda i, j: (0, j))
    w_down_spec = pl.BlockSpec((tf, d), lambda i, j: (j, 0))
    out_shape = [jax.ShapeDtypeStruct((n, d), F32)]
    out_specs = [pl.BlockSpec((tm, d), row, **rows_mode)]
    if emit_normed:
        out_shape.append(jax.ShapeDtypeStruct((n, d), BF16))
        out_specs.append(pl.BlockSpec((tm, d), row, **rows_mode))
    if cast_weights:
        out_shape += [jax.ShapeDtypeStruct(w.shape, BF16) for w in (wg, wu, wd)]
        out_specs += [w_up_spec, w_up_spec, w_down_spec]
    return pl.pallas_call(
        functools.partial(_ffn_kernel, emit_normed=emit_normed, cast_weights=cast_weights),
        grid=(n // tm, f // tf),
        in_specs=[
            pl.BlockSpec((tm, d), row, **rows_mode),
            pl.BlockSpec((1, d), lambda i, j: (0, 0)),
            w_up_spec,
            w_up_spec,
            w_down_spec,
            pl.BlockSpec((1, d), lambda i, j: (0, 0)),
        ],
        out_specs=out_specs,
        out_shape=out_shape,
        scratch_shapes=[pltpu.VMEM((tm, d), BF16)],
        compiler_params=_params(("parallel", "arbitrary"), 62),
        name="ffn_norm" if emit_normed else "ffn_final",
    )(x, g, wg, wu, wd, g_next)


def _inproj_kernel(u_ref, w_ref, wkr_ref, gq_ref, gkv_ref, cs_ref, z_ref, q_ref, ckv_ref, kr_ref, *, pw, ql, kl):
    u = u_ref[...]
    proj = _dot_t(u, w_ref[...])
    z_ref[...] = proj[:, :pw]
    q_ref[...] = _rms(proj[:, pw:pw + ql], gq_ref[...]).astype(BF16)
    ckv_ref[...] = _rms(proj[:, pw + ql:pw + ql + kl], gkv_ref[...])
    y = _dot_t(u, wkr_ref[...]) * cs_ref[...]
    kr_ref[...] = y + pltpu.roll(y, QK_ROPE, axis=1)


def _inproj(u, w_main, w_kr, g_q, g_kv, cs_tab, *, pw, ql, kl, tm=1024):
    n, d = u.shape
    period = cs_tab.shape[0] // tm
    row = lambda i: (i, 0)
    fixed = lambda i: (0, 0)
    return pl.pallas_call(
        functools.partial(_inproj_kernel, pw=pw, ql=ql, kl=kl),
        grid=(n // tm,),
        in_specs=[
            pl.BlockSpec((tm, d), row),
            pl.BlockSpec((pw + ql + kl, d), fixed),
            pl.BlockSpec(w_kr.shape, fixed),
            pl.BlockSpec((1, ql), fixed),
            pl.BlockSpec((1, kl), fixed),
            pl.BlockSpec((tm, 2 * QK_ROPE), lambda i: (i % period, 0)),
        ],
        out_specs=[
            pl.BlockSpec((tm, pw), row),
            pl.BlockSpec((tm, ql), row),
            pl.BlockSpec((tm, kl), row),
            pl.BlockSpec((tm, 2 * QK_ROPE), row),
        ],
        out_shape=[
            jax.ShapeDtypeStruct((n, pw), F32),
            jax.ShapeDtypeStruct((n, ql), BF16),
            jax.ShapeDtypeStruct((n, kl), F32),
            jax.ShapeDtypeStruct((n, 2 * QK_ROPE), F32),
        ],
        compiler_params=_params(("parallel",), 48),
        name="in_proj",
    )(u, w_main, w_kr, g_q, g_kv, cs_tab)


def _pool_rows(zext, pos0, w_ref, scale_ref, out_ref, out_row0, seg_rows, n_seg):
    p = zext.shape[1]
    gc = p // len(POOL_WINDOWS)
    ext = POOL_PAD + seg_rows
    pos = pos0 + lax.broadcasted_iota(jnp.int32, (seg_rows, 1), 0)
    for g, w in enumerate(POOL_WINDOWS):
        cols = slice(g * gc, (g + 1) * gc)
        s = zext[:, cols]
        win = s
        step = 1
        while step < w:
            win = win + pltpu.roll(win, step, axis=0)
            step *= 2
        cnt = jnp.minimum(pos + 1, w).astype(F32)
        for b in range(n_seg):
            lo = b * ext + POOL_PAD
            pooled = win[lo:lo + seg_rows] / cnt - s[lo:lo + seg_rows]
            mixed = _dot(pooled.astype(BF16), w_ref[g]) * scale_ref[:, cols]
            out_ref[pl.ds(out_row0 + b * seg_rows, seg_rows), cols] = mixed.astype(BF16)


def _pool_prompt_kernel(cur_ref, prev_ref, w_ref, scale_ref, out_ref, ext_ref, *, tp):
    t = pl.program_id(1)
    hist = jnp.where(t == 0, 0.0, prev_ref[0])
    ext_ref[0:POOL_PAD, :] = hist
    ext_ref[POOL_PAD:, :] = cur_ref[0]
    _pool_rows(ext_ref[...], t * tp, w_ref, scale_ref, out_ref.at[0], 0, tp, 1)


def _pool_prompt(z, w_pool, pool_scale, *, tp=512):
    b, t, p = z.shape
    hist_blocks = tp // POOL_PAD
    return pl.pallas_call(
        functools.partial(_pool_prompt_kernel, tp=tp),
        grid=(b, t // tp),
        in_specs=[
            pl.BlockSpec((1, tp, p), lambda i, j: (i, j, 0)),
            pl.BlockSpec((1, POOL_PAD, p), lambda i, j: (i, jnp.maximum(j * hist_blocks - 1, 0), 0)),
            pl.BlockSpec(w_pool.shape, lambda i, j: (0, 0, 0)),
            pl.BlockSpec((1, p), lambda i, j: (0, 0)),
        ],
        out_specs=pl.BlockSpec((1, tp, p), lambda i, j: (i, j, 0)),
        out_shape=jax.ShapeDtypeStruct((b, t, p), BF16),
        scratch_shapes=[pltpu.VMEM((POOL_PAD + tp, p), F32)],
        compiler_params=_params(("parallel", "arbitrary"), 32),
        name="pool_prompt",
    )(z, z, w_pool, pool_scale)


def _pool_sample_kernel(cur_ref, hist_ref, w_ref, scale_ref, out_ref, ext_ref, *, nb, ts, pos0):
    ext = POOL_PAD + ts
    for b in range(nb):
        ext_ref[b * ext:b * ext + POOL_PAD, :] = hist_ref[b]
        ext_ref[b * ext + POOL_PAD:(b + 1) * ext, :] = cur_ref[b]
    _pool_rows(ext_ref[...], pos0, w_ref, scale_ref, out_ref, 0, ts, nb)


def _pool_sample(z, hist, w_pool, pool_scale, *, pos0, nb=8):
    b, ts, p = z.shape
    return pl.pallas_call(
        functools.partial(_pool_sample_kernel, nb=nb, ts=ts, pos0=pos0),
        grid=(b // nb,),
        in_specs=[
            pl.BlockSpec((nb, ts, p), lambda i: (i, 0, 0)),
            pl.BlockSpec((nb, POOL_PAD, p), lambda i: (i, 0, 0)),
            pl.BlockSpec(w_pool.shape, lambda i: (0, 0, 0)),
            pl.BlockSpec((1, p), lambda i: (0, 0)),
        ],
        out_specs=pl.BlockSpec((nb * ts, p), lambda i: (i, 0)),
        out_shape=jax.ShapeDtypeStruct((b * ts, p), BF16),
        scratch_shapes=[pltpu.VMEM((nb * (POOL_PAD + ts), p), F32)],
        compiler_params=_params(("parallel",), 32),
        name="pool_sample",
    )(z, hist, w_pool, pool_scale)


def _qkv_prompt_kernel(ql_ref, ckv_ref, kr_ref, wq_ref, wk_ref, wv_ref, qt_ref, q_ref, k_ref, v_ref):
    ql = ql_ref[...]
    ckv = ckv_ref[...].astype(BF16)
    kr = kr_ref[...].astype(BF16)
    qt = qt_ref[...]
    for h in range(N_HEADS):
        q_ref[:, h * HEAD_W:(h + 1) * HEAD_W] = (_dot(ql, wq_ref[:, h * HEAD_W:(h + 1) * HEAD_W]) * qt).astype(BF16)
    kn = _dot(ckv, wk_ref[...])
    for h in range(N_HEADS):
        k_ref[:, h * HEAD_W:h * HEAD_W + QK_NOPE] = kn[:, h * QK_NOPE:(h + 1) * QK_NOPE].astype(BF16)
        k_ref[:, h * HEAD_W + QK_NOPE:(h + 1) * HEAD_W] = kr
    v_ref[...] = _dot(ckv, wv_ref[...]).astype(BF16)


def _qkv_prompt(q_lat, ckv, krblk, wq_cat, w_uk, w_uv, qt_tab, *, tm=512):
    n = q_lat.shape[0]
    period = qt_tab.shape[0] // tm
    row = lambda i: (i, 0)
    fixed = lambda i: (0, 0)
    return pl.pallas_call(
        _qkv_prompt_kernel,
        grid=(n // tm,),
        in_specs=[
            pl.BlockSpec((tm, q_lat.shape[1]), row),
            pl.BlockSpec((tm, ckv.shape[1]), row),
            pl.BlockSpec((tm, krblk.shape[1]), row),
            pl.BlockSpec(wq_cat.shape, fixed),
            pl.BlockSpec(w_uk.shape, fixed),
            pl.BlockSpec(w_uv.shape, fixed),
            pl.BlockSpec((tm, HEAD_W), lambda i: (i % period, 0)),
        ],
        out_specs=[
            pl.BlockSpec((tm, N_HEADS * HEAD_W), row),
            pl.BlockSpec((tm, N_HEADS * HEAD_W), row),
            pl.BlockSpec((tm, N_HEADS * V_HEAD), row),
        ],
        out_shape=[
            jax.ShapeDtypeStruct((n, N_HEADS * HEAD_W), BF16),
            jax.ShapeDtypeStruct((n, N_HEADS * HEAD_W), BF16),
            jax.ShapeDtypeStruct((n, N_HEADS * V_HEAD), BF16),
        ],
        compiler_params=_params(("parallel",), 48),
        name="qkv_prompt",
    )(q_lat, ckv, krblk, wq_cat, w_uk, w_uv, qt_tab)


def _softmax_strips(s_ref, p_ref, m_ref, mc_ref, a_ref, masked):
    blk = s_ref.shape[1]

    def strip(r0):
        if not masked:
            return s_ref[r0:r0 + STRIP, :], blk
        visible = ((r0 >> CHUNK_SHIFT) + 1) << CHUNK_SHIFT
        width = -(-visible // LANES) * LANES
        s = s_ref[r0:r0 + STRIP, :width]
        return jnp.where(lax.broadcasted_iota(jnp.int32, s.shape, 1) < visible, s, NEG_INF), width

    for r0 in range(0, blk, STRIP):
        s, _ = strip(r0)
        mc_ref[r0:r0 + STRIP, :] = jnp.broadcast_to(jnp.max(s, axis=-1, keepdims=True), (STRIP, LANES))
    m_old = m_ref[...]
    m_new = jnp.maximum(m_old, mc_ref[...])
    m_ref[...] = m_new
    a_ref[...] = jnp.exp2(m_old - m_new)
    for r0 in range(0, blk, STRIP):
        s, width = strip(r0)
        p_ref[r0:r0 + STRIP, :width] = jnp.exp2(s - jnp.tile(m_ref[r0:r0 + STRIP, :], (1, width // LANES))).astype(BF16)
        if width < blk:
            p_ref[r0:r0 + STRIP, width:] = jnp.zeros((STRIP, blk - width), BF16)


def _attn_prompt_kernel(q_ref, k_ref, v_ref, *rest, blk, hb, hp, n_cast):
    cast_src, o_ref, cast_dst = rest[:n_cast], rest[n_cast], rest[n_cast + 1:2 * n_cast + 1]
    s_ref, p_ref, m_ref, mc_ref, a_ref, acc_ref = rest[2 * n_cast + 1:]
    for src, dst in zip(cast_src, cast_dst):
        dst[...] = src[...].astype(BF16)
    qi = pl.program_id(2)
    ones = jnp.ones((blk, V_HEAD), BF16)
    for h0 in range(0, hb, hp):
        heads = range(h0, h0 + hp)
        m_ref[...] = jnp.full(m_ref.shape, NEG_INF, F32)
        acc_ref[...] = jnp.zeros_like(acc_ref)

        def block(off, masked, heads=heads):
            for n, h in enumerate(heads):
                q = q_ref[0, :, h * HEAD_W:(h + 1) * HEAD_W]
                s_ref[n] = _dot_t(q, k_ref[0, pl.ds(off, blk), h * HEAD_W:(h + 1) * HEAD_W])
            for n in range(hp):
                _softmax_strips(s_ref.at[n], p_ref.at[n], m_ref.at[n], mc_ref.at[n], a_ref.at[n], masked)
            for n, h in enumerate(heads):
                v = v_ref[0, pl.ds(off, blk), h * V_HEAD:(h + 1) * V_HEAD]
                pv = _dot(p_ref[n], jnp.concatenate([v, ones], axis=1))
                acc_ref[n] = jnp.tile(a_ref[n], (1, 2)) * acc_ref[n] + pv

        def below_diagonal(j, carry):
            block(pl.multiple_of(j * blk, blk), masked=False)
            return carry

        lax.fori_loop(0, qi, below_diagonal, 0)
        block(pl.multiple_of(qi * blk, blk), masked=True)
        for n, h in enumerate(heads):
            acc = acc_ref[n]
            o_ref[0, :, h * V_HEAD:(h + 1) * V_HEAD] = (acc[:, :V_HEAD] * (1.0 / acc[:, V_HEAD:])).astype(BF16)


def _attn_prompt(q_cat, k_cat, v, *, cast=(), blk=512, hb=4, hp=4):
    b, t, _ = q_cat.shape
    assert STRIP <= (1 << CHUNK_SHIFT) and blk % (1 << CHUNK_SHIFT) == 0
    ng, nq = N_HEADS // hb, t // blk
    cast_specs = []
    for w in cast:
        rows, cols = w.shape
        assert rows % (b * ng * BF16_ROWS) == 0 and cols % (nq * LANES) == 0, w.shape
        cast_specs.append(pl.BlockSpec((rows // (b * ng), cols // nq), lambda i, g, q: (i * ng + g, q)))
    return pl.pallas_call(
        functools.partial(_attn_prompt_kernel, blk=blk, hb=hb, hp=hp, n_cast=len(cast)),
        grid=(b, ng, nq),
        in_specs=[
            pl.BlockSpec((1, blk, hb * HEAD_W), lambda i, g, q: (i, q, g)),
            pl.BlockSpec((1, t, hb * HEAD_W), lambda i, g, q: (i, 0, g)),
            pl.BlockSpec((1, t, hb * V_HEAD), lambda i, g, q: (i, 0, g)),
            *cast_specs,
        ],
        out_specs=[pl.BlockSpec((1, blk, hb * V_HEAD), lambda i, g, q: (i, q, g)), *cast_specs],
        out_shape=[
            jax.ShapeDtypeStruct((b, t, N_HEADS * V_HEAD), BF16),
            *[jax.ShapeDtypeStruct(w.shape, BF16) for w in cast],
        ],
        scratch_shapes=[
            pltpu.VMEM((hp, blk, blk), F32),
            pltpu.VMEM((hp, blk, blk), BF16),
            pltpu.VMEM((hp, blk, LANES), F32),
            pltpu.VMEM((hp, blk, LANES), F32),
            pltpu.VMEM((hp, blk, LANES), F32),
            pltpu.VMEM((hp, blk, 2 * V_HEAD), F32),
        ],
        compiler_params=_params(("parallel", "parallel", "arbitrary"), 48),
        name="attn_prompt",
    )(q_cat, k_cat, v, *cast)


def _q_sample_kernel(ql_ref, wq_ref, wk_ref, qt_ref, qa_ref, qr_ref, *, kl):
    ql = ql_ref[...]
    qt = qt_ref[...]
    for h in range(N_HEADS):
        q = _dot(ql, wq_ref[:, h * HEAD_W:(h + 1) * HEAD_W]) * qt
        qn = q[:, :QK_NOPE].astype(BF16)
        qa_ref[:, h * kl:(h + 1) * kl] = _dot_t(qn, wk_ref[:, h * QK_NOPE:(h + 1) * QK_NOPE]).astype(BF16)
        y = q[:, QK_NOPE:]
        qr_ref[:, h * 2 * QK_ROPE:(h + 1) * 2 * QK_ROPE] = (y + pltpu.roll(y, QK_ROPE, axis=1)).astype(BF16)


def _q_sample(q_lat, wq_cat, w_uk, qt_tab, *, tm=512):
    n, ql = q_lat.shape
    kl = w_uk.shape[0]
    row = lambda i: (i, 0)
    fixed = lambda i: (0, 0)
    return pl.pallas_call(
        functools.partial(_q_sample_kernel, kl=kl),
        grid=(n // tm,),
        in_specs=[
            pl.BlockSpec((tm, ql), row),
            pl.BlockSpec(wq_cat.shape, fixed),
            pl.BlockSpec(w_uk.shape, fixed),
            pl.BlockSpec((tm, HEAD_W), row),
        ],
        out_specs=[
            pl.BlockSpec((tm, N_HEADS * kl), row),
            pl.BlockSpec((tm, N_HEADS * 2 * QK_ROPE), row),
        ],
        out_shape=[
            jax.ShapeDtypeStruct((n, N_HEADS * kl), BF16),
            jax.ShapeDtypeStruct((n, N_HEADS * 2 * QK_ROPE), BF16),
        ],
        compiler_params=_params(("parallel",), 48),
        name="q_sample",
    )(q_lat, wq_cat, w_uk, qt_tab)


def _attn_sample_kernel(qa_ref, qr_ref, cc_ref, ck_ref, nc_ref, nk_ref, wv_ref, o_ref, *, ts, past, kl):
    rows = N_HEADS * ts
    qs = jnp.concatenate([qa_ref[:, h * kl:(h + 1) * kl] for h in range(N_HEADS)], axis=0)
    qr = jnp.concatenate(
        [qr_ref[:, h * 2 * QK_ROPE:h * 2 * QK_ROPE + QK_ROPE] for h in range(N_HEADS)], axis=0)
    kc = cc_ref[0].astype(BF16)
    krc = ck_ref[0].astype(BF16)
    kn = nc_ref[...].astype(BF16)
    krn = nk_ref[:, :QK_ROPE].astype(BF16)
    s_c = _dot_t(qs, kc) + _dot_t(qr, krc)
    s_n = _dot_t(qs, kn) + _dot_t(qr, krn)
    q_chunk = (past + lax.broadcasted_iota(jnp.int32, (rows, 1), 0) % ts) >> CHUNK_SHIFT
    s_c = jnp.where((lax.broadcasted_iota(jnp.int32, (1, past), 1) >> CHUNK_SHIFT) <= q_chunk, s_c, NEG_INF)
    s_n = jnp.where(((past + lax.broadcasted_iota(jnp.int32, (1, ts), 1)) >> CHUNK_SHIFT) <= q_chunk, s_n, NEG_INF)
    m = jnp.maximum(jnp.max(s_c, axis=-1, keepdims=True), jnp.max(s_n, axis=-1, keepdims=True))
    p_c = jnp.exp2(s_c - m)
    p_n = jnp.exp2(s_n - m)
    l = jnp.sum(p_c, axis=-1, keepdims=True) + jnp.sum(p_n, axis=-1, keepdims=True)
    o_lat = ((_dot(p_c.astype(BF16), kc) + _dot(p_n.astype(BF16), kn)) * (1.0 / l)).astype(BF16)
    for h in range(N_HEADS):
        o_ref[:, h * V_HEAD:(h + 1) * V_HEAD] = _dot(
            o_lat[h * ts:(h + 1) * ts], wv_ref[:, h * V_HEAD:(h + 1) * V_HEAD]).astype(BF16)


def _attn_sample(q_abs, q_rope, cache_ckv, cache_kr, ckv_new, kr_new, w_uv, *, ts):
    b, past, kl = cache_ckv.shape
    row = lambda i: (i, 0)
    return pl.pallas_call(
        functools.partial(_attn_sample_kernel, ts=ts, past=past, kl=kl),
        grid=(b,),
        in_specs=[
            pl.BlockSpec((ts, q_abs.shape[1]), row),
            pl.BlockSpec((ts, q_rope.shape[1]), row),
            pl.BlockSpec((1, past, kl), lambda i: (i, 0, 0)),
            pl.BlockSpec((1, past, cache_kr.shape[2]), lambda i: (i, 0, 0)),
            pl.BlockSpec((ts, kl), row),
            pl.BlockSpec((ts, kr_new.shape[1]), row),
            pl.BlockSpec(w_uv.shape, lambda i: (0, 0)),
        ],
        out_specs=pl.BlockSpec((ts, N_HEADS * V_HEAD), row),
        out_shape=jax.ShapeDtypeStruct((b * ts, N_HEADS * V_HEAD), BF16),
        compiler_params=_params(("parallel",), 48),
        name="attn_sample",
    )(q_abs, q_rope, cache_ckv, cache_kr, ckv_new, kr_new, w_uv)


def _mix_out_kernel(u_ref, pa_ref, ob_ref, h_ref, wga_ref, wgb_ref, wpo_ref, woa_ref, wout_ref, acc_ref):
    j = pl.program_id(1)
    u = u_ref[...]
    gate_a = jax.nn.sigmoid(_dot_t(u, wga_ref[...]))
    gate_b = jax.nn.sigmoid(_dot_t(u, wgb_ref[...]))
    a = _dot(pa_ref[...], wpo_ref[...])
    b = _dot(ob_ref[...], woa_ref[...])
    merged = (gate_a * a + gate_b * b).astype(BF16)

    @pl.when(j == 0)
    def _():
        acc_ref[...] = h_ref[...]

    acc_ref[...] += _dot(merged, wout_ref[...])


def _mix_out(u, pooled, o_attn, h, w_gate_t, w_pool_out, w_o_attn, w_out, *, tm=512, tc=512):
    n, d = h.shape
    nc = d // tc
    row = lambda i, j: (i, 0)
    col = lambda i, j: (0, j)
    return pl.pallas_call(
        _mix_out_kernel,
        grid=(n // tm, nc),
        in_specs=[
            pl.BlockSpec((tm, d), row),
            pl.BlockSpec((tm, pooled.shape[1]), row),
            pl.BlockSpec((tm, o_attn.shape[1]), row),
            pl.BlockSpec((tm, d), row),
            pl.BlockSpec((tc, d), lambda i, j: (j, 0)),
            pl.BlockSpec((tc, d), lambda i, j: (j + nc, 0)),
            pl.BlockSpec((w_pool_out.shape[0], tc), col),
            pl.BlockSpec((w_o_attn.shape[0], tc), col),
            pl.BlockSpec((tc, d), lambda i, j: (j, 0)),
        ],
        out_specs=pl.BlockSpec((tm, d), row),
        out_shape=jax.ShapeDtypeStruct((n, d), F32),
        compiler_params=_params(("parallel", "arbitrary"), 56),
        name="mix_out",
    )(u, pooled, o_attn, h, w_gate_t, w_gate_t, w_pool_out, w_o_attn, w_out)


def _rope_tables(pos):
    half = QK_ROPE // 2
    inv = ROPE_THETA ** (-jnp.arange(half, dtype=F32) * 2.0 / QK_ROPE)
    ang = pos.astype(F32)[:, None] * inv[None, :]
    c, s = jnp.cos(ang), jnp.sin(ang)
    cs = jnp.concatenate([c, c, -s, s], axis=-1)
    qt = (SM_SCALE * LOG2_E) * jnp.concatenate([jnp.ones((pos.shape[0], QK_NOPE), F32), cs], axis=-1)
    return cs, qt


def _dup_rope_cols(w):
    half = QK_ROPE // 2
    x1, x2 = w[..., :half], w[..., half:]
    return jnp.concatenate([x1, x2, x2, x1], axis=-1)


def kernel(x_prompt, x_sample, cache_ckv, cache_krope, state_pool, g_ffn1, w1_gate, w1_up, w1_down, g_mix, w_in, g_q_lat, g_kv_lat, w_uq, w_uk, w_uv, w_o_attn, w_pool, pool_scale, w_pool_out, w_out, g_ffn2, w2_gate, w2_up, w2_down, g_final):
    bp, tp, d = x_prompt.shape
    bs, ts, _ = x_sample.shape
    assert g_ffn1.shape[0] == 1, "single-layer stack only"
    assert ts >= POOL_HIST and tp >= POOL_HIST
    past = cache_ckv.shape[2]
    pw = pool_scale.shape[1]
    ql = g_q_lat.shape[1]
    kl = g_kv_lat.shape[1]
    o3 = pw + ql + kl
    o4 = o3 + QK_ROPE

    cs_p, qt_p = _rope_tables(jnp.arange(tp, dtype=jnp.int32))
    cs_s, qt_s = _rope_tables(past + jnp.arange(ts, dtype=jnp.int32))
    cs_s = jnp.tile(cs_s, (bs, 1))
    qt_s = jnp.tile(qt_s, (bs, 1))

    bf = lambda w: w[0].astype(BF16)
    vec = lambda g: g[0].reshape(1, -1)
    w_in_t = w_in[0].T
    w_main = w_in_t[:o3].astype(BF16)
    w_kr = _dup_rope_cols(w_in_t[o3:o4].T).T.astype(BF16)
    w_gate = w_in_t[o4:].astype(BF16)
    wq_cat = jnp.concatenate(
        [w_uq[0][..., :QK_NOPE], _dup_rope_cols(w_uq[0][..., QK_NOPE:])], axis=-1
    ).reshape(ql, N_HEADS * HEAD_W).astype(BF16)
    wk = w_uk[0].reshape(kl, N_HEADS * QK_NOPE).astype(BF16)
    wv = w_uv[0].reshape(kl, N_HEADS * V_HEAD).astype(BF16)
    wpool = bf(w_pool)
    hist_s = jnp.pad(state_pool[0], ((0, 0), (POOL_PAD - POOL_HIST, 0), (0, 0)))
    g_last = g_final.reshape(1, -1)

    def in_proj(u, cs_tab):
        return _inproj(u, w_main, w_kr, vec(g_q_lat), vec(g_kv_lat), cs_tab, pw=pw, ql=ql, kl=kl)


    h1_s, u_s, w1g, w1u, w1d = _ffn(
        x_sample.reshape(bs * ts, d), vec(g_ffn1), w1_gate[0], w1_up[0], w1_down[0], vec(g_mix),
        emit_normed=True, cast_weights=True)
    z, q_lat, ckv_s, kr_s = in_proj(u_s, cs_s)
    z_s = z.reshape(bs, ts, pw)
    pooled_s = _pool_sample(z_s, hist_s, wpool, vec(pool_scale), pos0=past)
    q_abs_s, q_rope_s = _q_sample(q_lat, wq_cat, wk, qt_s)

    h1, u = _ffn(x_prompt.reshape(bp * tp, d), vec(g_ffn1), w1g, w1u, w1d, vec(g_mix), emit_normed=True)
    z, q_lat, ckv_p, kr_p = in_proj(u, cs_p)
    z_p = z.reshape(bp, tp, pw)
    pooled = _pool_prompt(z_p, wpool, vec(pool_scale)).reshape(bp * tp, pw)
    q_cat, k_cat, v = _qkv_prompt(q_lat, ckv_p, kr_p, wq_cat, wk, wv, qt_p)
    o, w2g, w2u, w2d, wpo, woa, wout = _attn_prompt(
        q_cat.reshape(bp, tp, -1), k_cat.reshape(bp, tp, -1), v.reshape(bp, tp, -1),
        cast=(w2_gate[0], w2_up[0], w2_down[0], w_pool_out[0], w_o_attn[0], w_out[0]))
    h2 = _mix_out(u, pooled, o.reshape(bp * tp, -1), h1, w_gate, wpo, woa, wout)
    (y_p,) = _ffn(h2, vec(g_ffn2), w2g, w2u, w2d, g_last, emit_normed=False)

    o_s = _attn_sample(q_abs_s, q_rope_s, cache_ckv[0], cache_krope[0], ckv_s, kr_s, wv, ts=ts)
    h2 = _mix_out(u_s, pooled_s, o_s, h1_s, w_gate, wpo, woa, wout)
    (y_s,) = _ffn(h2, vec(g_ffn2), w2g, w2u, w2d, g_last, emit_normed=False)

    return (
        y_p.reshape(bp, tp, d),
        y_s.reshape(bs, ts, d),
        ckv_p.reshape(1, bp, tp, kl),
        kr_p[:, :QK_ROPE].reshape(1, bp, tp, QK_ROPE),
        z_p[None, :, tp - POOL_HIST:],
        ckv_s.reshape(1, bs, ts, kl),
        kr_s[:, :QK_ROPE].reshape(1, bs, ts, QK_ROPE),
        z_s[None, :, ts - POOL_HIST:],
    )
```

```python
import functools

import jax
import jax.numpy as jnp
from jax import lax
from jax.experimental import pallas as pl
from jax.experimental.pallas import tpu as pltpu

F32 = jnp.float32
BF16 = jnp.bfloat16

CHUNK_SHIFT = 6
N_HEADS = 16
QK_NOPE = 128
QK_ROPE = 64
V_HEAD = 128
HEAD_W = 256
LANES = 128
BF16_ROWS = 16
STRIP = 32
POOL_WINDOWS = (2, 4, 8, 16)
POOL_HIST = 15
POOL_PAD = 16
ROPE_THETA = 10000.0
EPS = 1e-6
SM_SCALE = (QK_NOPE + QK_ROPE) ** -0.5
LOG2_E = 1.4426950408889634
NEG_INF = -1e30
MIB = 1024 * 1024


def _dot(a, b):
    return jnp.dot(a, b, preferred_element_type=F32)


def _dot_t(a, b):
    return lax.dot_general(a, b, (((1,), (1,)), ((), ())), preferred_element_type=F32)


def _rms(x, g):
    return x * lax.rsqrt(jnp.mean(x * x, axis=-1, keepdims=True) + EPS) * g


def _params(semantics, vmem_mib):
    return pltpu.CompilerParams(dimension_semantics=semantics, vmem_limit_bytes=vmem_mib * MIB)


def _ffn_kernel(x_ref, g_ref, wg_ref, wu_ref, wd_ref, gn_ref, acc_ref, *rest, emit_normed, cast_weights):
    rest = list(rest)
    u_ref = rest.pop(0) if emit_normed else None
    xn_ref = rest.pop()
    j = pl.program_id(1)

    @pl.when(j == 0)
    def _():
        xn_ref[...] = _rms(x_ref[...], g_ref[...]).astype(BF16)
        acc_ref[...] = jnp.zeros_like(acc_ref)

    wg, wu, wd = wg_ref[...], wu_ref[...], wd_ref[...]
    if cast_weights:
        wg, wu, wd = wg.astype(BF16), wu.astype(BF16), wd.astype(BF16)
        for dst, w in zip(rest, (wg, wu, wd)):
            dst[...] = w
    xn = xn_ref[...]
    gate = _dot(xn, wg)
    up = _dot(xn, wu)
    act = (gate * jax.nn.sigmoid(gate) * up).astype(BF16)
    acc_ref[...] += _dot(act, wd)

    @pl.when(j == pl.num_programs(1) - 1)
    def _():
        h = x_ref[...] + 0.5 * acc_ref[...]
        if emit_normed:
            acc_ref[...] = h
            u_ref[...] = _rms(h, gn_ref[...]).astype(BF16)
        else:
            acc_ref[...] = _rms(h, gn_ref[...])


def _ffn(x, g, wg, wu, wd, g_next, *, emit_normed, cast_weights=False, tm=1024, tf=256):
    n, d = x.shape
    f = wg.shape[1]
    row = lambda i, j: (i, 0)
    assert not cast_weights or n == tm
    rows_mode = dict(pipeline_mode=pl.Buffered(1)) if n == tm else {}
    w_up_spec = pl.BlockSpec((d, tf), lambda i, j: (0, j))
    w_down_spec = pl.BlockSpec((tf, d), lambda i, j: (j, 0))
    out_shape = [jax.ShapeDtypeStruct((n, d), F32)]
    out_specs = [pl.BlockSpec((tm, d), row, **rows_mode)]
    if emit_normed:
        out_shape.append(jax.ShapeDtypeStruct((n, d), BF16))
        out_specs.append(pl.BlockSpec((tm, d), row, **rows_mode))
    if cast_weights:
        out_shape += [jax.ShapeDtypeStruct(w.shape, BF16) for w in (wg, wu, wd)]
        out_specs += [w_up_spec, w_up_spec, w_down_spec]
    return pl.pallas_call(
        functools.partial(_ffn_kernel, emit_normed=emit_normed, cast_weights=cast_weights),
        grid=(n // tm, f // tf),
        in_specs=[
            pl.BlockSpec((tm, d), row, **rows_mode),
            pl.BlockSpec((1, d), lambda i, j: (0, 0)),
            w_up_spec,
            w_up_spec,
            w_down_spec,
            pl.BlockSpec((1, d), lambda i, j: (0, 0)),
        ],
        out_specs=out_specs,
        out_shape=out_shape,
        scratch_shapes=[pltpu.VMEM((tm, d), BF16)],
        compiler_params=_params(("parallel", "arbitrary"), 62),
        name="ffn_norm" if emit_normed else "ffn_final",
    )(x, g, wg, wu, wd, g_next)


def _inproj_kernel(u_ref, w_ref, wkr_ref, gq_ref, gkv_ref, cs_ref, z_ref, q_ref, ckv_ref, kr_ref, *, pw, ql, kl):
    u = u_ref[...]
    proj = _dot_t(u, w_ref[...])
    z_ref[...] = proj[:, :pw]
    q_ref[...] = _rms(proj[:, pw:pw + ql], gq_ref[...]).astype(BF16)
    ckv_ref[...] = _rms(proj[:, pw + ql:pw + ql + kl], gkv_ref[...])
    y = _dot_t(u, wkr_ref[...]) * cs_ref[...]
    kr_ref[...] = y + pltpu.roll(y, QK_ROPE, axis=1)


def _inproj(u, w_in_t, w_kr, g_q, g_kv, cs_tab, *, pw, ql, kl, tm=1024):
    n, d = u.shape
    period = cs_tab.shape[0] // tm
    row = lambda i: (i, 0)
    fixed = lambda i: (0, 0)
    return pl.pallas_call(
        functools.partial(_inproj_kernel, pw=pw, ql=ql, kl=kl),
        grid=(n // tm,),
        in_specs=[
            pl.BlockSpec((tm, d), row),
            pl.BlockSpec((pw + ql + kl, d), fixed),
            pl.BlockSpec(w_kr.shape, fixed),
            pl.BlockSpec((1, ql), fixed),
            pl.BlockSpec((1, kl), fixed),
            pl.BlockSpec((tm, 2 * QK_ROPE), lambda i: (i % period, 0)),
        ],
        out_specs=[
            pl.BlockSpec((tm, pw), row),
            pl.BlockSpec((tm, ql), row),
            pl.BlockSpec((tm, kl), row),
            pl.BlockSpec((tm, 2 * QK_ROPE), row),
        ],
        out_shape=[
            jax.ShapeDtypeStruct((n, pw), F32),
            jax.ShapeDtypeStruct((n, ql), BF16),
            jax.ShapeDtypeStruct((n, kl), F32),
            jax.ShapeDtypeStruct((n, 2 * QK_ROPE), F32),
        ],
        compiler_params=_params(("parallel",), 48),
        name="in_proj",
    )(u, w_in_t, w_kr, g_q, g_kv, cs_tab)


def _pool_rows(zext, pos0, w_ref, scale_ref, out_ref, out_row0, seg_rows, n_seg):
    p = zext.shape[1]
    gc = p // len(POOL_WINDOWS)
    ext = POOL_PAD + seg_rows
    pos = pos0 + lax.broadcasted_iota(jnp.int32, (seg_rows, 1), 0)
    for g, w in enumerate(POOL_WINDOWS):
        cols = slice(g * gc, (g + 1) * gc)
        s = zext[:, cols]
        win = s
        step = 1
        while step < w:
            win = win + pltpu.roll(win, step, axis=0)
            step *= 2
        cnt = jnp.minimum(pos + 1, w).astype(F32)
        for b in range(n_seg):
            lo = b * ext + POOL_PAD
            pooled = win[lo:lo + seg_rows] / cnt - s[lo:lo + seg_rows]
            mixed = _dot(pooled.astype(BF16), w_ref[g]) * scale_ref[:, cols]
            out_ref[pl.ds(out_row0 + b * seg_rows, seg_rows), cols] = mixed.astype(BF16)


def _pool_prompt_kernel(cur_ref, prev_ref, w_ref, scale_ref, out_ref, ext_ref, *, tp):
    t = pl.program_id(1)
    hist = jnp.where(t == 0, 0.0, prev_ref[0])
    ext_ref[0:POOL_PAD, :] = hist
    ext_ref[POOL_PAD:, :] = cur_ref[0]
    _pool_rows(ext_ref[...], t * tp, w_ref, scale_ref, out_ref.at[0], 0, tp, 1)


def _pool_prompt(z, w_pool, pool_scale, *, tp=512):
    b, t, p = z.shape
    hist_blocks = tp // POOL_PAD
    return pl.pallas_call(
        functools.partial(_pool_prompt_kernel, tp=tp),
        grid=(b, t // tp),
        in_specs=[
            pl.BlockSpec((1, tp, p), lambda i, j: (i, j, 0)),
            pl.BlockSpec((1, POOL_PAD, p), lambda i, j: (i, jnp.maximum(j * hist_blocks - 1, 0), 0)),
            pl.BlockSpec(w_pool.shape, lambda i, j: (0, 0, 0)),
            pl.BlockSpec((1, p), lambda i, j: (0, 0)),
        ],
        out_specs=pl.BlockSpec((1, tp, p), lambda i, j: (i, j, 0)),
        out_shape=jax.ShapeDtypeStruct((b, t, p), BF16),
        scratch_shapes=[pltpu.VMEM((POOL_PAD + tp, p), F32)],
        compiler_params=_params(("parallel", "arbitrary"), 32),
        name="pool_prompt",
    )(z, z, w_pool, pool_scale)


def _pool_sample_kernel(cur_ref, hist_ref, w_ref, scale_ref, out_ref, ext_ref, *, nb, ts, pos0):
    ext = POOL_PAD + ts
    for b in range(nb):
        ext_ref[b * ext:b * ext + POOL_PAD, :] = hist_ref[b]
        ext_ref[b * ext + POOL_PAD:(b + 1) * ext, :] = cur_ref[b]
    _pool_rows(ext_ref[...], pos0, w_ref, scale_ref, out_ref, 0, ts, nb)


def _pool_sample(z, hist, w_pool, pool_scale, *, pos0, nb=8):
    b, ts, p = z.shape
    return pl.pallas_call(
        functools.partial(_pool_sample_kernel, nb=nb, ts=ts, pos0=pos0),
        grid=(b // nb,),
        in_specs=[
            pl.BlockSpec((nb, ts, p), lambda i: (i, 0, 0)),
            pl.BlockSpec((nb, POOL_PAD, p), lambda i: (i, 0, 0)),
            pl.BlockSpec(w_pool.shape, lambda i: (0, 0, 0)),
            pl.BlockSpec((1, p), lambda i: (0, 0)),
        ],
        out_specs=pl.BlockSpec((nb * ts, p), lambda i: (i, 0)),
        out_shape=jax.ShapeDtypeStruct((b * ts, p), BF16),
        scratch_shapes=[pltpu.VMEM((nb * (POOL_PAD + ts), p), F32)],
        compiler_params=_params(("parallel",), 32),
        name="pool_sample",
    )(z, hist, w_pool, pool_scale)


def _qkv_prompt_kernel(ql_ref, ckv_ref, kr_ref, wq_ref, wk_ref, wv_ref, qt_ref, q_ref, k_ref, v_ref):
    ql = ql_ref[...]
    ckv = ckv_ref[...].astype(BF16)
    kr = kr_ref[...].astype(BF16)
    qt = qt_ref[...]
    for h in range(N_HEADS):
        q_ref[:, h * HEAD_W:(h + 1) * HEAD_W] = (_dot(ql, wq_ref[:, h * HEAD_W:(h + 1) * HEAD_W]) * qt).astype(BF16)
    kn = _dot(ckv, wk_ref[...])
    for h in range(N_HEADS):
        k_ref[:, h * HEAD_W:h * HEAD_W + QK_NOPE] = kn[:, h * QK_NOPE:(h + 1) * QK_NOPE].astype(BF16)
        k_ref[:, h * HEAD_W + QK_NOPE:(h + 1) * HEAD_W] = kr
    v_ref[...] = _dot(ckv, wv_ref[...]).astype(BF16)


def _qkv_prompt(q_lat, ckv, krblk, wq_cat, w_uk, w_uv, qt_tab, *, tm=512):
    n = q_lat.shape[0]
    period = qt_tab.shape[0] // tm
    row = lambda i: (i, 0)
    fixed = lambda i: (0, 0)
    return pl.pallas_call(
        _qkv_prompt_kernel,
        grid=(n // tm,),
        in_specs=[
            pl.BlockSpec((tm, q_lat.shape[1]), row),
            pl.BlockSpec((tm, ckv.shape[1]), row),
            pl.BlockSpec((tm, krblk.shape[1]), row),
            pl.BlockSpec(wq_cat.shape, fixed),
            pl.BlockSpec(w_uk.shape, fixed),
            pl.BlockSpec(w_uv.shape, fixed),
            pl.BlockSpec((tm, HEAD_W), lambda i: (i % period, 0)),
        ],
        out_specs=[
            pl.BlockSpec((tm, N_HEADS * HEAD_W), row),
            pl.BlockSpec((tm, N_HEADS * HEAD_W), row),
            pl.BlockSpec((tm, N_HEADS * V_HEAD), row),
        ],
        out_shape=[
            jax.ShapeDtypeStruct((n, N_HEADS * HEAD_W), BF16),
            jax.ShapeDtypeStruct((n, N_HEADS * HEAD_W), BF16),
            jax.ShapeDtypeStruct((n, N_HEADS * V_HEAD), BF16),
        ],
        compiler_params=_params(("parallel",), 48),
        name="qkv_prompt",
    )(q_lat, ckv, krblk, wq_cat, w_uk, w_uv, qt_tab)


def _softmax_strips(s_ref, p_ref, m_ref, mc_ref, a_ref, masked):
    blk = s_ref.shape[1]

    def strip(r0):
        if not masked:
            return s_ref[r0:r0 + STRIP, :], blk
        visible = ((r0 >> CHUNK_SHIFT) + 1) << CHUNK_SHIFT
        width = -(-visible // LANES) * LANES
        s = s_ref[r0:r0 + STRIP, :width]
        return jnp.where(lax.broadcasted_iota(jnp.int32, s.shape, 1) < visible, s, NEG_INF), width

    for r0 in range(0, blk, STRIP):
        s, _ = strip(r0)
        mc_ref[r0:r0 + STRIP, :] = jnp.broadcast_to(jnp.max(s, axis=-1, keepdims=True), (STRIP, LANES))
    m_old = m_ref[...]
    m_new = jnp.maximum(m_old, mc_ref[...])
    m_ref[...] = m_new
    a_ref[...] = jnp.exp2(m_old - m_new)
    for r0 in range(0, blk, STRIP):
        s, width = strip(r0)
        p_ref[r0:r0 + STRIP, :width] = jnp.exp2(s - jnp.tile(m_ref[r0:r0 + STRIP, :], (1, width // LANES))).astype(BF16)
        if width < blk:
            p_ref[r0:r0 + STRIP, width:] = jnp.zeros((STRIP, blk - width), BF16)


def _attn_prompt_kernel(q_ref, k_ref, v_ref, *rest, blk, hb, hp, n_cast):
    cast_src, o_ref, cast_dst = rest[:n_cast], rest[n_cast], rest[n_cast + 1:2 * n_cast + 1]
    s_ref, p_ref, m_ref, mc_ref, a_ref, acc_ref = rest[2 * n_cast + 1:]
    qi = pl.program_id(2)
    ones = jnp.ones((blk, V_HEAD), BF16)
    for h0 in range(0, hb, hp):
        heads = range(h0, h0 + hp)
        m_ref[...] = jnp.full(m_ref.shape, NEG_INF, F32)
        acc_ref[...] = jnp.zeros_like(acc_ref)

        def block(off, masked, heads=heads):
            for n, h in enumerate(heads):
                q = q_ref[0, :, h * HEAD_W:(h + 1) * HEAD_W]
                s_ref[n] = _dot_t(q, k_ref[0, pl.ds(off, blk), h * HEAD_W:(h + 1) * HEAD_W])
            for n in range(hp):
                _softmax_strips(s_ref.at[n], p_ref.at[n], m_ref.at[n], mc_ref.at[n], a_ref.at[n], masked)
            for n, h in enumerate(heads):
                v = v_ref[0, pl.ds(off, blk), h * V_HEAD:(h + 1) * V_HEAD]
                pv = _dot(p_ref[n], jnp.concatenate([v, ones], axis=1))
                acc_ref[n] = jnp.tile(a_ref[n], (1, 2)) * acc_ref[n] + pv

        def below_diagonal(j, carry):
            block(pl.multiple_of(j * blk, blk), masked=False)
            return carry

        lax.fori_loop(0, qi, below_diagonal, 0)
        if h0 == 0:
            for src, dst in zip(cast_src, cast_dst):
                dst[...] = src[...].astype(BF16)
        block(pl.multiple_of(qi * blk, blk), masked=True)
        for n, h in enumerate(heads):
            acc = acc_ref[n]
            o_ref[0, :, h * V_HEAD:(h + 1) * V_HEAD] = (acc[:, :V_HEAD] * (1.0 / acc[:, V_HEAD:])).astype(BF16)


def _attn_prompt(q_cat, k_cat, v, *, cast=(), blk=512, hb=4, hp=4):
    b, t, _ = q_cat.shape
    assert STRIP <= (1 << CHUNK_SHIFT) and blk % (1 << CHUNK_SHIFT) == 0
    ng, nq = N_HEADS // hb, t // blk
    cast_specs = []
    for w in cast:
        rows, cols = w.shape
        assert rows % (b * ng * BF16_ROWS) == 0 and cols % (nq * LANES) == 0, w.shape
        cast_specs.append(pl.BlockSpec((rows // (b * ng), cols // nq), lambda i, g, q: (i * ng + g, q)))
    return pl.pallas_call(
        functools.partial(_attn_prompt_kernel, blk=blk, hb=hb, hp=hp, n_cast=len(cast)),
        grid=(b, ng, nq),
        in_specs=[
            pl.BlockSpec((1, blk, hb * HEAD_W), lambda i, g, q: (i, q, g)),
            pl.BlockSpec((1, t, hb * HEAD_W), lambda i, g, q: (i, 0, g)),
            pl.BlockSpec((1, t, hb * V_HEAD), lambda i, g, q: (i, 0, g)),
            *cast_specs,
        ],
        out_specs=[pl.BlockSpec((1, blk, hb * V_HEAD), lambda i, g, q: (i, q, g)), *cast_specs],
        out_shape=[
            jax.ShapeDtypeStruct((b, t, N_HEADS * V_HEAD), BF16),
            *[jax.ShapeDtypeStruct(w.shape, BF16) for w in cast],
        ],
        scratch_shapes=[
            pltpu.VMEM((hp, blk, blk), F32),
            pltpu.VMEM((hp, blk, blk), BF16),
            pltpu.VMEM((hp, blk, LANES), F32),
            pltpu.VMEM((hp, blk, LANES), F32),
            pltpu.VMEM((hp, blk, LANES), F32),
            pltpu.VMEM((hp, blk, 2 * V_HEAD), F32),
        ],
        compiler_params=_params(("parallel", "parallel", "arbitrary"), 48),
        name="attn_prompt",
    )(q_cat, k_cat, v, *cast)


def _q_sample_kernel(ql_ref, wq_ref, wk_ref, qt_ref, qa_ref, qr_ref, *, kl):
    ql = ql_ref[...]
    qt = qt_ref[...]
    for h in range(N_HEADS):
        q = _dot(ql, wq_ref[:, h * HEAD_W:(h + 1) * HEAD_W]) * qt
        qn = q[:, :QK_NOPE].astype(BF16)
        qa_ref[:, h * kl:(h + 1) * kl] = _dot_t(qn, wk_ref[:, h * QK_NOPE:(h + 1) * QK_NOPE]).astype(BF16)
        y = q[:, QK_NOPE:]
        qr_ref[:, h * 2 * QK_ROPE:(h + 1) * 2 * QK_ROPE] = (y + pltpu.roll(y, QK_ROPE, axis=1)).astype(BF16)


def _q_sample(q_lat, wq_cat, w_uk, qt_tab, *, tm=512):
    n, ql = q_lat.shape
    kl = w_uk.shape[0]
    row = lambda i: (i, 0)
    fixed = lambda i: (0, 0)
    return pl.pallas_call(
        functools.partial(_q_sample_kernel, kl=kl),
        grid=(n // tm,),
        in_specs=[
            pl.BlockSpec((tm, ql), row),
            pl.BlockSpec(wq_cat.shape, fixed),
            pl.BlockSpec(w_uk.shape, fixed),
            pl.BlockSpec((tm, HEAD_W), row),
        ],
        out_specs=[
            pl.BlockSpec((tm, N_HEADS * kl), row),
            pl.BlockSpec((tm, N_HEADS * 2 * QK_ROPE), row),
        ],
        out_shape=[
            jax.ShapeDtypeStruct((n, N_HEADS * kl), BF16),
            jax.ShapeDtypeStruct((n, N_HEADS * 2 * QK_ROPE), BF16),
        ],
        compiler_params=_params(("parallel",), 48),
        name="q_sample",
    )(q_lat, wq_cat, w_uk, qt_tab)


def _attn_sample_kernel(qa_ref, qr_ref, cc_ref, ck_ref, nc_ref, nk_ref, wv_ref, o_ref, *, ts, past, kl):
    rows = N_HEADS * ts
    qs = jnp.concatenate([qa_ref[:, h * kl:(h + 1) * kl] for h in range(N_HEADS)], axis=0)
    qr = jnp.concatenate(
        [qr_ref[:, h * 2 * QK_ROPE:h * 2 * QK_ROPE + QK_ROPE] for h in range(N_HEADS)], axis=0)
    kc = cc_ref[0].astype(BF16)
    krc = ck_ref[0].astype(BF16)
    kn = nc_ref[...].astype(BF16)
    krn = nk_ref[:, :QK_ROPE].astype(BF16)
    s_c = _dot_t(qs, kc) + _dot_t(qr, krc)
    s_n = _dot_t(qs, kn) + _dot_t(qr, krn)
    q_chunk = (past + lax.broadcasted_iota(jnp.int32, (rows, 1), 0) % ts) >> CHUNK_SHIFT
    s_c = jnp.where((lax.broadcasted_iota(jnp.int32, (1, past), 1) >> CHUNK_SHIFT) <= q_chunk, s_c, NEG_INF)
    s_n = jnp.where(((past + lax.broadcasted_iota(jnp.int32, (1, ts), 1)) >> CHUNK_SHIFT) <= q_chunk, s_n, NEG_INF)
    m = jnp.maximum(jnp.max(s_c, axis=-1, keepdims=True), jnp.max(s_n, axis=-1, keepdims=True))
    p_c = jnp.exp2(s_c - m)
    p_n = jnp.exp2(s_n - m)
    l = jnp.sum(p_c, axis=-1, keepdims=True) + jnp.sum(p_n, axis=-1, keepdims=True)
    o_lat = ((_dot(p_c.astype(BF16), kc) + _dot(p_n.astype(BF16), kn)) * (1.0 / l)).astype(BF16)
    for h in range(N_HEADS):
        o_ref[:, h * V_HEAD:(h + 1) * V_HEAD] = _dot(
            o_lat[h * ts:(h + 1) * ts], wv_ref[:, h * V_HEAD:(h + 1) * V_HEAD]).astype(BF16)


def _attn_sample(q_abs, q_rope, cache_ckv, cache_kr, ckv_new, kr_new, w_uv, *, ts):
    b, past, kl = cache_ckv.shape
    row = lambda i: (i, 0)
    return pl.pallas_call(
        functools.partial(_attn_sample_kernel, ts=ts, past=past, kl=kl),
        grid=(b,),
        in_specs=[
            pl.BlockSpec((ts, q_abs.shape[1]), row),
            pl.BlockSpec((ts, q_rope.shape[1]), row),
            pl.BlockSpec((1, past, kl), lambda i: (i, 0, 0)),
            pl.BlockSpec((1, past, cache_kr.shape[2]), lambda i: (i, 0, 0)),
            pl.BlockSpec((ts, kl), row),
            pl.BlockSpec((ts, kr_new.shape[1]), row),
            pl.BlockSpec(w_uv.shape, lambda i: (0, 0)),
        ],
        out_specs=pl.BlockSpec((ts, N_HEADS * V_HEAD), row),
        out_shape=jax.ShapeDtypeStruct((b * ts, N_HEADS * V_HEAD), BF16),
        compiler_params=_params(("parallel",), 48),
        name="attn_sample",
    )(q_abs, q_rope, cache_ckv, cache_kr, ckv_new, kr_new, w_uv)


def _mix_out_kernel(u_ref, pa_ref, ob_ref, h_ref, wga_ref, wgb_ref, wpo_ref, woa_ref, wout_ref, acc_ref):
    j = pl.program_id(1)
    u = u_ref[...]
    gate_a = jax.nn.sigmoid(_dot_t(u, wga_ref[...]))
    gate_b = jax.nn.sigmoid(_dot_t(u, wgb_ref[...]))
    a = _dot(pa_ref[...], wpo_ref[...])
    b = _dot(ob_ref[...], woa_ref[...])
    merged = (gate_a * a + gate_b * b).astype(BF16)

    @pl.when(j == 0)
    def _():
        acc_ref[...] = h_ref[...]

    acc_ref[...] += _dot(merged, wout_ref[...])


def _mix_out(u, pooled, o_attn, h, w_in_t, gate_row0, w_pool_out, w_o_attn, w_out, *, tm=512, tc=512):
    n, d = h.shape
    nc = d // tc
    row = lambda i, j: (i, 0)
    col = lambda i, j: (0, j)
    return pl.pallas_call(
        _mix_out_kernel,
        grid=(n // tm, nc),
        in_specs=[
            pl.BlockSpec((tm, d), row),
            pl.BlockSpec((tm, pooled.shape[1]), row),
            pl.BlockSpec((tm, o_attn.shape[1]), row),
            pl.BlockSpec((tm, d), row),
            pl.BlockSpec((pl.Element(tc), pl.Element(d)),
                         lambda i, j: (pl.multiple_of(gate_row0 + j * tc, BF16_ROWS), 0)),
            pl.BlockSpec((pl.Element(tc), pl.Element(d)),
                         lambda i, j: (pl.multiple_of(gate_row0 + d + j * tc, BF16_ROWS), 0)),
            pl.BlockSpec((w_pool_out.shape[0], tc), col),
            pl.BlockSpec((w_o_attn.shape[0], tc), col),
            pl.BlockSpec((tc, d), lambda i, j: (j, 0)),
        ],
        out_specs=pl.BlockSpec((tm, d), row),
        out_shape=jax.ShapeDtypeStruct((n, d), F32),
        compiler_params=_params(("parallel", "arbitrary"), 56),
        name="mix_out",
    )(u, pooled, o_attn, h, w_in_t, w_in_t, w_pool_out, w_o_attn, w_out)


def _rope_tables(pos):
    half = QK_ROPE // 2
    inv = ROPE_THETA ** (-jnp.arange(half, dtype=F32) * 2.0 / QK_ROPE)
    ang = pos.astype(F32)[:, None] * inv[None, :]
    c, s = jnp.cos(ang), jnp.sin(ang)
    cs = jnp.concatenate([c, c, -s, s], axis=-1)
    qt = (SM_SCALE * LOG2_E) * jnp.concatenate([jnp.ones((pos.shape[0], QK_NOPE), F32), cs], axis=-1)
    return cs, qt


def _dup_rope_cols(w):
    half = QK_ROPE // 2
    x1, x2 = w[..., :half], w[..., half:]
    return jnp.concatenate([x1, x2, x2, x1], axis=-1)


def kernel(x_prompt, x_sample, cache_ckv, cache_krope, state_pool, g_ffn1, w1_gate, w1_up, w1_down, g_mix, w_in, g_q_lat, g_kv_lat, w_uq, w_uk, w_uv, w_o_attn, w_pool, pool_scale, w_pool_out, w_out, g_ffn2, w2_gate, w2_up, w2_down, g_final):
    bp, tp, d = x_prompt.shape
    bs, ts, _ = x_sample.shape
    assert g_ffn1.shape[0] == 1, "single-layer stack only"
    assert ts >= POOL_HIST and tp >= POOL_HIST
    past = cache_ckv.shape[2]
    pw = pool_scale.shape[1]
    ql = g_q_lat.shape[1]
    kl = g_kv_lat.shape[1]
    o3 = pw + ql + kl
    o4 = o3 + QK_ROPE

    cs_p, qt_p = _rope_tables(jnp.arange(tp, dtype=jnp.int32))
    cs_s, qt_s = _rope_tables(past + jnp.arange(ts, dtype=jnp.int32))
    cs_s = jnp.tile(cs_s, (bs, 1))
    qt_s = jnp.tile(qt_s, (bs, 1))

    bf = lambda w: w[0].astype(BF16)
    vec = lambda g: g[0].reshape(1, -1)
    w_in_t = w_in[0].T.astype(BF16)
    w_kr = _dup_rope_cols(w_in_t[o3:o4].T).T
    wq_cat = jnp.concatenate(
        [w_uq[0][..., :QK_NOPE], _dup_rope_cols(w_uq[0][..., QK_NOPE:])], axis=-1
    ).reshape(ql, N_HEADS * HEAD_W).astype(BF16)
    wk = w_uk[0].reshape(kl, N_HEADS * QK_NOPE).astype(BF16)
    wv = w_uv[0].reshape(kl, N_HEADS * V_HEAD).astype(BF16)
    wpool = bf(w_pool)
    hist_s = jnp.pad(state_pool[0], ((0, 0), (POOL_PAD - POOL_HIST, 0), (0, 0)))
    g_last = g_final.reshape(1, -1)

    def in_proj(u, cs_tab):
        return _inproj(u, w_in_t, w_kr, vec(g_q_lat), vec(g_kv_lat), cs_tab, pw=pw, ql=ql, kl=kl)


    h1_s, u_s, w1g, w1u, w1d = _ffn(
        x_sample.reshape(bs * ts, d), vec(g_ffn1), w1_gate[0], w1_up[0], w1_down[0], vec(g_mix),
        emit_normed=True, cast_weights=True)
    z, q_lat, ckv_s, kr_s = in_proj(u_s, cs_s)
    z_s = z.reshape(bs, ts, pw)
    pooled_s = _pool_sample(z_s, hist_s, wpool, vec(pool_scale), pos0=past)
    q_abs_s, q_rope_s = _q_sample(q_lat, wq_cat, wk, qt_s)

    h1, u = _ffn(x_prompt.reshape(bp * tp, d), vec(g_ffn1), w1g, w1u, w1d, vec(g_mix), emit_normed=True)
    z, q_lat, ckv_p, kr_p = in_proj(u, cs_p)
    z_p = z.reshape(bp, tp, pw)
    pooled = _pool_prompt(z_p, wpool, vec(pool_scale)).reshape(bp * tp, pw)
    q_cat, k_cat, v = _qkv_prompt(q_lat, ckv_p, kr_p, wq_cat, wk, wv, qt_p)
    o, w2g, w2u, w2d, wpo, woa, wout = _attn_prompt(
        q_cat.reshape(bp, tp, -1), k_cat.reshape(bp, tp, -1), v.reshape(bp, tp, -1),
        cast=(w2_gate[0], w2_up[0], w2_down[0], w_pool_out[0], w_o_attn[0], w_out[0]))
    h2 = _mix_out(u, pooled, o.reshape(bp * tp, -1), h1, w_in_t, o4, wpo, woa, wout)
    (y_p,) = _ffn(h2, vec(g_ffn2), w2g, w2u, w2d, g_last, emit_normed=False)

    o_s = _attn_sample(q_abs_s, q_rope_s, cache_ckv[0], cache_krope[0], ckv_s, kr_s, wv, ts=ts)
    h2 = _mix_out(u_s, pooled_s, o_s, h1_s, w_in_t, o4, wpo, woa, wout)
    (y_s,) = _ffn(h2, vec(g_ffn2), w2g, w2u, w2d, g_last, emit_normed=False)

    return (
        y_p.reshape(bp, tp, d),
        y_s.reshape(bs, ts, d),
        ckv_p.reshape(1, bp, tp, kl),
        kr_p[:, :QK_ROPE].reshape(1, bp, tp, QK_ROPE),
        z_p[None, :, tp - POOL_HIST:],
        ckv_s.reshape(1, bs, ts, kl),
        kr_s[:, :QK_ROPE].reshape(1, bs, ts, QK_ROPE),
        z_s[None, :, ts - POOL_HIST:],
    )
```

```python
import functools

import jax
import jax.numpy as jnp
from jax import lax
from jax.experimental import pallas as pl
from jax.experimental.pallas import tpu as pltpu

F32 = jnp.float32
BF16 = jnp.bfloat16

CHUNK_SHIFT = 6
N_HEADS = 16
QK_NOPE = 128
QK_ROPE = 64
V_HEAD = 128
HEAD_W = 256
LANES = 128
BF16_ROWS = 16
STRIP = 32
POOL_WINDOWS = (2, 4, 8, 16)
POOL_HIST = 15
POOL_PAD = 16
ROPE_THETA = 10000.0
EPS = 1e-6
SM_SCALE = (QK_NOPE + QK_ROPE) ** -0.5
LOG2_E = 1.4426950408889634
NEG_INF = -1e30
MIB = 1024 * 1024


def _dot(a, b):
    return jnp.dot(a, b, preferred_element_type=F32)


def _dot_t(a, b):
    return lax.dot_general(a, b, (((1,), (1,)), ((), ())), preferred_element_type=F32)


def _rms(x, g):
    return x * lax.rsqrt(jnp.mean(x * x, axis=-1, keepdims=True) + EPS) * g


def _params(semantics, vmem_mib):
    return pltpu.CompilerParams(dimension_semantics=semantics, vmem_limit_bytes=vmem_mib * MIB)


def _ffn_kernel(x_ref, g_ref, wg_ref, wu_ref, wd_ref, gn_ref, *rest, emit_normed, cast_weights, n_side):
    rest = list(rest)
    side_src = [rest.pop(0) for _ in range(n_side)]
    acc_ref = rest.pop(0)
    u_ref = rest.pop(0) if emit_normed else None
    w_dst = [rest.pop(0) for _ in range(3 if cast_weights else 0)]
    side_dst = [rest.pop(0) for _ in range(n_side)]
    (xn_ref,) = rest
    j = pl.program_id(1)

    @pl.when(j == 0)
    def _():
        xn_ref[...] = _rms(x_ref[...], g_ref[...]).astype(BF16)
        acc_ref[...] = jnp.zeros_like(acc_ref)

    for src, dst in zip(side_src, side_dst):
        dst[...] = src[...].astype(BF16)
    wg, wu, wd = wg_ref[...], wu_ref[...], wd_ref[...]
    if cast_weights:
        wg, wu, wd = wg.astype(BF16), wu.astype(BF16), wd.astype(BF16)
        for dst, w in zip(w_dst, (wg, wu, wd)):
            dst[...] = w
    xn = xn_ref[...]
    gate = _dot(xn, wg)
    up = _dot(xn, wu)
    act = (gate * jax.nn.sigmoid(gate) * up).astype(BF16)
    acc_ref[...] += _dot(act, wd)

    @pl.when(j == pl.num_programs(1) - 1)
    def _():
        h = x_ref[...] + 0.5 * acc_ref[...]
        if emit_normed:
            acc_ref[...] = h
            u_ref[...] = _rms(h, gn_ref[...]).astype(BF16)
        else:
            acc_ref[...] = _rms(h, gn_ref[...])


def _side_cast_specs(mats, n0, n1, to01):
    specs = []
    for w in mats:
        rows, cols = w.shape
        if rows % (n0 * BF16_ROWS) == 0 and cols % (n1 * LANES) == 0:
            specs.append(pl.BlockSpec((rows // n0, cols // n1), lambda *g: to01(*g)))
        else:
            assert rows % (n1 * BF16_ROWS) == 0 and cols % (n0 * LANES) == 0, w.shape
            specs.append(pl.BlockSpec((rows // n1, cols // n0), lambda *g: to01(*g)[::-1]))
    return specs


def _ffn(x, g, wg, wu, wd, g_next, *, emit_normed, cast_weights=False, side_cast=(), tm=1024, tf=256):
    n, d = x.shape
    f = wg.shape[1]
    row = lambda i, j: (i, 0)
    side_specs = _side_cast_specs(side_cast, n // tm, f // tf, lambda i, j: (i, j))
    assert not cast_weights or n == tm
    rows_mode = dict(pipeline_mode=pl.Buffered(1)) if n == tm else {}
    w_up_spec = pl.BlockSpec((d, tf), lambda i, j: (0, j))
    w_down_spec = pl.BlockSpec((tf, d), lambda i, j: (j, 0))
    out_shape = [jax.ShapeDtypeStruct((n, d), F32)]
    out_specs = [pl.BlockSpec((tm, d), row, **rows_mode)]
    if emit_normed:
        out_shape.append(jax.ShapeDtypeStruct((n, d), BF16))
        out_specs.append(pl.BlockSpec((tm, d), row, **rows_mode))
    if cast_weights:
        out_shape += [jax.ShapeDtypeStruct(w.shape, BF16) for w in (wg, wu, wd)]
        out_specs += [w_up_spec, w_up_spec, w_down_spec]
    out_shape += [jax.ShapeDtypeStruct(w.shape, BF16) for w in side_cast]
    out_specs += side_specs
    return pl.pallas_call(
        functools.partial(_ffn_kernel, emit_normed=emit_normed, cast_weights=cast_weights, n_side=len(side_cast)),
        grid=(n // tm, f // tf),
        in_specs=[
            pl.BlockSpec((tm, d), row, **rows_mode),
            pl.BlockSpec((1, d), lambda i, j: (0, 0)),
            w_up_spec,
            w_up_spec,
            w_down_spec,
            pl.BlockSpec((1, d), lambda i, j: (0, 0)),
            *side_specs,
        ],
        out_specs=out_specs,
        out_shape=out_shape,
        scratch_shapes=[pltpu.VMEM((tm, d), BF16)],
        compiler_params=_params(("parallel", "arbitrary"), 62),
        name="ffn_norm" if emit_normed else "ffn_final",
    )(x, g, wg, wu, wd, g_next, *side_cast)


def _inproj_kernel(u_ref, w_ref, wkr_ref, gq_ref, gkv_ref, cs_ref, z_ref, q_ref, ckv_ref, kr_ref, *, pw, ql, kl):
    u = u_ref[...]
    proj = _dot_t(u, w_ref[...])
    z_ref[...] = proj[:, :pw]
    q_ref[...] = _rms(proj[:, pw:pw + ql], gq_ref[...]).astype(BF16)
    ckv_ref[...] = _rms(proj[:, pw + ql:pw + ql + kl], gkv_ref[...])
    y = _dot_t(u, wkr_ref[...]) * cs_ref[...]
    kr_ref[...] = y + pltpu.roll(y, QK_ROPE, axis=1)


def _inproj(u, w_in_t, w_kr, g_q, g_kv, cs_tab, *, pw, ql, kl, tm=1024):
    n, d = u.shape
    period = cs_tab.shape[0] // tm
    row = lambda i: (i, 0)
    fixed = lambda i: (0, 0)
    return pl.pallas_call(
        functools.partial(_inproj_kernel, pw=pw, ql=ql, kl=kl),
        grid=(n // tm,),
        in_specs=[
            pl.BlockSpec((tm, d), row),
            pl.BlockSpec((pw + ql + kl, d), fixed),
            pl.BlockSpec(w_kr.shape, fixed),
            pl.BlockSpec((1, ql), fixed),
            pl.BlockSpec((1, kl), fixed),
            pl.BlockSpec((tm, 2 * QK_ROPE), lambda i: (i % period, 0)),
        ],
        out_specs=[
            pl.BlockSpec((tm, pw), row),
            pl.BlockSpec((tm, ql), row),
            pl.BlockSpec((tm, kl), row),
            pl.BlockSpec((tm, 2 * QK_ROPE), row),
        ],
        out_shape=[
            jax.ShapeDtypeStruct((n, pw), F32),
            jax.ShapeDtypeStruct((n, ql), BF16),
            jax.ShapeDtypeStruct((n, kl), F32),
            jax.ShapeDtypeStruct((n, 2 * QK_ROPE), F32),
        ],
        compiler_params=_params(("parallel",), 48),
        name="in_proj",
    )(u, w_in_t, w_kr, g_q, g_kv, cs_tab)


def _pool_rows(zext, pos0, w_ref, scale_ref, out_ref, out_row0, seg_rows, n_seg):
    p = zext.shape[1]
    gc = p // len(POOL_WINDOWS)
    ext = POOL_PAD + seg_rows
    pos = pos0 + lax.broadcasted_iota(jnp.int32, (seg_rows, 1), 0)
    for g, w in enumerate(POOL_WINDOWS):
        cols = slice(g * gc, (g + 1) * gc)
        s = zext[:, cols]
        win = s
        step = 1
        while step < w:
            win = win + pltpu.roll(win, step, axis=0)
            step *= 2
        cnt = jnp.minimum(pos + 1, w).astype(F32)
        for b in range(n_seg):
            lo = b * ext + POOL_PAD
            pooled = win[lo:lo + seg_rows] / cnt - s[lo:lo + seg_rows]
            mixed = _dot(pooled.astype(BF16), w_ref[g]) * scale_ref[:, cols]
            out_ref[pl.ds(out_row0 + b * seg_rows, seg_rows), cols] = mixed.astype(BF16)


def _pool_prompt_kernel(cur_ref, prev_ref, w_ref, scale_ref, out_ref, ext_ref, *, tp):
    t = pl.program_id(1)
    hist = jnp.where(t == 0, 0.0, prev_ref[0])
    ext_ref[0:POOL_PAD, :] = hist
    ext_ref[POOL_PAD:, :] = cur_ref[0]
    _pool_rows(ext_ref[...], t * tp, w_ref, scale_ref, out_ref.at[0], 0, tp, 1)


def _pool_prompt(z, w_pool, pool_scale, *, tp=512):
    b, t, p = z.shape
    hist_blocks = tp // POOL_PAD
    return pl.pallas_call(
        functools.partial(_pool_prompt_kernel, tp=tp),
        grid=(b, t // tp),
        in_specs=[
            pl.BlockSpec((1, tp, p), lambda i, j: (i, j, 0)),
            pl.BlockSpec((1, POOL_PAD, p), lambda i, j: (i, jnp.maximum(j * hist_blocks - 1, 0), 0)),
            pl.BlockSpec(w_pool.shape, lambda i, j: (0, 0, 0)),
            pl.BlockSpec((1, p), lambda i, j: (0, 0)),
        ],
        out_specs=pl.BlockSpec((1, tp, p), lambda i, j: (i, j, 0)),
        out_shape=jax.ShapeDtypeStruct((b, t, p), BF16),
        scratch_shapes=[pltpu.VMEM((POOL_PAD + tp, p), F32)],
        compiler_params=_params(("parallel", "arbitrary"), 32),
        name="pool_prompt",
    )(z, z, w_pool, pool_scale)


def _pool_sample_kernel(cur_ref, hist_ref, w_ref, scale_ref, out_ref, ext_ref, *, nb, ts, pos0):
    ext = POOL_PAD + ts
    for b in range(nb):
        ext_ref[b * ext:b * ext + POOL_PAD, :] = hist_ref[b]
        ext_ref[b * ext + POOL_PAD:(b + 1) * ext, :] = cur_ref[b]
    _pool_rows(ext_ref[...], pos0, w_ref, scale_ref, out_ref, 0, ts, nb)


def _pool_sample(z, hist, w_pool, pool_scale, *, pos0, nb=8):
    b, ts, p = z.shape
    return pl.pallas_call(
        functools.partial(_pool_sample_kernel, nb=nb, ts=ts, pos0=pos0),
        grid=(b // nb,),
        in_specs=[
            pl.BlockSpec((nb, ts, p), lambda i: (i, 0, 0)),
            pl.BlockSpec((nb, POOL_PAD, p), lambda i: (i, 0, 0)),
            pl.BlockSpec(w_pool.shape, lambda i: (0, 0, 0)),
            pl.BlockSpec((1, p), lambda i: (0, 0)),
        ],
        out_specs=pl.BlockSpec((nb * ts, p), lambda i: (i, 0)),
        out_shape=jax.ShapeDtypeStruct((b * ts, p), BF16),
        scratch_shapes=[pltpu.VMEM((nb * (POOL_PAD + ts), p), F32)],
        compiler_params=_params(("parallel",), 32),
        name="pool_sample",
    )(z, hist, w_pool, pool_scale)


def _qkv_prompt_kernel(ql_ref, ckv_ref, kr_ref, wq_ref, wk_ref, wv_ref, qt_ref, q_ref, k_ref, v_ref):
    ql = ql_ref[...]
    ckv = ckv_ref[...].astype(BF16)
    kr = kr_ref[...].astype(BF16)
    qt = qt_ref[...]
    for h in range(N_HEADS):
        q_ref[:, h * HEAD_W:(h + 1) * HEAD_W] = (_dot(ql, wq_ref[:, h * HEAD_W:(h + 1) * HEAD_W]) * qt).astype(BF16)
    kn = _dot(ckv, wk_ref[...])
    for h in range(N_HEADS):
        k_ref[:, h * HEAD_W:h * HEAD_W + QK_NOPE] = kn[:, h * QK_NOPE:(h + 1) * QK_NOPE].astype(BF16)
        k_ref[:, h * HEAD_W + QK_NOPE:(h + 1) * HEAD_W] = kr
    v_ref[...] = _dot(ckv, wv_ref[...]).astype(BF16)


def _qkv_prompt(q_lat, ckv, krblk, wq_cat, w_uk, w_uv, qt_tab, *, tm=512):
    n = q_lat.shape[0]
    period = qt_tab.shape[0] // tm
    row = lambda i: (i, 0)
    fixed = lambda i: (0, 0)
    return pl.pallas_call(
        _qkv_prompt_kernel,
        grid=(n // tm,),
        in_specs=[
            pl.BlockSpec((tm, q_lat.shape[1]), row),
            pl.BlockSpec((tm, ckv.shape[1]), row),
            pl.BlockSpec((tm, krblk.shape[1]), row),
            pl.BlockSpec(wq_cat.shape, fixed),
            pl.BlockSpec(w_uk.shape, fixed),
            pl.BlockSpec(w_uv.shape, fixed),
            pl.BlockSpec((tm, HEAD_W), lambda i: (i % period, 0)),
        ],
        out_specs=[
            pl.BlockSpec((tm, N_HEADS * HEAD_W), row),
            pl.BlockSpec((tm, N_HEADS * HEAD_W), row),
            pl.BlockSpec((tm, N_HEADS * V_HEAD), row),
        ],
        out_shape=[
            jax.ShapeDtypeStruct((n, N_HEADS * HEAD_W), BF16),
            jax.ShapeDtypeStruct((n, N_HEADS * HEAD_W), BF16),
            jax.ShapeDtypeStruct((n, N_HEADS * V_HEAD), BF16),
        ],
        compiler_params=_params(("parallel",), 48),
        name="qkv_prompt",
    )(q_lat, ckv, krblk, wq_cat, w_uk, w_uv, qt_tab)


def _softmax_strips(s_ref, p_ref, m_ref, mc_ref, a_ref, masked):
    blk = s_ref.shape[1]

    def strip(r0):
        if not masked:
            return s_ref[r0:r0 + STRIP, :], blk
        visible = ((r0 >> CHUNK_SHIFT) + 1) << CHUNK_SHIFT
        width = -(-visible // LANES) * LANES
        s = s_ref[r0:r0 + STRIP, :width]
        return jnp.where(lax.broadcasted_iota(jnp.int32, s.shape, 1) < visible, s, NEG_INF), width

    for r0 in range(0, blk, STRIP):
        s, _ = strip(r0)
        mc_ref[r0:r0 + STRIP, :] = jnp.broadcast_to(jnp.max(s, axis=-1, keepdims=True), (STRIP, LANES))
    m_old = m_ref[...]
    m_new = jnp.maximum(m_old, mc_ref[...])
    m_ref[...] = m_new
    a_ref[...] = jnp.exp2(m_old - m_new)
    for r0 in range(0, blk, STRIP):
        s, width = strip(r0)
        p_ref[r0:r0 + STRIP, :width] = jnp.exp2(s - jnp.tile(m_ref[r0:r0 + STRIP, :], (1, width // LANES))).astype(BF16)
        if width < blk:
            p_ref[r0:r0 + STRIP, width:] = jnp.zeros((STRIP, blk - width), BF16)


def _attn_prompt_kernel(q_ref, k_ref, v_ref, *rest, blk, hb, hp, n_cast):
    cast_src, o_ref, cast_dst = rest[:n_cast], rest[n_cast], rest[n_cast + 1:2 * n_cast + 1]
    s_ref, p_ref, m_ref, mc_ref, a_ref, acc_ref = rest[2 * n_cast + 1:]
    qi = pl.program_id(2)
    ones = jnp.ones((blk, V_HEAD), BF16)
    for h0 in range(0, hb, hp):
        heads = range(h0, h0 + hp)
        m_ref[...] = jnp.full(m_ref.shape, NEG_INF, F32)
        acc_ref[...] = jnp.zeros_like(acc_ref)

        def block(off, masked, heads=heads):
            for n, h in enumerate(heads):
                q = q_ref[0, :, h * HEAD_W:(h + 1) * HEAD_W]
                s_ref[n] = _dot_t(q, k_ref[0, pl.ds(off, blk), h * HEAD_W:(h + 1) * HEAD_W])
            for n in range(hp):
                _softmax_strips(s_ref.at[n], p_ref.at[n], m_ref.at[n], mc_ref.at[n], a_ref.at[n], masked)
            for n, h in enumerate(heads):
                v = v_ref[0, pl.ds(off, blk), h * V_HEAD:(h + 1) * V_HEAD]
                pv = _dot(p_ref[n], jnp.concatenate([v, ones], axis=1))
                acc_ref[n] = jnp.tile(a_ref[n], (1, 2)) * acc_ref[n] + pv

        def below_diagonal(j, carry):
            block(pl.multiple_of(j * blk, blk), masked=False)
            return carry

        lax.fori_loop(0, qi, below_diagonal, 0)
        if h0 == 0:
            for src, dst in zip(cast_src, cast_dst):
                dst[...] = src[...].astype(BF16)
        block(pl.multiple_of(qi * blk, blk), masked=True)
        for n, h in enumerate(heads):
            acc = acc_ref[n]
            o_ref[0, :, h * V_HEAD:(h + 1) * V_HEAD] = (acc[:, :V_HEAD] * (1.0 / acc[:, V_HEAD:])).astype(BF16)


def _attn_prompt(q_cat, k_cat, v, *, cast=(), blk=512, hb=4, hp=4):
    b, t, _ = q_cat.shape
    assert STRIP <= (1 << CHUNK_SHIFT) and blk % (1 << CHUNK_SHIFT) == 0
    ng, nq = N_HEADS // hb, t // blk
    cast_specs = _side_cast_specs(cast, b * ng, nq, lambda i, g, q: (i * ng + g, q))
    return pl.pallas_call(
        functools.partial(_attn_prompt_kernel, blk=blk, hb=hb, hp=hp, n_cast=len(cast)),
        grid=(b, ng, nq),
        in_specs=[
            pl.BlockSpec((1, blk, hb * HEAD_W), lambda i, g, q: (i, q, g)),
            pl.BlockSpec((1, t, hb * HEAD_W), lambda i, g, q: (i, 0, g)),
            pl.BlockSpec((1, t, hb * V_HEAD), lambda i, g, q: (i, 0, g)),
            *cast_specs,
        ],
        out_specs=[pl.BlockSpec((1, blk, hb * V_HEAD), lambda i, g, q: (i, q, g)), *cast_specs],
        out_shape=[
            jax.ShapeDtypeStruct((b, t, N_HEADS * V_HEAD), BF16),
            *[jax.ShapeDtypeStruct(w.shape, BF16) for w in cast],
        ],
        scratch_shapes=[
            pltpu.VMEM((hp, blk, blk), F32),
            pltpu.VMEM((hp, blk, blk), BF16),
            pltpu.VMEM((hp, blk, LANES), F32),
            pltpu.VMEM((hp, blk, LANES), F32),
            pltpu.VMEM((hp, blk, LANES), F32),
            pltpu.VMEM((hp, blk, 2 * V_HEAD), F32),
        ],
        compiler_params=_params(("parallel", "parallel", "arbitrary"), 48),
        name="attn_prompt",
    )(q_cat, k_cat, v, *cast)


def _q_sample_kernel(ql_ref, wq_ref, wk_ref, qt_ref, qa_ref, qr_ref, *, kl):
    ql = ql_ref[...]
    qt = qt_ref[...]
    for h in range(N_HEADS):
        q = _dot(ql, wq_ref[:, h * HEAD_W:(h + 1) * HEAD_W]) * qt
        qn = q[:, :QK_NOPE].astype(BF16)
        qa_ref[:, h * kl:(h + 1) * kl] = _dot_t(qn, wk_ref[:, h * QK_NOPE:(h + 1) * QK_NOPE]).astype(BF16)
        y = q[:, QK_NOPE:]
        qr_ref[:, h * 2 * QK_ROPE:(h + 1) * 2 * QK_ROPE] = (y + pltpu.roll(y, QK_ROPE, axis=1)).astype(BF16)


def _q_sample(q_lat, wq_cat, w_uk, qt_tab, *, tm=512):
    n, ql = q_lat.shape
    kl = w_uk.shape[0]
    row = lambda i: (i, 0)
    fixed = lambda i: (0, 0)
    return pl.pallas_call(
        functools.partial(_q_sample_kernel, kl=kl),
        grid=(n // tm,),
        in_specs=[
            pl.BlockSpec((tm, ql), row),
            pl.BlockSpec(wq_cat.shape, fixed),
            pl.BlockSpec(w_uk.shape, fixed),
            pl.BlockSpec((tm, HEAD_W), row),
        ],
        out_specs=[
            pl.BlockSpec((tm, N_HEADS * kl), row),
            pl.BlockSpec((tm, N_HEADS * 2 * QK_ROPE), row),
        ],
        out_shape=[
            jax.ShapeDtypeStruct((n, N_HEADS * kl), BF16),
            jax.ShapeDtypeStruct((n, N_HEADS * 2 * QK_ROPE), BF16),
        ],
        compiler_params=_params(("parallel",), 48),
        name="q_sample",
    )(q_lat, wq_cat, w_uk, qt_tab)


def _attn_sample_kernel(qa_ref, qr_ref, cc_ref, ck_ref, nc_ref, nk_ref, wv_ref, o_ref, *, ts, past, kl):
    rows = N_HEADS * ts
    qs = jnp.concatenate([qa_ref[:, h * kl:(h + 1) * kl] for h in range(N_HEADS)], axis=0)
    qr = jnp.concatenate(
        [qr_ref[:, h * 2 * QK_ROPE:h * 2 * QK_ROPE + QK_ROPE] for h in range(N_HEADS)], axis=0)
    kc = cc_ref[0].astype(BF16)
    krc = ck_ref[0].astype(BF16)
    kn = nc_ref[...].astype(BF16)
    krn = nk_ref[:, :QK_ROPE].astype(BF16)
    s_c = _dot_t(qs, kc) + _dot_t(qr, krc)
    s_n = _dot_t(qs, kn) + _dot_t(qr, krn)
    q_chunk = (past + lax.broadcasted_iota(jnp.int32, (rows, 1), 0) % ts) >> CHUNK_SHIFT
    s_c = jnp.where((lax.broadcasted_iota(jnp.int32, (1, past), 1) >> CHUNK_SHIFT) <= q_chunk, s_c, NEG_INF)
    s_n = jnp.where(((past + lax.broadcasted_iota(jnp.int32, (1, ts), 1)) >> CHUNK_SHIFT) <= q_chunk, s_n, NEG_INF)
    m = jnp.maximum(jnp.max(s_c, axis=-1, keepdims=True), jnp.max(s_n, axis=-1, keepdims=True))
    p_c = jnp.exp2(s_c - m)
    p_n = jnp.exp2(s_n - m)
    l = jnp.sum(p_c, axis=-1, keepdims=True) + jnp.sum(p_n, axis=-1, keepdims=True)
    o_lat = ((_dot(p_c.astype(BF16), kc) + _dot(p_n.astype(BF16), kn)) * (1.0 / l)).astype(BF16)
    for h in range(N_HEADS):
        o_ref[:, h * V_HEAD:(h + 1) * V_HEAD] = _dot(
            o_lat[h * ts:(h + 1) * ts], wv_ref[:, h * V_HEAD:(h + 1) * V_HEAD]).astype(BF16)


def _attn_sample(q_abs, q_rope, cache_ckv, cache_kr, ckv_new, kr_new, w_uv, *, ts):
    b, past, kl = cache_ckv.shape
    row = lambda i: (i, 0)
    return pl.pallas_call(
        functools.partial(_attn_sample_kernel, ts=ts, past=past, kl=kl),
        grid=(b,),
        in_specs=[
            pl.BlockSpec((ts, q_abs.shape[1]), row),
            pl.BlockSpec((ts, q_rope.shape[1]), row),
            pl.BlockSpec((1, past, kl), lambda i: (i, 0, 0)),
            pl.BlockSpec((1, past, cache_kr.shape[2]), lambda i: (i, 0, 0)),
            pl.BlockSpec((ts, kl), row),
            pl.BlockSpec((ts, kr_new.shape[1]), row),
            pl.BlockSpec(w_uv.shape, lambda i: (0, 0)),
        ],
        out_specs=pl.BlockSpec((ts, N_HEADS * V_HEAD), row),
        out_shape=jax.ShapeDtypeStruct((b * ts, N_HEADS * V_HEAD), BF16),
        compiler_params=_params(("parallel",), 48),
        name="attn_sample",
    )(q_abs, q_rope, cache_ckv, cache_kr, ckv_new, kr_new, w_uv)


def _mix_out_kernel(u_ref, pa_ref, ob_ref, h_ref, wga_ref, wgb_ref, wpo_ref, woa_ref, wout_ref, acc_ref):
    j = pl.program_id(1)
    u = u_ref[...]
    gate_a = jax.nn.sigmoid(_dot_t(u, wga_ref[...]))
    gate_b = jax.nn.sigmoid(_dot_t(u, wgb_ref[...]))
    a = _dot(pa_ref[...], wpo_ref[...])
    b = _dot(ob_ref[...], woa_ref[...])
    merged = (gate_a * a + gate_b * b).astype(BF16)

    @pl.when(j == 0)
    def _():
        acc_ref[...] = h_ref[...]

    acc_ref[...] += _dot(merged, wout_ref[...])


def _mix_out(u, pooled, o_attn, h, w_in_t, gate_row0, w_pool_out, w_o_attn, w_out, *, tm=512, tc=512):
    n, d = h.shape
    nc = d // tc
    row = lambda i, j: (i, 0)
    col = lambda i, j: (0, j)
    return pl.pallas_call(
        _mix_out_kernel,
        grid=(n // tm, nc),
        in_specs=[
            pl.BlockSpec((tm, d), row),
            pl.BlockSpec((tm, pooled.shape[1]), row),
            pl.BlockSpec((tm, o_attn.shape[1]), row),
            pl.BlockSpec((tm, d), row),
            pl.BlockSpec((pl.Element(tc), pl.Element(d)),
                         lambda i, j: (pl.multiple_of(gate_row0 + j * tc, BF16_ROWS), 0)),
            pl.BlockSpec((pl.Element(tc), pl.Element(d)),
                         lambda i, j: (pl.multiple_of(gate_row0 + d + j * tc, BF16_ROWS), 0)),
            pl.BlockSpec((w_pool_out.shape[0], tc), col),
            pl.BlockSpec((w_o_attn.shape[0], tc), col),
            pl.BlockSpec((tc, d), lambda i, j: (j, 0)),
        ],
        out_specs=pl.BlockSpec((tm, d), row),
        out_shape=jax.ShapeDtypeStruct((n, d), F32),
        compiler_params=_params(("parallel", "arbitrary"), 56),
        name="mix_out",
    )(u, pooled, o_attn, h, w_in_t, w_in_t, w_pool_out, w_o_attn, w_out)


def _rope_tables(pos):
    half = QK_ROPE // 2
    inv = ROPE_THETA ** (-jnp.arange(half, dtype=F32) * 2.0 / QK_ROPE)
    ang = pos.astype(F32)[:, None] * inv[None, :]
    c, s = jnp.cos(ang), jnp.sin(ang)
    cs = jnp.concatenate([c, c, -s, s], axis=-1)
    qt = (SM_SCALE * LOG2_E) * jnp.concatenate([jnp.ones((pos.shape[0], QK_NOPE), F32), cs], axis=-1)
    return cs, qt


def _dup_rope_cols(w):
    half = QK_ROPE // 2
    x1, x2 = w[..., :half], w[..., half:]
    return jnp.concatenate([x1, x2, x2, x1], axis=-1)


def kernel(x_prompt, x_sample, cache_ckv, cache_krope, state_pool, g_ffn1, w1_gate, w1_up, w1_down, g_mix, w_in, g_q_lat, g_kv_lat, w_uq, w_uk, w_uv, w_o_attn, w_pool, pool_scale, w_pool_out, w_out, g_ffn2, w2_gate, w2_up, w2_down, g_final):
    bp, tp, d = x_prompt.shape
    bs, ts, _ = x_sample.shape
    assert g_ffn1.shape[0] == 1, "single-layer stack only"
    assert ts >= POOL_HIST and tp >= POOL_HIST
    past = cache_ckv.shape[2]
    pw = pool_scale.shape[1]
    ql = g_q_lat.shape[1]
    kl = g_kv_lat.shape[1]
    o3 = pw + ql + kl
    o4 = o3 + QK_ROPE

    cs_p, qt_p = _rope_tables(jnp.arange(tp, dtype=jnp.int32))
    cs_s, qt_s = _rope_tables(past + jnp.arange(ts, dtype=jnp.int32))
    cs_s = jnp.tile(cs_s, (bs, 1))
    qt_s = jnp.tile(qt_s, (bs, 1))

    bf = lambda w: w[0].astype(BF16)
    vec = lambda g: g[0].reshape(1, -1)
    w_in_t = w_in[0].T.astype(BF16)
    w_kr = _dup_rope_cols(w_in_t[o3:o4].T).T
    wq_cat = jnp.concatenate(
        [w_uq[0][..., :QK_NOPE], _dup_rope_cols(w_uq[0][..., QK_NOPE:])], axis=-1
    ).reshape(ql, N_HEADS * HEAD_W).astype(BF16)
    wk = w_uk[0].reshape(kl, N_HEADS * QK_NOPE).astype(BF16)
    wv = w_uv[0].reshape(kl, N_HEADS * V_HEAD).astype(BF16)
    wpool = bf(w_pool)
    hist_s = jnp.pad(state_pool[0], ((0, 0), (POOL_PAD - POOL_HIST, 0), (0, 0)))
    g_last = g_final.reshape(1, -1)

    def in_proj(u, cs_tab):
        return _inproj(u, w_in_t, w_kr, vec(g_q_lat), vec(g_kv_lat), cs_tab, pw=pw, ql=ql, kl=kl)


    h1_s, u_s, w1g, w1u, w1d = _ffn(
        x_sample.reshape(bs * ts, d), vec(g_ffn1), w1_gate[0], w1_up[0], w1_down[0], vec(g_mix),
        emit_normed=True, cast_weights=True)
    z, q_lat, ckv_s, kr_s = in_proj(u_s, cs_s)
    z_s = z.reshape(bs, ts, pw)
    pooled_s = _pool_sample(z_s, hist_s, wpool, vec(pool_scale), pos0=past)
    q_abs_s, q_rope_s = _q_sample(q_lat, wq_cat, wk, qt_s)

    h1, u, w2g, w2u, w2d = _ffn(
        x_prompt.reshape(bp * tp, d), vec(g_ffn1), w1g, w1u, w1d, vec(g_mix),
        emit_normed=True, side_cast=(w2_gate[0], w2_up[0], w2_down[0]), tm=512, tf=512)
    z, q_lat, ckv_p, kr_p = in_proj(u, cs_p)
    z_p = z.reshape(bp, tp, pw)
    pooled = _pool_prompt(z_p, wpool, vec(pool_scale)).reshape(bp * tp, pw)
    q_cat, k_cat, v = _qkv_prompt(q_lat, ckv_p, kr_p, wq_cat, wk, wv, qt_p)
    o, wpo, woa, wout = _attn_prompt(
        q_cat.reshape(bp, tp, -1), k_cat.reshape(bp, tp, -1), v.reshape(bp, tp, -1),
        cast=(w_pool_out[0], w_o_attn[0], w_out[0]))
    h2 = _mix_out(u, pooled, o.reshape(bp * tp, -1), h1, w_in_t, o4, wpo, woa, wout)
    (y_p,) = _ffn(h2, vec(g_ffn2), w2g, w2u, w2d, g_last, emit_normed=False)

    o_s = _attn_sample(q_abs_s, q_rope_s, cache_ckv[0], cache_krope[0], ckv_s, kr_s, wv, ts=ts)
    h2 = _mix_out(u_s, pooled_s, o_s, h1_s, w_in_t, o4, wpo, woa, wout)
    (y_s,) = _ffn(h2, vec(g_ffn2), w2g, w2u, w2d, g_last, emit_normed=False)

    return (
        y_p.reshape(bp, tp, d),
        y_s.reshape(bs, ts, d),
        ckv_p.reshape(1, bp, tp, kl),
        kr_p[:, :QK_ROPE].reshape(1, bp, tp, QK_ROPE),
        z_p[None, :, tp - POOL_HIST:],
        ckv_s.reshape(1, bs, ts, kl),
        kr_s[:, :QK_ROPE].reshape(1, bs, ts, QK_ROPE),
        z_s[None, :, ts - POOL_HIST:],
    )
```

```python
import functools

import jax
import jax.numpy as jnp
from jax import lax
from jax.experimental import pallas as pl
from jax.experimental.pallas import tpu as pltpu

F32 = jnp.float32
BF16 = jnp.bfloat16

CHUNK_SHIFT = 6
N_HEADS = 16
QK_NOPE = 128
QK_ROPE = 64
V_HEAD = 128
HEAD_W = 256
LANES = 128
BF16_ROWS = 16
STRIP = 32
POOL_WINDOWS = (2, 4, 8, 16)
POOL_HIST = 15
POOL_PAD = 16
ROPE_THETA = 10000.0
EPS = 1e-6
SM_SCALE = (QK_NOPE + QK_ROPE) ** -0.5
LOG2_E = 1.4426950408889634
NEG_INF = -1e30
MIB = 1024 * 1024


def _dot(a, b):
    return jnp.dot(a, b, preferred_element_type=F32)


def _dot_t(a, b):
    return lax.dot_general(a, b, (((1,), (1,)), ((), ())), preferred_element_type=F32)


def _rms(x, g):
    return x * lax.rsqrt(jnp.mean(x * x, axis=-1, keepdims=True) + EPS) * g


def _params(semantics, vmem_mib):
    return pltpu.CompilerParams(dimension_semantics=semantics, vmem_limit_bytes=vmem_mib * MIB)


def _ffn_kernel(x_ref, g_ref, wg_ref, wu_ref, wd_ref, gn_ref, acc_ref, *rest, emit_normed, cast_weights):
    rest = list(rest)
    u_ref = rest.pop(0) if emit_normed else None
    xn_ref = rest.pop()
    w_dst = rest
    j = pl.program_id(1)

    @pl.when(j == 0)
    def _():
        xn_ref[...] = _rms(x_ref[...], g_ref[...]).astype(BF16)
        acc_ref[...] = jnp.zeros_like(acc_ref)

    wg, wu, wd = wg_ref[...], wu_ref[...], wd_ref[...]
    if cast_weights:
        wg, wu, wd = wg.astype(BF16), wu.astype(BF16), wd.astype(BF16)
        for dst, w in zip(w_dst, (wg, wu, wd)):
            dst[...] = w
    xn = xn_ref[...]
    gate = _dot(xn, wg)
    up = _dot(xn, wu)
    act = (gate * jax.nn.sigmoid(gate) * up).astype(BF16)
    acc_ref[...] += _dot(act, wd)

    @pl.when(j == pl.num_programs(1) - 1)
    def _():
        h = x_ref[...] + 0.5 * acc_ref[...]
        if emit_normed:
            acc_ref[...] = h
            u_ref[...] = _rms(h, gn_ref[...]).astype(BF16)
        else:
            acc_ref[...] = _rms(h, gn_ref[...])


def _side_cast_specs(mats, n0, n1, to01):
    specs = []
    for w in mats:
        rows, cols = w.shape
        if rows % (n0 * BF16_ROWS) == 0 and cols % (n1 * LANES) == 0:
            specs.append(pl.BlockSpec((rows // n0, cols // n1), lambda *g: to01(*g)))
        else:
            assert rows % (n1 * BF16_ROWS) == 0 and cols % (n0 * LANES) == 0, w.shape
            specs.append(pl.BlockSpec((rows // n1, cols // n0), lambda *g: to01(*g)[::-1]))
    return specs


def _ffn(x, g, wg, wu, wd, g_next, *, emit_normed, cast_weights=False, tm=1024, tf=256):
    n, d = x.shape
    f = wg.shape[1]
    row = lambda i, j: (i, 0)
    assert not cast_weights or n == tm
    rows_mode = dict(pipeline_mode=pl.Buffered(1)) if n == tm else {}
    w_up_spec = pl.BlockSpec((d, tf), lambda i, j: (0, j))
    w_down_spec = pl.BlockSpec((tf, d), lambda i, j: (j, 0))
    out_shape = [jax.ShapeDtypeStruct((n, d), F32)]
    out_specs = [pl.BlockSpec((tm, d), row, **rows_mode)]
    if emit_normed:
        out_shape.append(jax.ShapeDtypeStruct((n, d), BF16))
        out_specs.append(pl.BlockSpec((tm, d), row, **rows_mode))
    if cast_weights:
        out_shape += [jax.ShapeDtypeStruct(w.shape, BF16) for w in (wg, wu, wd)]
        out_specs += [w_up_spec, w_up_spec, w_down_spec]
    return pl.pallas_call(
        functools.partial(_ffn_kernel, emit_normed=emit_normed, cast_weights=cast_weights),
        grid=(n // tm, f // tf),
        in_specs=[
            pl.BlockSpec((tm, d), row, **rows_mode),
            pl.BlockSpec((1, d), lambda i, j: (0, 0)),
            w_up_spec,
            w_up_spec,
            w_down_spec,
            pl.BlockSpec((1, d), lambda i, j: (0, 0)),
        ],
        out_specs=out_specs,
        out_shape=out_shape,
        scratch_shapes=[pltpu.VMEM((tm, d), BF16)],
        compiler_params=_params(("parallel", "arbitrary"), 62),
        name="ffn_norm" if emit_normed else "ffn_final",
    )(x, g, wg, wu, wd, g_next)


def _inproj_kernel(u_ref, w_ref, wkr_ref, gq_ref, gkv_ref, cs_ref, z_ref, q_ref, ckv_ref, kr_ref, *, pw, ql, kl):
    u = u_ref[...]
    proj = _dot_t(u, w_ref[...])
    z_ref[...] = proj[:, :pw]
    q_ref[...] = _rms(proj[:, pw:pw + ql], gq_ref[...]).astype(BF16)
    ckv_ref[...] = _rms(proj[:, pw + ql:pw + ql + kl], gkv_ref[...])
    y = _dot_t(u, wkr_ref[...]) * cs_ref[...]
    kr_ref[...] = y + pltpu.roll(y, QK_ROPE, axis=1)


def _inproj(u, w_in_t, w_kr, g_q, g_kv, cs_tab, *, pw, ql, kl, tm=1024):
    n, d = u.shape
    period = cs_tab.shape[0] // tm
    row = lambda i: (i, 0)
    fixed = lambda i: (0, 0)
    return pl.pallas_call(
        functools.partial(_inproj_kernel, pw=pw, ql=ql, kl=kl),
        grid=(n // tm,),
        in_specs=[
            pl.BlockSpec((tm, d), row),
            pl.BlockSpec((pw + ql + kl, d), fixed),
            pl.BlockSpec(w_kr.shape, fixed),
            pl.BlockSpec((1, ql), fixed),
            pl.BlockSpec((1, kl), fixed),
            pl.BlockSpec((tm, 2 * QK_ROPE), lambda i: (i % period, 0)),
        ],
        out_specs=[
            pl.BlockSpec((tm, pw), row),
            pl.BlockSpec((tm, ql), row),
            pl.BlockSpec((tm, kl), row),
            pl.BlockSpec((tm, 2 * QK_ROPE), row),
        ],
        out_shape=[
            jax.ShapeDtypeStruct((n, pw), F32),
            jax.ShapeDtypeStruct((n, ql), BF16),
            jax.ShapeDtypeStruct((n, kl), F32),
            jax.ShapeDtypeStruct((n, 2 * QK_ROPE), F32),
        ],
        compiler_params=_params(("parallel",), 48),
        name="in_proj",
    )(u, w_in_t, w_kr, g_q, g_kv, cs_tab)


def _pool_rows(zext, pos0, w_ref, scale_ref, out_ref, out_row0, seg_rows, n_seg):
    p = zext.shape[1]
    gc = p // len(POOL_WINDOWS)
    ext = POOL_PAD + seg_rows
    pos = pos0 + lax.broadcasted_iota(jnp.int32, (seg_rows, 1), 0)
    for g, w in enumerate(POOL_WINDOWS):
        cols = slice(g * gc, (g + 1) * gc)
        s = zext[:, cols]
        win = s
        step = 1
        while step < w:
            win = win + pltpu.roll(win, step, axis=0)
            step *= 2
        cnt = jnp.minimum(pos + 1, w).astype(F32)
        for b in range(n_seg):
            lo = b * ext + POOL_PAD
            pooled = win[lo:lo + seg_rows] / cnt - s[lo:lo + seg_rows]
            mixed = _dot(pooled.astype(BF16), w_ref[g]) * scale_ref[:, cols]
            out_ref[pl.ds(out_row0 + b * seg_rows, seg_rows), cols] = mixed.astype(BF16)


def _pool_prompt_kernel(cur_ref, prev_ref, w_ref, scale_ref, out_ref, ext_ref, *, tp):
    t = pl.program_id(1)
    hist = jnp.where(t == 0, 0.0, prev_ref[0])
    ext_ref[0:POOL_PAD, :] = hist
    ext_ref[POOL_PAD:, :] = cur_ref[0]
    _pool_rows(ext_ref[...], t * tp, w_ref, scale_ref, out_ref.at[0], 0, tp, 1)


def _pool_prompt(z, w_pool, pool_scale, *, tp=512):
    b, t, p = z.shape
    hist_blocks = tp // POOL_PAD
    return pl.pallas_call(
        functools.partial(_pool_prompt_kernel, tp=tp),
        grid=(b, t // tp),
        in_specs=[
            pl.BlockSpec((1, tp, p), lambda i, j: (i, j, 0)),
            pl.BlockSpec((1, POOL_PAD, p), lambda i, j: (i, jnp.maximum(j * hist_blocks - 1, 0), 0)),
            pl.BlockSpec(w_pool.shape, lambda i, j: (0, 0, 0)),
            pl.BlockSpec((1, p), lambda i, j: (0, 0)),
        ],
        out_specs=pl.BlockSpec((1, tp, p), lambda i, j: (i, j, 0)),
        out_shape=jax.ShapeDtypeStruct((b, t, p), BF16),
        scratch_shapes=[pltpu.VMEM((POOL_PAD + tp, p), F32)],
        compiler_params=_params(("parallel", "arbitrary"), 32),
        name="pool_prompt",
    )(z, z, w_pool, pool_scale)


def _pool_sample_kernel(cur_ref, hist_ref, w_ref, scale_ref, out_ref, ext_ref, *, nb, ts, pos0):
    ext = POOL_PAD + ts
    for b in range(nb):
        ext_ref[b * ext:b * ext + POOL_PAD, :] = hist_ref[b]
        ext_ref[b * ext + POOL_PAD:(b + 1) * ext, :] = cur_ref[b]
    _pool_rows(ext_ref[...], pos0, w_ref, scale_ref, out_ref, 0, ts, nb)


def _pool_sample(z, hist, w_pool, pool_scale, *, pos0, nb=8):
    b, ts, p = z.shape
    return pl.pallas_call(
        functools.partial(_pool_sample_kernel, nb=nb, ts=ts, pos0=pos0),
        grid=(b // nb,),
        in_specs=[
            pl.BlockSpec((nb, ts, p), lambda i: (i, 0, 0)),
            pl.BlockSpec((nb, POOL_PAD, p), lambda i: (i, 0, 0)),
            pl.BlockSpec(w_pool.shape, lambda i: (0, 0, 0)),
            pl.BlockSpec((1, p), lambda i: (0, 0)),
        ],
        out_specs=pl.BlockSpec((nb * ts, p), lambda i: (i, 0)),
        out_shape=jax.ShapeDtypeStruct((b * ts, p), BF16),
        scratch_shapes=[pltpu.VMEM((nb * (POOL_PAD + ts), p), F32)],
        compiler_params=_params(("parallel",), 32),
        name="pool_sample",
    )(z, hist, w_pool, pool_scale)


def _qkv_prompt_kernel(ql_ref, ckv_ref, kr_ref, wq_ref, wk_ref, wv_ref, qt_ref, q_ref, k_ref, v_ref):
    ql = ql_ref[...]
    ckv = ckv_ref[...].astype(BF16)
    kr = kr_ref[...].astype(BF16)
    qt = qt_ref[...]
    for h in range(N_HEADS):
        q_ref[:, h * HEAD_W:(h + 1) * HEAD_W] = (_dot(ql, wq_ref[:, h * HEAD_W:(h + 1) * HEAD_W]) * qt).astype(BF16)
    kn = _dot(ckv, wk_ref[...])
    for h in range(N_HEADS):
        k_ref[:, h * HEAD_W:h * HEAD_W + QK_NOPE] = kn[:, h * QK_NOPE:(h + 1) * QK_NOPE].astype(BF16)
        k_ref[:, h * HEAD_W + QK_NOPE:(h + 1) * HEAD_W] = kr
    v_ref[...] = _dot(ckv, wv_ref[...]).astype(BF16)


def _qkv_prompt(q_lat, ckv, krblk, wq_cat, w_uk, w_uv, qt_tab, *, tm=512):
    n = q_lat.shape[0]
    period = qt_tab.shape[0] // tm
    row = lambda i: (i, 0)
    fixed = lambda i: (0, 0)
    return pl.pallas_call(
        _qkv_prompt_kernel,
        grid=(n // tm,),
        in_specs=[
            pl.BlockSpec((tm, q_lat.shape[1]), row),
            pl.BlockSpec((tm, ckv.shape[1]), row),
            pl.BlockSpec((tm, krblk.shape[1]), row),
            pl.BlockSpec(wq_cat.shape, fixed),
            pl.BlockSpec(w_uk.shape, fixed),
            pl.BlockSpec(w_uv.shape, fixed),
            pl.BlockSpec((tm, HEAD_W), lambda i: (i % period, 0)),
        ],
        out_specs=[
            pl.BlockSpec((tm, N_HEADS * HEAD_W), row),
            pl.BlockSpec((tm, N_HEADS * HEAD_W), row),
            pl.BlockSpec((tm, N_HEADS * V_HEAD), row),
        ],
        out_shape=[
            jax.ShapeDtypeStruct((n, N_HEADS * HEAD_W), BF16),
            jax.ShapeDtypeStruct((n, N_HEADS * HEAD_W), BF16),
            jax.ShapeDtypeStruct((n, N_HEADS * V_HEAD), BF16),
        ],
        compiler_params=_params(("parallel",), 48),
        name="qkv_prompt",
    )(q_lat, ckv, krblk, wq_cat, w_uk, w_uv, qt_tab)


def _softmax_strips(s_ref, p_ref, m_ref, mc_ref, a_ref, masked):
    blk = s_ref.shape[1]

    def strip(r0):
        if not masked:
            return s_ref[r0:r0 + STRIP, :], blk
        visible = ((r0 >> CHUNK_SHIFT) + 1) << CHUNK_SHIFT
        width = -(-visible // LANES) * LANES
        s = s_ref[r0:r0 + STRIP, :width]
        return jnp.where(lax.broadcasted_iota(jnp.int32, s.shape, 1) < visible, s, NEG_INF), width

    for r0 in range(0, blk, STRIP):
        s, _ = strip(r0)
        mc_ref[r0:r0 + STRIP, :] = jnp.broadcast_to(jnp.max(s, axis=-1, keepdims=True), (STRIP, LANES))
    m_old = m_ref[...]
    m_new = jnp.maximum(m_old, mc_ref[...])
    m_ref[...] = m_new
    a_ref[...] = jnp.exp2(m_old - m_new)
    for r0 in range(0, blk, STRIP):
        s, width = strip(r0)
        p_ref[r0:r0 + STRIP, :width] = jnp.exp2(s - jnp.tile(m_ref[r0:r0 + STRIP, :], (1, width // LANES))).astype(BF16)
        if width < blk:
            p_ref[r0:r0 + STRIP, width:] = jnp.zeros((STRIP, blk - width), BF16)


def _attn_prompt_kernel(q_ref, k_ref, v_ref, *rest, blk, hb, nq, n_cast):
    cast_src, o_ref, cast_dst = rest[:n_cast], rest[n_cast], rest[n_cast + 1:2 * n_cast + 1]
    s_ref, p_ref, m_ref, mc_ref, a_ref, acc_ref = rest[2 * n_cast + 1:]
    qi = pl.program_id(2)
    ones = jnp.ones((blk, V_HEAD), BF16)

    def scores(j, slot):
        for h in range(hb):
            q = q_ref[0, :, h * HEAD_W:(h + 1) * HEAD_W]
            s_ref[slot, h] = _dot_t(q, k_ref[0, j * blk:(j + 1) * blk, h * HEAD_W:(h + 1) * HEAD_W])

    def softmax_pv(j, slot, masked):
        for h in range(hb):
            _softmax_strips(s_ref.at[slot, h], p_ref.at[h], m_ref.at[h], mc_ref.at[h], a_ref.at[h], masked)
        for h in range(hb):
            v = v_ref[0, j * blk:(j + 1) * blk, h * V_HEAD:(h + 1) * V_HEAD]
            pv = _dot(p_ref[h], jnp.concatenate([v, ones], axis=1))
            acc_ref[h] = jnp.tile(a_ref[h], (1, 2)) * acc_ref[h] + pv

    for c in range(nq):
        @pl.when(qi == c)
        def _(c=c):
            m_ref[...] = jnp.full(m_ref.shape, NEG_INF, F32)
            acc_ref[...] = jnp.zeros_like(acc_ref)
            for src, dst in zip(cast_src, cast_dst):
                dst[...] = src[...].astype(BF16)
            scores(0, 0)
            for j in range(c + 1):
                if j < c:
                    scores(j + 1, (j + 1) % 2)
                softmax_pv(j, j % 2, masked=(j == c))
            for h in range(hb):
                acc = acc_ref[h]
                o_ref[0, :, h * V_HEAD:(h + 1) * V_HEAD] = (acc[:, :V_HEAD] * (1.0 / acc[:, V_HEAD:])).astype(BF16)


def _attn_prompt(q_cat, k_cat, v, *, cast=(), blk=512, hb=4):
    b, t, _ = q_cat.shape
    assert STRIP <= (1 << CHUNK_SHIFT) and blk % (1 << CHUNK_SHIFT) == 0
    ng, nq = N_HEADS // hb, t // blk
    cast_specs = _side_cast_specs(cast, b * ng, nq, lambda i, g, q: (i * ng + g, q))
    return pl.pallas_call(
        functools.partial(_attn_prompt_kernel, blk=blk, hb=hb, nq=nq, n_cast=len(cast)),
        grid=(b, ng, nq),
        in_specs=[
            pl.BlockSpec((1, blk, hb * HEAD_W), lambda i, g, q: (i, q, g)),
            pl.BlockSpec((1, t, hb * HEAD_W), lambda i, g, q: (i, 0, g)),
            pl.BlockSpec((1, t, hb * V_HEAD), lambda i, g, q: (i, 0, g)),
            *cast_specs,
        ],
        out_specs=[pl.BlockSpec((1, blk, hb * V_HEAD), lambda i, g, q: (i, q, g)), *cast_specs],
        out_shape=[
            jax.ShapeDtypeStruct((b, t, N_HEADS * V_HEAD), BF16),
            *[jax.ShapeDtypeStruct(w.shape, BF16) for w in cast],
        ],
        scratch_shapes=[
            pltpu.VMEM((2, hb, blk, blk), F32),
            pltpu.VMEM((hb, blk, blk), BF16),
            pltpu.VMEM((hb, blk, LANES), F32),
            pltpu.VMEM((hb, blk, LANES), F32),
            pltpu.VMEM((hb, blk, LANES), F32),
            pltpu.VMEM((hb, blk, 2 * V_HEAD), F32),
        ],
        compiler_params=_params(("parallel", "parallel", "arbitrary"), 48),
        name="attn_prompt",
    )(q_cat, k_cat, v, *cast)


def _q_sample_kernel(ql_ref, wq_ref, wk_ref, qt_ref, qa_ref, qr_ref, *, kl):
    ql = ql_ref[...]
    qt = qt_ref[...]
    for h in range(N_HEADS):
        q = _dot(ql, wq_ref[:, h * HEAD_W:(h + 1) * HEAD_W]) * qt
        qn = q[:, :QK_NOPE].astype(BF16)
        qa_ref[:, h * kl:(h + 1) * kl] = _dot_t(qn, wk_ref[:, h * QK_NOPE:(h + 1) * QK_NOPE]).astype(BF16)
        y = q[:, QK_NOPE:]
        qr_ref[:, h * 2 * QK_ROPE:(h + 1) * 2 * QK_ROPE] = (y + pltpu.roll(y, QK_ROPE, axis=1)).astype(BF16)


def _q_sample(q_lat, wq_cat, w_uk, qt_tab, *, tm=512):
    n, ql = q_lat.shape
    kl = w_uk.shape[0]
    row = lambda i: (i, 0)
    fixed = lambda i: (0, 0)
    return pl.pallas_call(
        functools.partial(_q_sample_kernel, kl=kl),
        grid=(n // tm,),
        in_specs=[
            pl.BlockSpec((tm, ql), row),
            pl.BlockSpec(wq_cat.shape, fixed),
            pl.BlockSpec(w_uk.shape, fixed),
            pl.BlockSpec((tm, HEAD_W), row),
        ],
        out_specs=[
            pl.BlockSpec((tm, N_HEADS * kl), row),
            pl.BlockSpec((tm, N_HEADS * 2 * QK_ROPE), row),
        ],
        out_shape=[
            jax.ShapeDtypeStruct((n, N_HEADS * kl), BF16),
            jax.ShapeDtypeStruct((n, N_HEADS * 2 * QK_ROPE), BF16),
        ],
        compiler_params=_params(("parallel",), 48),
        name="q_sample",
    )(q_lat, wq_cat, w_uk, qt_tab)


def _attn_sample_kernel(qa_ref, qr_ref, cc_ref, ck_ref, nc_ref, nk_ref, wv_ref, o_ref, *, nb, ts, past, kl):
    rows = N_HEADS * ts
    q_chunk = (past + lax.broadcasted_iota(jnp.int32, (rows, 1), 0) % ts) >> CHUNK_SHIFT
    vis_c = (lax.broadcasted_iota(jnp.int32, (1, past), 1) >> CHUNK_SHIFT) <= q_chunk
    vis_n = ((past + lax.broadcasted_iota(jnp.int32, (1, ts), 1)) >> CHUNK_SHIFT) <= q_chunk
    for b in range(nb):
        tok = slice(b * ts, (b + 1) * ts)
        qs = jnp.concatenate([qa_ref[tok, h * kl:(h + 1) * kl] for h in range(N_HEADS)], axis=0)
        qr = jnp.concatenate(
            [qr_ref[tok, h * 2 * QK_ROPE:h * 2 * QK_ROPE + QK_ROPE] for h in range(N_HEADS)], axis=0)
        kc = cc_ref[b].astype(BF16)
        krc = ck_ref[b].astype(BF16)
        kn = nc_ref[tok, :].astype(BF16)
        krn = nk_ref[tok, :QK_ROPE].astype(BF16)
        s_c = jnp.where(vis_c, _dot_t(qs, kc) + _dot_t(qr, krc), NEG_INF)
        s_n = jnp.where(vis_n, _dot_t(qs, kn) + _dot_t(qr, krn), NEG_INF)
        m = jnp.maximum(jnp.max(s_c, axis=-1, keepdims=True), jnp.max(s_n, axis=-1, keepdims=True))
        p_c = jnp.exp2(s_c - m)
        p_n = jnp.exp2(s_n - m)
        l = jnp.sum(p_c, axis=-1, keepdims=True) + jnp.sum(p_n, axis=-1, keepdims=True)
        o_lat = ((_dot(p_c.astype(BF16), kc) + _dot(p_n.astype(BF16), kn)) * (1.0 / l)).astype(BF16)
        for h in range(N_HEADS):
            o_ref[tok, h * V_HEAD:(h + 1) * V_HEAD] = _dot(
                o_lat[h * ts:(h + 1) * ts], wv_ref[:, h * V_HEAD:(h + 1) * V_HEAD]).astype(BF16)


def _attn_sample(q_abs, q_rope, cache_ckv, cache_kr, ckv_new, kr_new, w_uv, *, ts, nb=2):
    b, past, kl = cache_ckv.shape
    row = lambda i: (i, 0)
    return pl.pallas_call(
        functools.partial(_attn_sample_kernel, nb=nb, ts=ts, past=past, kl=kl),
        grid=(b // nb,),
        in_specs=[
            pl.BlockSpec((nb * ts, q_abs.shape[1]), row),
            pl.BlockSpec((nb * ts, q_rope.shape[1]), row),
            pl.BlockSpec((nb, past, kl), lambda i: (i, 0, 0)),
            pl.BlockSpec((nb, past, cache_kr.shape[2]), lambda i: (i, 0, 0)),
            pl.BlockSpec((nb * ts, kl), row),
            pl.BlockSpec((nb * ts, kr_new.shape[1]), row),
            pl.BlockSpec(w_uv.shape, lambda i: (0, 0)),
        ],
        out_specs=pl.BlockSpec((nb * ts, N_HEADS * V_HEAD), row),
        out_shape=jax.ShapeDtypeStruct((b * ts, N_HEADS * V_HEAD), BF16),
        compiler_params=_params(("parallel",), 48),
        name="attn_sample",
    )(q_abs, q_rope, cache_ckv, cache_kr, ckv_new, kr_new, w_uv)


def _mix_out_kernel(u_ref, pa_ref, ob_ref, h_ref, wga_ref, wgb_ref, wpo_ref, woa_ref, wout_ref, acc_ref):
    j = pl.program_id(1)
    u = u_ref[...]
    gate_a = jax.nn.sigmoid(_dot_t(u, wga_ref[...]))
    gate_b = jax.nn.sigmoid(_dot_t(u, wgb_ref[...]))
    a = _dot(pa_ref[...], wpo_ref[...])
    b = _dot(ob_ref[...], woa_ref[...])
    merged = (gate_a * a + gate_b * b).astype(BF16)

    @pl.when(j == 0)
    def _():
        acc_ref[...] = h_ref[...]

    acc_ref[...] += _dot(merged, wout_ref[...])


def _mix_out(u, pooled, o_attn, h, w_in_t, gate_row0, w_pool_out, w_o_attn, w_out, *, tm=512, tc=512):
    n, d = h.shape
    nc = d // tc
    row = lambda i, j: (i, 0)
    col = lambda i, j: (0, j)
    return pl.pallas_call(
        _mix_out_kernel,
        grid=(n // tm, nc),
        in_specs=[
            pl.BlockSpec((tm, d), row),
            pl.BlockSpec((tm, pooled.shape[1]), row),
            pl.BlockSpec((tm, o_attn.shape[1]), row),
            pl.BlockSpec((tm, d), row),
            pl.BlockSpec((pl.Element(tc), pl.Element(d)),
                         lambda i, j: (pl.multiple_of(gate_row0 + j * tc, BF16_ROWS), 0)),
            pl.BlockSpec((pl.Element(tc), pl.Element(d)),
                         lambda i, j: (pl.multiple_of(gate_row0 + d + j * tc, BF16_ROWS), 0)),
            pl.BlockSpec((w_pool_out.shape[0], tc), col),
            pl.BlockSpec((w_o_attn.shape[0], tc), col),
            pl.BlockSpec((tc, d), lambda i, j: (j, 0)),
        ],
        out_specs=pl.BlockSpec((tm, d), row),
        out_shape=jax.ShapeDtypeStruct((n, d), F32),
        compiler_params=_params(("parallel", "arbitrary"), 56),
        name="mix_out",
    )(u, pooled, o_attn, h, w_in_t, w_in_t, w_pool_out, w_o_attn, w_out)


def _rope_tables(pos):
    half = QK_ROPE // 2
    inv = ROPE_THETA ** (-jnp.arange(half, dtype=F32) * 2.0 / QK_ROPE)
    ang = pos.astype(F32)[:, None] * inv[None, :]
    c, s = jnp.cos(ang), jnp.sin(ang)
    cs = jnp.concatenate([c, c, -s, s], axis=-1)
    qt = (SM_SCALE * LOG2_E) * jnp.concatenate([jnp.ones((pos.shape[0], QK_NOPE), F32), cs], axis=-1)
    return cs, qt


def _dup_rope_cols(w):
    half = QK_ROPE // 2
    x1, x2 = w[..., :half], w[..., half:]
    return jnp.concatenate([x1, x2, x2, x1], axis=-1)


def kernel(x_prompt, x_sample, cache_ckv, cache_krope, state_pool, g_ffn1, w1_gate, w1_up, w1_down, g_mix, w_in, g_q_lat, g_kv_lat, w_uq, w_uk, w_uv, w_o_attn, w_pool, pool_scale, w_pool_out, w_out, g_ffn2, w2_gate, w2_up, w2_down, g_final):
    bp, tp, d = x_prompt.shape
    bs, ts, _ = x_sample.shape
    assert g_ffn1.shape[0] == 1, "single-layer stack only"
    assert ts >= POOL_HIST and tp >= POOL_HIST
    past = cache_ckv.shape[2]
    pw = pool_scale.shape[1]
    ql = g_q_lat.shape[1]
    kl = g_kv_lat.shape[1]
    o3 = pw + ql + kl
    o4 = o3 + QK_ROPE

    cs_p, qt_p = _rope_tables(jnp.arange(tp, dtype=jnp.int32))
    cs_s, qt_s = _rope_tables(past + jnp.arange(ts, dtype=jnp.int32))
    cs_s = jnp.tile(cs_s, (bs, 1))
    qt_s = jnp.tile(qt_s, (bs, 1))

    bf = lambda w: w[0].astype(BF16)
    vec = lambda g: g[0].reshape(1, -1)
    w_in_t = w_in[0].T.astype(BF16)
    w_kr = _dup_rope_cols(w_in_t[o3:o4].T).T
    wq_cat = jnp.concatenate(
        [w_uq[0][..., :QK_NOPE], _dup_rope_cols(w_uq[0][..., QK_NOPE:])], axis=-1
    ).reshape(ql, N_HEADS * HEAD_W).astype(BF16)
    wk = w_uk[0].reshape(kl, N_HEADS * QK_NOPE).astype(BF16)
    wv = w_uv[0].reshape(kl, N_HEADS * V_HEAD).astype(BF16)
    wpool = bf(w_pool)
    hist_s = jnp.pad(state_pool[0], ((0, 0), (POOL_PAD - POOL_HIST, 0), (0, 0)))
    g_last = g_final.reshape(1, -1)

    def in_proj(u, cs_tab):
        return _inproj(u, w_in_t, w_kr, vec(g_q_lat), vec(g_kv_lat), cs_tab, pw=pw, ql=ql, kl=kl)


    h1_s, u_s, w1g, w1u, w1d = _ffn(
        x_sample.reshape(bs * ts, d), vec(g_ffn1), w1_gate[0], w1_up[0], w1_down[0], vec(g_mix),
        emit_normed=True, cast_weights=True)
    z, q_lat, ckv_s, kr_s = in_proj(u_s, cs_s)
    z_s = z.reshape(bs, ts, pw)
    pooled_s = _pool_sample(z_s, hist_s, wpool, vec(pool_scale), pos0=past)
    q_abs_s, q_rope_s = _q_sample(q_lat, wq_cat, wk, qt_s)

    h1, u = _ffn(x_prompt.reshape(bp * tp, d), vec(g_ffn1), w1g, w1u, w1d, vec(g_mix), emit_normed=True)
    z, q_lat, ckv_p, kr_p = in_proj(u, cs_p)
    z_p = z.reshape(bp, tp, pw)
    pooled = _pool_prompt(z_p, wpool, vec(pool_scale)).reshape(bp * tp, pw)
    q_cat, k_cat, v = _qkv_prompt(q_lat, ckv_p, kr_p, wq_cat, wk, wv, qt_p)
    o, w2g, w2u, w2d, wpo, woa, wout = _attn_prompt(
        q_cat.reshape(bp, tp, -1), k_cat.reshape(bp, tp, -1), v.reshape(bp, tp, -1),
        cast=(w2_gate[0], w2_up[0], w2_down[0], w_pool_out[0], w_o_attn[0], w_out[0]))
    h2 = _mix_out(u, pooled, o.reshape(bp * tp, -1), h1, w_in_t, o4, wpo, woa, wout)
    (y_p,) = _ffn(h2, vec(g_ffn2), w2g, w2u, w2d, g_last, emit_normed=False)

    o_s = _attn_sample(q_abs_s, q_rope_s, cache_ckv[0], cache_krope[0], ckv_s, kr_s, wv, ts=ts)
    h2 = _mix_out(u_s, pooled_s, o_s, h1_s, w_in_t, o4, wpo, woa, wout)
    (y_s,) = _ffn(h2, vec(g_ffn2), w2g, w2u, w2d, g_last, emit_normed=False, tf=512)

    return (
        y_p.reshape(bp, tp, d),
        y_s.reshape(bs, ts, d),
        ckv_p.reshape(1, bp, tp, kl),
        kr_p[:, :QK_ROPE].reshape(1, bp, tp, QK_ROPE),
        z_p[None, :, tp - POOL_HIST:],
        ckv_s.reshape(1, bs, ts, kl),
        kr_s[:, :QK_ROPE].reshape(1, bs, ts, QK_ROPE),
        z_s[None, :, ts - POOL_HIST:],
    )
```

```python
import functools

import jax
import jax.numpy as jnp
from jax import lax
from jax.experimental import pallas as pl
from jax.experimental.pallas import tpu as pltpu

F32 = jnp.float32
BF16 = jnp.bfloat16

CHUNK_SHIFT = 6
N_HEADS = 16
QK_NOPE = 128
QK_ROPE = 64
V_HEAD = 128
HEAD_W = 256
LANES = 128
BF16_ROWS = 16
STRIP = 32
POOL_WINDOWS = (2, 4, 8, 16)
POOL_HIST = 15
POOL_PAD = 16
ROPE_THETA = 10000.0
EPS = 1e-6
SM_SCALE = (QK_NOPE + QK_ROPE) ** -0.5
LOG2_E = 1.4426950408889634
NEG_INF = -1e30
MIB = 1024 * 1024


def _dot(a, b):
    return jnp.dot(a, b, preferred_element_type=F32)


def _dot_t(a, b):
    return lax.dot_general(a, b, (((1,), (1,)), ((), ())), preferred_element_type=F32)


def _rms(x, g):
    return x * lax.rsqrt(jnp.mean(x * x, axis=-1, keepdims=True) + EPS) * g


def _params(semantics, vmem_mib):
    return pltpu.CompilerParams(dimension_semantics=semantics, vmem_limit_bytes=vmem_mib * MIB)


def _ffn_kernel(x_ref, g_ref, wg_ref, wu_ref, wd_ref, gn_ref, acc_ref, *rest, emit_normed, cast_weights):
    rest = list(rest)
    u_ref = rest.pop(0) if emit_normed else None
    xn_ref = rest.pop()
    w_dst = rest
    j = pl.program_id(1)

    @pl.when(j == 0)
    def _():
        xn_ref[...] = _rms(x_ref[...], g_ref[...]).astype(BF16)
        acc_ref[...] = jnp.zeros_like(acc_ref)

    wg, wu, wd = wg_ref[...], wu_ref[...], wd_ref[...]
    if cast_weights:
        wg, wu, wd = wg.astype(BF16), wu.astype(BF16), wd.astype(BF16)
        for dst, w in zip(w_dst, (wg, wu, wd)):
            dst[...] = w
    xn = xn_ref[...]
    gate = _dot(xn, wg)
    up = _dot(xn, wu)
    act = (gate * jax.nn.sigmoid(gate) * up).astype(BF16)
    acc_ref[...] += _dot(act, wd)

    @pl.when(j == pl.num_programs(1) - 1)
    def _():
        h = x_ref[...] + 0.5 * acc_ref[...]
        if emit_normed:
            acc_ref[...] = h
            u_ref[...] = _rms(h, gn_ref[...]).astype(BF16)
        else:
            acc_ref[...] = _rms(h, gn_ref[...])


def _side_cast_specs(mats, n0, n1, to01):
    specs = []
    for w in mats:
        rows, cols = w.shape
        if rows % (n0 * BF16_ROWS) == 0 and cols % (n1 * LANES) == 0:
            specs.append(pl.BlockSpec((rows // n0, cols // n1), lambda *g: to01(*g)))
        else:
            assert rows % (n1 * BF16_ROWS) == 0 and cols % (n0 * LANES) == 0, w.shape
            specs.append(pl.BlockSpec((rows // n1, cols // n0), lambda *g: to01(*g)[::-1]))
    return specs


def _ffn(x, g, wg, wu, wd, g_next, *, emit_normed, cast_weights=False, tm=1024, tf=256):
    n, d = x.shape
    f = wg.shape[1]
    row = lambda i, j: (i, 0)
    assert not cast_weights or n == tm
    rows_mode = dict(pipeline_mode=pl.Buffered(1)) if n == tm else {}
    w_up_spec = pl.BlockSpec((d, tf), lambda i, j: (0, j))
    w_down_spec = pl.BlockSpec((tf, d), lambda i, j: (j, 0))
    out_shape = [jax.ShapeDtypeStruct((n, d), F32)]
    out_specs = [pl.BlockSpec((tm, d), row, **rows_mode)]
    if emit_normed:
        out_shape.append(jax.ShapeDtypeStruct((n, d), BF16))
        out_specs.append(pl.BlockSpec((tm, d), row, **rows_mode))
    if cast_weights:
        out_shape += [jax.ShapeDtypeStruct(w.shape, BF16) for w in (wg, wu, wd)]
        out_specs += [w_up_spec, w_up_spec, w_down_spec]
    return pl.pallas_call(
        functools.partial(_ffn_kernel, emit_normed=emit_normed, cast_weights=cast_weights),
        grid=(n // tm, f // tf),
        in_specs=[
            pl.BlockSpec((tm, d), row, **rows_mode),
            pl.BlockSpec((1, d), lambda i, j: (0, 0)),
            w_up_spec,
            w_up_spec,
            w_down_spec,
            pl.BlockSpec((1, d), lambda i, j: (0, 0)),
        ],
        out_specs=out_specs,
        out_shape=out_shape,
        scratch_shapes=[pltpu.VMEM((tm, d), BF16)],
        compiler_params=_params(("parallel", "arbitrary"), 62),
        name="ffn_norm" if emit_normed else "ffn_final",
    )(x, g, wg, wu, wd, g_next)


def _inproj_kernel(u_ref, w_ref, wkr_ref, gq_ref, gkv_ref, cs_ref, z_ref, q_ref, ckv_ref, kr_ref, *, pw, ql, kl):
    u = u_ref[...]
    proj = _dot_t(u, w_ref[...])
    z_ref[...] = proj[:, :pw]
    q_ref[...] = _rms(proj[:, pw:pw + ql], gq_ref[...]).astype(BF16)
    ckv_ref[...] = _rms(proj[:, pw + ql:pw + ql + kl], gkv_ref[...])
    y = _dot_t(u, wkr_ref[...]) * cs_ref[...]
    kr_ref[...] = y + pltpu.roll(y, QK_ROPE, axis=1)


def _inproj(u, w_in_t, w_kr, g_q, g_kv, cs_tab, *, pw, ql, kl, tm=1024):
    n, d = u.shape
    period = cs_tab.shape[0] // tm
    row = lambda i: (i, 0)
    fixed = lambda i: (0, 0)
    return pl.pallas_call(
        functools.partial(_inproj_kernel, pw=pw, ql=ql, kl=kl),
        grid=(n // tm,),
        in_specs=[
            pl.BlockSpec((tm, d), row),
            pl.BlockSpec((pw + ql + kl, d), fixed),
            pl.BlockSpec(w_kr.shape, fixed),
            pl.BlockSpec((1, ql), fixed),
            pl.BlockSpec((1, kl), fixed),
            pl.BlockSpec((tm, 2 * QK_ROPE), lambda i: (i % period, 0)),
        ],
        out_specs=[
            pl.BlockSpec((tm, pw), row),
            pl.BlockSpec((tm, ql), row),
            pl.BlockSpec((tm, kl), row),
            pl.BlockSpec((tm, 2 * QK_ROPE), row),
        ],
        out_shape=[
            jax.ShapeDtypeStruct((n, pw), F32),
            jax.ShapeDtypeStruct((n, ql), BF16),
            jax.ShapeDtypeStruct((n, kl), F32),
            jax.ShapeDtypeStruct((n, 2 * QK_ROPE), F32),
        ],
        compiler_params=_params(("parallel",), 48),
        name="in_proj",
    )(u, w_in_t, w_kr, g_q, g_kv, cs_tab)


def _pool_rows(zext, pos0, w_ref, scale_ref, out_ref, out_row0, seg_rows, n_seg):
    p = zext.shape[1]
    gc = p // len(POOL_WINDOWS)
    ext = POOL_PAD + seg_rows
    pos = pos0 + lax.broadcasted_iota(jnp.int32, (seg_rows, 1), 0)
    for g, w in enumerate(POOL_WINDOWS):
        cols = slice(g * gc, (g + 1) * gc)
        s = zext[:, cols]
        win = s
        step = 1
        while step < w:
            win = win + pltpu.roll(win, step, axis=0)
            step *= 2
        cnt = jnp.minimum(pos + 1, w).astype(F32)
        for b in range(n_seg):
            lo = b * ext + POOL_PAD
            pooled = win[lo:lo + seg_rows] / cnt - s[lo:lo + seg_rows]
            mixed = _dot(pooled.astype(BF16), w_ref[g]) * scale_ref[:, cols]
            out_ref[pl.ds(out_row0 + b * seg_rows, seg_rows), cols] = mixed.astype(BF16)


def _pool_prompt_kernel(cur_ref, prev_ref, w_ref, scale_ref, out_ref, ext_ref, *, tp):
    t = pl.program_id(1)
    hist = jnp.where(t == 0, 0.0, prev_ref[0])
    ext_ref[0:POOL_PAD, :] = hist
    ext_ref[POOL_PAD:, :] = cur_ref[0]
    _pool_rows(ext_ref[...], t * tp, w_ref, scale_ref, out_ref.at[0], 0, tp, 1)


def _pool_prompt(z, w_pool, pool_scale, *, tp=512):
    b, t, p = z.shape
    hist_blocks = tp // POOL_PAD
    return pl.pallas_call(
        functools.partial(_pool_prompt_kernel, tp=tp),
        grid=(b, t // tp),
        in_specs=[
            pl.BlockSpec((1, tp, p), lambda i, j: (i, j, 0)),
            pl.BlockSpec((1, POOL_PAD, p), lambda i, j: (i, jnp.maximum(j * hist_blocks - 1, 0), 0)),
            pl.BlockSpec(w_pool.shape, lambda i, j: (0, 0, 0)),
            pl.BlockSpec((1, p), lambda i, j: (0, 0)),
        ],
        out_specs=pl.BlockSpec((1, tp, p), lambda i, j: (i, j, 0)),
        out_shape=jax.ShapeDtypeStruct((b, t, p), BF16),
        scratch_shapes=[pltpu.VMEM((POOL_PAD + tp, p), F32)],
        compiler_params=_params(("parallel", "arbitrary"), 32),
        name="pool_prompt",
    )(z, z, w_pool, pool_scale)


def _pool_sample_kernel(cur_ref, hist_ref, w_ref, scale_ref, out_ref, ext_ref, *, nb, ts, pos0):
    ext = POOL_PAD + ts
    for b in range(nb):
        ext_ref[b * ext:b * ext + POOL_PAD, :] = hist_ref[b]
        ext_ref[b * ext + POOL_PAD:(b + 1) * ext, :] = cur_ref[b]
    _pool_rows(ext_ref[...], pos0, w_ref, scale_ref, out_ref, 0, ts, nb)


def _pool_sample(z, hist, w_pool, pool_scale, *, pos0, nb=8):
    b, ts, p = z.shape
    return pl.pallas_call(
        functools.partial(_pool_sample_kernel, nb=nb, ts=ts, pos0=pos0),
        grid=(b // nb,),
        in_specs=[
            pl.BlockSpec((nb, ts, p), lambda i: (i, 0, 0)),
            pl.BlockSpec((nb, POOL_PAD, p), lambda i: (i, 0, 0)),
            pl.BlockSpec(w_pool.shape, lambda i: (0, 0, 0)),
            pl.BlockSpec((1, p), lambda i: (0, 0)),
        ],
        out_specs=pl.BlockSpec((nb * ts, p), lambda i: (i, 0)),
        out_shape=jax.ShapeDtypeStruct((b * ts, p), BF16),
        scratch_shapes=[pltpu.VMEM((nb * (POOL_PAD + ts), p), F32)],
        compiler_params=_params(("parallel",), 32),
        name="pool_sample",
    )(z, hist, w_pool, pool_scale)


def _qkv_prompt_kernel(ql_ref, ckv_ref, kr_ref, wq_ref, wk_ref, wv_ref, qt_ref, q_ref, k_ref, v_ref):
    ql = ql_ref[...]
    ckv = ckv_ref[...].astype(BF16)
    kr = kr_ref[...].astype(BF16)
    qt = qt_ref[...]
    for h in range(N_HEADS):
        q_ref[:, h * HEAD_W:(h + 1) * HEAD_W] = (_dot(ql, wq_ref[:, h * HEAD_W:(h + 1) * HEAD_W]) * qt).astype(BF16)
    kn = _dot(ckv, wk_ref[...])
    for h in range(N_HEADS):
        k_ref[:, h * HEAD_W:h * HEAD_W + QK_NOPE] = kn[:, h * QK_NOPE:(h + 1) * QK_NOPE].astype(BF16)
        k_ref[:, h * HEAD_W + QK_NOPE:(h + 1) * HEAD_W] = kr
    v_ref[...] = _dot(ckv, wv_ref[...]).astype(BF16)


def _qkv_prompt(q_lat, ckv, krblk, wq_cat, w_uk, w_uv, qt_tab, *, tm=512):
    n = q_lat.shape[0]
    period = qt_tab.shape[0] // tm
    row = lambda i: (i, 0)
    fixed = lambda i: (0, 0)
    return pl.pallas_call(
        _qkv_prompt_kernel,
        grid=(n // tm,),
        in_specs=[
            pl.BlockSpec((tm, q_lat.shape[1]), row),
            pl.BlockSpec((tm, ckv.shape[1]), row),
            pl.BlockSpec((tm, krblk.shape[1]), row),
            pl.BlockSpec(wq_cat.shape, fixed),
            pl.BlockSpec(w_uk.shape, fixed),
            pl.BlockSpec(w_uv.shape, fixed),
            pl.BlockSpec((tm, HEAD_W), lambda i: (i % period, 0)),
        ],
        out_specs=[
            pl.BlockSpec((tm, N_HEADS * HEAD_W), row),
            pl.BlockSpec((tm, N_HEADS * HEAD_W), row),
            pl.BlockSpec((tm, N_HEADS * V_HEAD), row),
        ],
        out_shape=[
            jax.ShapeDtypeStruct((n, N_HEADS * HEAD_W), BF16),
            jax.ShapeDtypeStruct((n, N_HEADS * HEAD_W), BF16),
            jax.ShapeDtypeStruct((n, N_HEADS * V_HEAD), BF16),
        ],
        compiler_params=_params(("parallel",), 48),
        name="qkv_prompt",
    )(q_lat, ckv, krblk, wq_cat, w_uk, w_uv, qt_tab)


def _softmax_strips(s_ref, p_ref, m_ref, mc_ref, a_ref, masked):
    blk = s_ref.shape[1]

    def strip(r0):
        if not masked:
            return s_ref[r0:r0 + STRIP, :], blk
        visible = ((r0 >> CHUNK_SHIFT) + 1) << CHUNK_SHIFT
        width = -(-visible // LANES) * LANES
        s = s_ref[r0:r0 + STRIP, :width]
        return jnp.where(lax.broadcasted_iota(jnp.int32, s.shape, 1) < visible, s, NEG_INF), width

    for r0 in range(0, blk, STRIP):
        s, _ = strip(r0)
        mc_ref[r0:r0 + STRIP, :] = jnp.broadcast_to(jnp.max(s, axis=-1, keepdims=True), (STRIP, LANES))
    m_old = m_ref[...]
    m_new = jnp.maximum(m_old, mc_ref[...])
    m_ref[...] = m_new
    a_ref[...] = jnp.exp2(m_old - m_new)
    for r0 in range(0, blk, STRIP):
        s, width = strip(r0)
        p_ref[r0:r0 + STRIP, :width] = jnp.exp2(s - jnp.tile(m_ref[r0:r0 + STRIP, :], (1, width // LANES))).astype(BF16)
        if width < blk:
            p_ref[r0:r0 + STRIP, width:] = jnp.zeros((STRIP, blk - width), BF16)


def _attn_prompt_kernel(q_ref, k_ref, v_ref, *rest, blk, hb, n_cast):
    cast_src, o_ref, cast_dst = rest[:n_cast], rest[n_cast], rest[n_cast + 1:2 * n_cast + 1]
    s_ref, p_ref, m_ref, mc_ref, a_ref, acc_ref = rest[2 * n_cast + 1:]
    qi = pl.program_id(2)
    ones = jnp.ones((blk, V_HEAD), BF16)
    m_ref[...] = jnp.full(m_ref.shape, NEG_INF, F32)
    acc_ref[...] = jnp.zeros_like(acc_ref)

    def block(off, masked):
        for h in range(hb):
            q = q_ref[0, :, h * HEAD_W:(h + 1) * HEAD_W]
            s_ref[h] = _dot_t(q, k_ref[0, pl.ds(off, blk), h * HEAD_W:(h + 1) * HEAD_W])
        for h in range(hb):
            _softmax_strips(s_ref.at[h], p_ref.at[h], m_ref.at[h], mc_ref.at[h], a_ref.at[h], masked)
        for h in range(hb):
            v = v_ref[0, pl.ds(off, blk), h * V_HEAD:(h + 1) * V_HEAD]
            pv = _dot(p_ref[h], jnp.concatenate([v, ones], axis=1))
            acc_ref[h] = jnp.tile(a_ref[h], (1, 2)) * acc_ref[h] + pv

    def below_diagonal(j, carry):
        block(pl.multiple_of(j * blk, blk), masked=False)
        return carry

    lax.fori_loop(0, qi, below_diagonal, 0)
    for src, dst in zip(cast_src, cast_dst):
        dst[...] = src[...].astype(BF16)
    block(pl.multiple_of(qi * blk, blk), masked=True)
    for h in range(hb):
        acc = acc_ref[h]
        o_ref[0, :, h * V_HEAD:(h + 1) * V_HEAD] = (acc[:, :V_HEAD] * (1.0 / acc[:, V_HEAD:])).astype(BF16)


def _attn_prompt(q_cat, k_cat, v, *, cast=(), blk=512, hb=4):
    b, t, _ = q_cat.shape
    assert STRIP <= (1 << CHUNK_SHIFT) and blk % (1 << CHUNK_SHIFT) == 0
    ng, nq = N_HEADS // hb, t // blk
    cast_specs = _side_cast_specs(cast, b * ng, nq, lambda i, g, q: (i * ng + g, q))
    return pl.pallas_call(
        functools.partial(_attn_prompt_kernel, blk=blk, hb=hb, n_cast=len(cast)),
        grid=(b, ng, nq),
        in_specs=[
            pl.BlockSpec((1, blk, hb * HEAD_W), lambda i, g, q: (i, q, g)),
            pl.BlockSpec((1, t, hb * HEAD_W), lambda i, g, q: (i, 0, g)),
            pl.BlockSpec((1, t, hb * V_HEAD), lambda i, g, q: (i, 0, g)),
            *cast_specs,
        ],
        out_specs=[pl.BlockSpec((1, blk, hb * V_HEAD), lambda i, g, q: (i, q, g)), *cast_specs],
        out_shape=[
            jax.ShapeDtypeStruct((b, t, N_HEADS * V_HEAD), BF16),
            *[jax.ShapeDtypeStruct(w.shape, BF16) for w in cast],
        ],
        scratch_shapes=[
            pltpu.VMEM((hb, blk, blk), F32),
            pltpu.VMEM((hb, blk, blk), BF16),
            pltpu.VMEM((hb, blk, LANES), F32),
            pltpu.VMEM((hb, blk, LANES), F32),
            pltpu.VMEM((hb, blk, LANES), F32),
            pltpu.VMEM((hb, blk, 2 * V_HEAD), F32),
        ],
        compiler_params=_params(("parallel", "parallel", "arbitrary"), 48),
        name="attn_prompt",
    )(q_cat, k_cat, v, *cast)


def _q_sample_kernel(ql_ref, wq_ref, wk_ref, qt_ref, qa_ref, qr_ref, *, kl):
    ql = ql_ref[...]
    qt = qt_ref[...]
    for h in range(N_HEADS):
        q = _dot(ql, wq_ref[:, h * HEAD_W:(h + 1) * HEAD_W]) * qt
        qn = q[:, :QK_NOPE].astype(BF16)
        qa_ref[:, h * kl:(h + 1) * kl] = _dot_t(qn, wk_ref[:, h * QK_NOPE:(h + 1) * QK_NOPE]).astype(BF16)
        y = q[:, QK_NOPE:]
        qr_ref[:, h * 2 * QK_ROPE:(h + 1) * 2 * QK_ROPE] = (y + pltpu.roll(y, QK_ROPE, axis=1)).astype(BF16)


def _q_sample(q_lat, wq_cat, w_uk, qt_tab, *, tm=512):
    n, ql = q_lat.shape
    kl = w_uk.shape[0]
    row = lambda i: (i, 0)
    fixed = lambda i: (0, 0)
    return pl.pallas_call(
        functools.partial(_q_sample_kernel, kl=kl),
        grid=(n // tm,),
        in_specs=[
            pl.BlockSpec((tm, ql), row),
            pl.BlockSpec(wq_cat.shape, fixed),
            pl.BlockSpec(w_uk.shape, fixed),
            pl.BlockSpec((tm, HEAD_W), row),
        ],
        out_specs=[
            pl.BlockSpec((tm, N_HEADS * kl), row),
            pl.BlockSpec((tm, N_HEADS * 2 * QK_ROPE), row),
        ],
        out_shape=[
            jax.ShapeDtypeStruct((n, N_HEADS * kl), BF16),
            jax.ShapeDtypeStruct((n, N_HEADS * 2 * QK_ROPE), BF16),
        ],
        compiler_params=_params(("parallel",), 48),
        name="q_sample",
    )(q_lat, wq_cat, w_uk, qt_tab)


def _attn_sample_kernel(qa_ref, qr_ref, cc_ref, ck_ref, nc_ref, nk_ref, wv_ref, o_ref, *, nb, ts, past, kl):
    rows = N_HEADS * ts
    q_chunk = (past + lax.broadcasted_iota(jnp.int32, (rows, 1), 0) % ts) >> CHUNK_SHIFT
    vis_c = (lax.broadcasted_iota(jnp.int32, (1, past), 1) >> CHUNK_SHIFT) <= q_chunk
    vis_n = ((past + lax.broadcasted_iota(jnp.int32, (1, ts), 1)) >> CHUNK_SHIFT) <= q_chunk
    for b in range(nb):
        tok = slice(b * ts, (b + 1) * ts)
        qs = jnp.concatenate([qa_ref[tok, h * kl:(h + 1) * kl] for h in range(N_HEADS)], axis=0)
        qr = jnp.concatenate(
            [qr_ref[tok, h * 2 * QK_ROPE:h * 2 * QK_ROPE + QK_ROPE] for h in range(N_HEADS)], axis=0)
        kc = cc_ref[b].astype(BF16)
        krc_t = ck_ref[b].astype(BF16)
        kn = nc_ref[tok, :].astype(BF16)
        krn = nk_ref[tok, :QK_ROPE].astype(BF16)
        s_c = jnp.where(vis_c, _dot_t(qs, kc) + _dot(qr, krc_t), NEG_INF)
        s_n = jnp.where(vis_n, _dot_t(qs, kn) + _dot_t(qr, krn), NEG_INF)
        m = jnp.maximum(jnp.max(s_c, axis=-1, keepdims=True), jnp.max(s_n, axis=-1, keepdims=True))
        p_c = jnp.exp2(s_c - m)
        p_n = jnp.exp2(s_n - m)
        l = jnp.sum(p_c, axis=-1, keepdims=True) + jnp.sum(p_n, axis=-1, keepdims=True)
        o_lat = ((_dot(p_c.astype(BF16), kc) + _dot(p_n.astype(BF16), kn)) * (1.0 / l)).astype(BF16)
        for h in range(N_HEADS):
            o_ref[tok, h * V_HEAD:(h + 1) * V_HEAD] = _dot(
                o_lat[h * ts:(h + 1) * ts], wv_ref[:, h * V_HEAD:(h + 1) * V_HEAD]).astype(BF16)


def _attn_sample(q_abs, q_rope, cache_ckv, cache_kr_t, ckv_new, kr_new, w_uv, *, ts, nb=2):
    b, past, kl = cache_ckv.shape
    row = lambda i: (i, 0)
    return pl.pallas_call(
        functools.partial(_attn_sample_kernel, nb=nb, ts=ts, past=past, kl=kl),
        grid=(b // nb,),
        in_specs=[
            pl.BlockSpec((nb * ts, q_abs.shape[1]), row),
            pl.BlockSpec((nb * ts, q_rope.shape[1]), row),
            pl.BlockSpec((nb, past, kl), lambda i: (i, 0, 0)),
            pl.BlockSpec((nb, cache_kr_t.shape[1], past), lambda i: (i, 0, 0)),
            pl.BlockSpec((nb * ts, kl), row),
            pl.BlockSpec((nb * ts, kr_new.shape[1]), row),
            pl.BlockSpec(w_uv.shape, lambda i: (0, 0)),
        ],
        out_specs=pl.BlockSpec((nb * ts, N_HEADS * V_HEAD), row),
        out_shape=jax.ShapeDtypeStruct((b * ts, N_HEADS * V_HEAD), BF16),
        compiler_params=_params(("parallel",), 48),
        name="attn_sample",
    )(q_abs, q_rope, cache_ckv, cache_kr_t, ckv_new, kr_new, w_uv)


def _mix_out_kernel(u_ref, pa_ref, ob_ref, h_ref, wga_ref, wgb_ref, wpo_ref, woa_ref, wout_ref, acc_ref):
    j = pl.program_id(1)
    u = u_ref[...]
    gate_a = jax.nn.sigmoid(_dot_t(u, wga_ref[...]))
    gate_b = jax.nn.sigmoid(_dot_t(u, wgb_ref[...]))
    a = _dot(pa_ref[...], wpo_ref[...])
    b = _dot(ob_ref[...], woa_ref[...])
    merged = (gate_a * a + gate_b * b).astype(BF16)

    @pl.when(j == 0)
    def _():
        acc_ref[...] = h_ref[...]

    acc_ref[...] += _dot(merged, wout_ref[...])


def _mix_out(u, pooled, o_attn, h, w_in_t, gate_row0, w_pool_out, w_o_attn, w_out, *, tm=512, tc=512):
    n, d = h.shape
    nc = d // tc
    row = lambda i, j: (i, 0)
    col = lambda i, j: (0, j)
    return pl.pallas_call(
        _mix_out_kernel,
        grid=(n // tm, nc),
        in_specs=[
            pl.BlockSpec((tm, d), row),
            pl.BlockSpec((tm, pooled.shape[1]), row),
            pl.BlockSpec((tm, o_attn.shape[1]), row),
            pl.BlockSpec((tm, d), row),
            pl.BlockSpec((pl.Element(tc), pl.Element(d)),
                         lambda i, j: (pl.multiple_of(gate_row0 + j * tc, BF16_ROWS), 0)),
            pl.BlockSpec((pl.Element(tc), pl.Element(d)),
                         lambda i, j: (pl.multiple_of(gate_row0 + d + j * tc, BF16_ROWS), 0)),
            pl.BlockSpec((w_pool_out.shape[0], tc), col),
            pl.BlockSpec((w_o_attn.shape[0], tc), col),
            pl.BlockSpec((tc, d), lambda i, j: (j, 0)),
        ],
        out_specs=pl.BlockSpec((tm, d), row),
        out_shape=jax.ShapeDtypeStruct((n, d), F32),
        compiler_params=_params(("parallel", "arbitrary"), 56),
        name="mix_out",
    )(u, pooled, o_attn, h, w_in_t, w_in_t, w_pool_out, w_o_attn, w_out)


def _rope_tables(pos):
    half = QK_ROPE // 2
    inv = ROPE_THETA ** (-jnp.arange(half, dtype=F32) * 2.0 / QK_ROPE)
    ang = pos.astype(F32)[:, None] * inv[None, :]
    c, s = jnp.cos(ang), jnp.sin(ang)
    cs = jnp.concatenate([c, c, -s, s], axis=-1)
    qt = (SM_SCALE * LOG2_E) * jnp.concatenate([jnp.ones((pos.shape[0], QK_NOPE), F32), cs], axis=-1)
    return cs, qt


def _dup_rope_cols(w):
    half = QK_ROPE // 2
    x1, x2 = w[..., :half], w[..., half:]
    return jnp.concatenate([x1, x2, x2, x1], axis=-1)


def kernel(x_prompt, x_sample, cache_ckv, cache_krope, state_pool, g_ffn1, w1_gate, w1_up, w1_down, g_mix, w_in, g_q_lat, g_kv_lat, w_uq, w_uk, w_uv, w_o_attn, w_pool, pool_scale, w_pool_out, w_out, g_ffn2, w2_gate, w2_up, w2_down, g_final):
    bp, tp, d = x_prompt.shape
    bs, ts, _ = x_sample.shape
    assert g_ffn1.shape[0] == 1, "single-layer stack only"
    assert ts >= POOL_HIST and tp >= POOL_HIST
    past = cache_ckv.shape[2]
    pw = pool_scale.shape[1]
    ql = g_q_lat.shape[1]
    kl = g_kv_lat.shape[1]
    o3 = pw + ql + kl
    o4 = o3 + QK_ROPE

    cs_p, qt_p = _rope_tables(jnp.arange(tp, dtype=jnp.int32))
    cs_s, qt_s = _rope_tables(past + jnp.arange(ts, dtype=jnp.int32))
    cs_s = jnp.tile(cs_s, (bs, 1))
    qt_s = jnp.tile(qt_s, (bs, 1))

    bf = lambda w: w[0].astype(BF16)
    vec = lambda g: g[0].reshape(1, -1)
    w_in_t = w_in[0].T.astype(BF16)
    w_kr = _dup_rope_cols(w_in_t[o3:o4].T).T
    wq_cat = jnp.concatenate(
        [w_uq[0][..., :QK_NOPE], _dup_rope_cols(w_uq[0][..., QK_NOPE:])], axis=-1
    ).reshape(ql, N_HEADS * HEAD_W).astype(BF16)
    wk = w_uk[0].reshape(kl, N_HEADS * QK_NOPE).astype(BF16)
    wv = w_uv[0].reshape(kl, N_HEADS * V_HEAD).astype(BF16)
    wpool = bf(w_pool)
    hist_s = jnp.pad(state_pool[0], ((0, 0), (POOL_PAD - POOL_HIST, 0), (0, 0)))
    g_last = g_final.reshape(1, -1)

    def in_proj(u, cs_tab):
        return _inproj(u, w_in_t, w_kr, vec(g_q_lat), vec(g_kv_lat), cs_tab, pw=pw, ql=ql, kl=kl)


    h1_s, u_s, w1g, w1u, w1d = _ffn(
        x_sample.reshape(bs * ts, d), vec(g_ffn1), w1_gate[0], w1_up[0], w1_down[0], vec(g_mix),
        emit_normed=True, cast_weights=True)
    z, q_lat, ckv_s, kr_s = in_proj(u_s, cs_s)
    z_s = z.reshape(bs, ts, pw)
    pooled_s = _pool_sample(z_s, hist_s, wpool, vec(pool_scale), pos0=past)
    q_abs_s, q_rope_s = _q_sample(q_lat, wq_cat, wk, qt_s)

    h1, u = _ffn(x_prompt.reshape(bp * tp, d), vec(g_ffn1), w1g, w1u, w1d, vec(g_mix), emit_normed=True)
    z, q_lat, ckv_p, kr_p = in_proj(u, cs_p)
    z_p = z.reshape(bp, tp, pw)
    pooled = _pool_prompt(z_p, wpool, vec(pool_scale)).reshape(bp * tp, pw)
    q_cat, k_cat, v = _qkv_prompt(q_lat, ckv_p, kr_p, wq_cat, wk, wv, qt_p)
    o, w2g, w2u, w2d, wpo, woa, wout = _attn_prompt(
        q_cat.reshape(bp, tp, -1), k_cat.reshape(bp, tp, -1), v.reshape(bp, tp, -1),
        cast=(w2_gate[0], w2_up[0], w2_down[0], w_pool_out[0], w_o_attn[0], w_out[0]))
    h2 = _mix_out(u, pooled, o.reshape(bp * tp, -1), h1, w_in_t, o4, wpo, woa, wout)
    (y_p,) = _ffn(h2, vec(g_ffn2), w2g, w2u, w2d, g_last, emit_normed=False)

    o_s = _attn_sample(q_abs_s, q_rope_s, cache_ckv[0], jnp.swapaxes(cache_krope[0], 1, 2), ckv_s, kr_s, wv, ts=ts)
    h2 = _mix_out(u_s, pooled_s, o_s, h1_s, w_in_t, o4, wpo, woa, wout)
    (y_s,) = _ffn(h2, vec(g_ffn2), w2g, w2u, w2d, g_last, emit_normed=False)

    return (
        y_p.reshape(bp, tp, d),
        y_s.reshape(bs, ts, d),
        ckv_p.reshape(1, bp, tp, kl),
        kr_p[:, :QK_ROPE].reshape(1, bp, tp, QK_ROPE),
        z_p[None, :, tp - POOL_HIST:],
        ckv_s.reshape(1, bs, ts, kl),
        kr_s[:, :QK_ROPE].reshape(1, bs, ts, QK_ROPE),
        z_s[None, :, ts - POOL_HIST:],
    )
```

```python
import functools

import jax
import jax.numpy as jnp
from jax import lax
from jax.experimental import pallas as pl
from jax.experimental.pallas import tpu as pltpu

F32 = jnp.float32
BF16 = jnp.bfloat16

CHUNK_SHIFT = 6
N_HEADS = 16
QK_NOPE = 128
QK_ROPE = 64
V_HEAD = 128
HEAD_W = 256
LANES = 128
BF16_ROWS = 16
STRIP = 32
POOL_WINDOWS = (2, 4, 8, 16)
POOL_HIST = 15
POOL_PAD = 16
ROPE_THETA = 10000.0
EPS = 1e-6
SM_SCALE = (QK_NOPE + QK_ROPE) ** -0.5
LOG2_E = 1.4426950408889634
NEG_INF = -1e30
MIB = 1024 * 1024


def _dot(a, b):
    return jnp.dot(a, b, preferred_element_type=F32)


def _dot_t(a, b):
    return lax.dot_general(a, b, (((1,), (1,)), ((), ())), preferred_element_type=F32)


def _rms(x, g):
    return x * lax.rsqrt(jnp.mean(x * x, axis=-1, keepdims=True) + EPS) * g


def _params(semantics, vmem_mib):
    return pltpu.CompilerParams(dimension_semantics=semantics, vmem_limit_bytes=vmem_mib * MIB)


def _ffn_kernel(x_ref, g_ref, wg_ref, wu_ref, wd_ref, gn_ref, acc_ref, *rest, emit_normed, cast_weights):
    rest = list(rest)
    u_ref = rest.pop(0) if emit_normed else None
    xn_ref = rest.pop()
    w_dst = rest
    j = pl.program_id(1)

    @pl.when(j == 0)
    def _():
        xn_ref[...] = _rms(x_ref[...], g_ref[...]).astype(BF16)
        acc_ref[...] = jnp.zeros_like(acc_ref)

    wg, wu, wd = wg_ref[...], wu_ref[...], wd_ref[...]
    if cast_weights:
        wg, wu, wd = wg.astype(BF16), wu.astype(BF16), wd.astype(BF16)
        for dst, w in zip(w_dst, (wg, wu, wd)):
            dst[...] = w
    xn = xn_ref[...]
    gate = _dot(xn, wg)
    up = _dot(xn, wu)
    act = (gate * jax.nn.sigmoid(gate) * up).astype(BF16)
    acc_ref[...] += _dot(act, wd)

    @pl.when(j == pl.num_programs(1) - 1)
    def _():
        h = x_ref[...] + 0.5 * acc_ref[...]
        if emit_normed:
            acc_ref[...] = h
            u_ref[...] = _rms(h, gn_ref[...]).astype(BF16)
        else:
            acc_ref[...] = _rms(h, gn_ref[...])


def _side_cast_specs(mats, n0, n1, to01):
    specs = []
    for w in mats:
        rows, cols = w.shape
        if rows % (n0 * BF16_ROWS) == 0 and cols % (n1 * LANES) == 0:
            specs.append(pl.BlockSpec((rows // n0, cols // n1), lambda *g: to01(*g)))
        else:
            assert rows % (n1 * BF16_ROWS) == 0 and cols % (n0 * LANES) == 0, w.shape
            specs.append(pl.BlockSpec((rows // n1, cols // n0), lambda *g: to01(*g)[::-1]))
    return specs


def _ffn(x, g, wg, wu, wd, g_next, *, emit_normed, cast_weights=False, tm=1024, tf=256):
    n, d = x.shape
    f = wg.shape[1]
    row = lambda i, j: (i, 0)
    assert not cast_weights or n == tm
    rows_mode = dict(pipeline_mode=pl.Buffered(1)) if n == tm else {}
    w_up_spec = pl.BlockSpec((d, tf), lambda i, j: (0, j))
    w_down_spec = pl.BlockSpec((tf, d), lambda i, j: (j, 0))
    out_shape = [jax.ShapeDtypeStruct((n, d), F32)]
    out_specs = [pl.BlockSpec((tm, d), row, **rows_mode)]
    if emit_normed:
        out_shape.append(jax.ShapeDtypeStruct((n, d), BF16))
        out_specs.append(pl.BlockSpec((tm, d), row, **rows_mode))
    if cast_weights:
        out_shape += [jax.ShapeDtypeStruct(w.shape, BF16) for w in (wg, wu, wd)]
        out_specs += [w_up_spec, w_up_spec, w_down_spec]
    return pl.pallas_call(
        functools.partial(_ffn_kernel, emit_normed=emit_normed, cast_weights=cast_weights),
        grid=(n // tm, f // tf),
        in_specs=[
            pl.BlockSpec((tm, d), row, **rows_mode),
            pl.BlockSpec((1, d), lambda i, j: (0, 0)),
            w_up_spec,
            w_up_spec,
            w_down_spec,
            pl.BlockSpec((1, d), lambda i, j: (0, 0)),
        ],
        out_specs=out_specs,
        out_shape=out_shape,
        scratch_shapes=[pltpu.VMEM((tm, d), BF16)],
        compiler_params=_params(("parallel", "arbitrary"), 62),
        name="ffn_norm" if emit_normed else "ffn_final",
    )(x, g, wg, wu, wd, g_next)


def _inproj_kernel(u_ref, w_ref, wkr_ref, gq_ref, gkv_ref, cs_ref, *rest, pw, ql, kl, seq_tiles):
    if seq_tiles:
        wpool_ref, scale_ref, z_ref, q_ref, ckv_ref, kr_ref, pooled_ref, ext_ref = rest

        @pl.when(pl.program_id(0) == 0)
        def _():
            ext_ref[...] = jnp.zeros_like(ext_ref)
    else:
        z_ref, q_ref, ckv_ref, kr_ref = rest

    u = u_ref[...]
    proj = _dot_t(u, w_ref[...])
    z_ref[...] = proj[:, :pw]
    if seq_tiles:
        tm = u.shape[0]
        t = pl.program_id(0) % seq_tiles
        ext_ref[0:POOL_PAD, :] = jnp.where(t == 0, 0.0, ext_ref[tm:tm + POOL_PAD, :])
        ext_ref[POOL_PAD:, :] = proj[:, :pw]
        _pool_rows(ext_ref[...], t * tm, wpool_ref, scale_ref, pooled_ref, 0, tm, 1)
    q_ref[...] = _rms(proj[:, pw:pw + ql], gq_ref[...]).astype(BF16)
    ckv_ref[...] = _rms(proj[:, pw + ql:pw + ql + kl], gkv_ref[...])
    y = _dot_t(u, wkr_ref[...]) * cs_ref[...]
    kr_ref[...] = y + pltpu.roll(y, QK_ROPE, axis=1)


def _inproj(u, w_in_t, w_kr, g_q, g_kv, cs_tab, *, pw, ql, kl, pool=None, tm=1024):
    n, d = u.shape
    period = cs_tab.shape[0] // tm
    row = lambda i: (i, 0)
    fixed = lambda i: (0, 0)
    in_specs = [
        pl.BlockSpec((tm, d), row),
        pl.BlockSpec((pw + ql + kl, d), fixed),
        pl.BlockSpec(w_kr.shape, fixed),
        pl.BlockSpec((1, ql), fixed),
        pl.BlockSpec((1, kl), fixed),
        pl.BlockSpec((tm, 2 * QK_ROPE), lambda i: (i % period, 0)),
    ]
    out_specs = [
        pl.BlockSpec((tm, pw), row),
        pl.BlockSpec((tm, ql), row),
        pl.BlockSpec((tm, kl), row),
        pl.BlockSpec((tm, 2 * QK_ROPE), row),
    ]
    out_shape = [
        jax.ShapeDtypeStruct((n, pw), F32),
        jax.ShapeDtypeStruct((n, ql), BF16),
        jax.ShapeDtypeStruct((n, kl), F32),
        jax.ShapeDtypeStruct((n, 2 * QK_ROPE), F32),
    ]
    operands = [u, w_in_t, w_kr, g_q, g_kv, cs_tab]
    scratch, seq_tiles = [], 0
    if pool is not None:
        w_pool, pool_scale, seq_len = pool
        assert seq_len % tm == 0
        seq_tiles = seq_len // tm
        in_specs += [pl.BlockSpec(w_pool.shape, lambda i: (0, 0, 0)), pl.BlockSpec((1, pw), fixed)]
        out_specs.append(pl.BlockSpec((tm, pw), row))
        out_shape.append(jax.ShapeDtypeStruct((n, pw), BF16))
        operands += [w_pool, pool_scale]
        scratch = [pltpu.VMEM((POOL_PAD + tm, pw), F32)]
    return pl.pallas_call(
        functools.partial(_inproj_kernel, pw=pw, ql=ql, kl=kl, seq_tiles=seq_tiles),
        grid=(n // tm,),
        in_specs=in_specs,
        out_specs=out_specs,
        out_shape=out_shape,
        scratch_shapes=scratch,
        compiler_params=_params(("arbitrary",) if pool is not None else ("parallel",), 56),
        name="in_proj",
    )(*operands)


def _pool_rows(zext, pos0, w_ref, scale_ref, out_ref, out_row0, seg_rows, n_seg):
    p = zext.shape[1]
    gc = p // len(POOL_WINDOWS)
    ext = POOL_PAD + seg_rows
    pos = pos0 + lax.broadcasted_iota(jnp.int32, (seg_rows, 1), 0)
    for g, w in enumerate(POOL_WINDOWS):
        cols = slice(g * gc, (g + 1) * gc)
        s = zext[:, cols]
        win = s
        step = 1
        while step < w:
            win = win + pltpu.roll(win, step, axis=0)
            step *= 2
        cnt = jnp.minimum(pos + 1, w).astype(F32)
        for b in range(n_seg):
            lo = b * ext + POOL_PAD
            pooled = win[lo:lo + seg_rows] / cnt - s[lo:lo + seg_rows]
            mixed = _dot(pooled.astype(BF16), w_ref[g]) * scale_ref[:, cols]
            out_ref[pl.ds(out_row0 + b * seg_rows, seg_rows), cols] = mixed.astype(BF16)


def _pool_sample_kernel(cur_ref, hist_ref, w_ref, scale_ref, out_ref, ext_ref, *, nb, ts, pos0):
    ext = POOL_PAD + ts
    for b in range(nb):
        ext_ref[b * ext:b * ext + POOL_PAD, :] = hist_ref[b]
        ext_ref[b * ext + POOL_PAD:(b + 1) * ext, :] = cur_ref[b]
    _pool_rows(ext_ref[...], pos0, w_ref, scale_ref, out_ref, 0, ts, nb)


def _pool_sample(z, hist, w_pool, pool_scale, *, pos0, nb=8):
    b, ts, p = z.shape
    return pl.pallas_call(
        functools.partial(_pool_sample_kernel, nb=nb, ts=ts, pos0=pos0),
        grid=(b // nb,),
        in_specs=[
            pl.BlockSpec((nb, ts, p), lambda i: (i, 0, 0)),
            pl.BlockSpec((nb, POOL_PAD, p), lambda i: (i, 0, 0)),
            pl.BlockSpec(w_pool.shape, lambda i: (0, 0, 0)),
            pl.BlockSpec((1, p), lambda i: (0, 0)),
        ],
        out_specs=pl.BlockSpec((nb * ts, p), lambda i: (i, 0)),
        out_shape=jax.ShapeDtypeStruct((b * ts, p), BF16),
        scratch_shapes=[pltpu.VMEM((nb * (POOL_PAD + ts), p), F32)],
        compiler_params=_params(("parallel",), 32),
        name="pool_sample",
    )(z, hist, w_pool, pool_scale)


def _qkv_prompt_kernel(ql_ref, ckv_ref, kr_ref, wq_ref, wk_ref, wv_ref, qt_ref, q_ref, k_ref, v_ref):
    ql = ql_ref[...]
    ckv = ckv_ref[...].astype(BF16)
    kr = kr_ref[...].astype(BF16)
    qt = qt_ref[...]
    for h in range(N_HEADS):
        q_ref[:, h * HEAD_W:(h + 1) * HEAD_W] = (_dot(ql, wq_ref[:, h * HEAD_W:(h + 1) * HEAD_W]) * qt).astype(BF16)
    kn = _dot(ckv, wk_ref[...])
    for h in range(N_HEADS):
        k_ref[:, h * HEAD_W:h * HEAD_W + QK_NOPE] = kn[:, h * QK_NOPE:(h + 1) * QK_NOPE].astype(BF16)
        k_ref[:, h * HEAD_W + QK_NOPE:(h + 1) * HEAD_W] = kr
    v_ref[...] = _dot(ckv, wv_ref[...]).astype(BF16)


def _qkv_prompt(q_lat, ckv, krblk, wq_cat, w_uk, w_uv, qt_tab, *, tm=512):
    n = q_lat.shape[0]
    period = qt_tab.shape[0] // tm
    row = lambda i: (i, 0)
    fixed = lambda i: (0, 0)
    return pl.pallas_call(
        _qkv_prompt_kernel,
        grid=(n // tm,),
        in_specs=[
            pl.BlockSpec((tm, q_lat.shape[1]), row),
            pl.BlockSpec((tm, ckv.shape[1]), row),
            pl.BlockSpec((tm, krblk.shape[1]), row),
            pl.BlockSpec(wq_cat.shape, fixed),
            pl.BlockSpec(w_uk.shape, fixed),
            pl.BlockSpec(w_uv.shape, fixed),
            pl.BlockSpec((tm, HEAD_W), lambda i: (i % period, 0)),
        ],
        out_specs=[
            pl.BlockSpec((tm, N_HEADS * HEAD_W), row),
            pl.BlockSpec((tm, N_HEADS * HEAD_W), row),
            pl.BlockSpec((tm, N_HEADS * V_HEAD), row),
        ],
        out_shape=[
            jax.ShapeDtypeStruct((n, N_HEADS * HEAD_W), BF16),
            jax.ShapeDtypeStruct((n, N_HEADS * HEAD_W), BF16),
            jax.ShapeDtypeStruct((n, N_HEADS * V_HEAD), BF16),
        ],
        compiler_params=_params(("parallel",), 48),
        name="qkv_prompt",
    )(q_lat, ckv, krblk, wq_cat, w_uk, w_uv, qt_tab)


def _softmax_strips(s_ref, p_ref, m_ref, mc_ref, a_ref, masked):
    blk = s_ref.shape[1]

    def strip(r0):
        if not masked:
            return s_ref[r0:r0 + STRIP, :], blk
        visible = ((r0 >> CHUNK_SHIFT) + 1) << CHUNK_SHIFT
        width = -(-visible // LANES) * LANES
        s = s_ref[r0:r0 + STRIP, :width]
        return jnp.where(lax.broadcasted_iota(jnp.int32, s.shape, 1) < visible, s, NEG_INF), width

    for r0 in range(0, blk, STRIP):
        s, _ = strip(r0)
        mc_ref[r0:r0 + STRIP, :] = jnp.broadcast_to(jnp.max(s, axis=-1, keepdims=True), (STRIP, LANES))
    m_old = m_ref[...]
    m_new = jnp.maximum(m_old, mc_ref[...])
    m_ref[...] = m_new
    a_ref[...] = jnp.exp2(m_old - m_new)
    for r0 in range(0, blk, STRIP):
        s, width = strip(r0)
        p_ref[r0:r0 + STRIP, :width] = jnp.exp2(s - jnp.tile(m_ref[r0:r0 + STRIP, :], (1, width // LANES))).astype(BF16)
        if width < blk:
            p_ref[r0:r0 + STRIP, width:] = jnp.zeros((STRIP, blk - width), BF16)


def _attn_prompt_kernel(q_ref, k_ref, v_ref, *rest, blk, hb, n_cast):
    cast_src, o_ref, cast_dst = rest[:n_cast], rest[n_cast], rest[n_cast + 1:2 * n_cast + 1]
    s_ref, p_ref, m_ref, mc_ref, a_ref, acc_ref = rest[2 * n_cast + 1:]
    qi = pl.program_id(2)
    ones = jnp.ones((blk, V_HEAD), BF16)
    m_ref[...] = jnp.full(m_ref.shape, NEG_INF, F32)
    acc_ref[...] = jnp.zeros_like(acc_ref)

    def block(off, masked):
        for h in range(hb):
            q = q_ref[0, :, h * HEAD_W:(h + 1) * HEAD_W]
            s_ref[h] = _dot_t(q, k_ref[0, pl.ds(off, blk), h * HEAD_W:(h + 1) * HEAD_W])
        for h in range(hb):
            _softmax_strips(s_ref.at[h], p_ref.at[h], m_ref.at[h], mc_ref.at[h], a_ref.at[h], masked)
        for h in range(hb):
            v = v_ref[0, pl.ds(off, blk), h * V_HEAD:(h + 1) * V_HEAD]
            pv = _dot(p_ref[h], jnp.concatenate([v, ones], axis=1))
            acc_ref[h] = jnp.tile(a_ref[h], (1, 2)) * acc_ref[h] + pv

    def below_diagonal(j, carry):
        block(pl.multiple_of(j * blk, blk), masked=False)
        return carry

    lax.fori_loop(0, qi, below_diagonal, 0)
    for src, dst in zip(cast_src, cast_dst):
        dst[...] = src[...].astype(BF16)
    block(pl.multiple_of(qi * blk, blk), masked=True)
    for h in range(hb):
        acc = acc_ref[h]
        o_ref[0, :, h * V_HEAD:(h + 1) * V_HEAD] = (acc[:, :V_HEAD] * (1.0 / acc[:, V_HEAD:])).astype(BF16)


def _attn_prompt(q_cat, k_cat, v, *, cast=(), blk=512, hb=4):
    b, t, _ = q_cat.shape
    assert STRIP <= (1 << CHUNK_SHIFT) and blk % (1 << CHUNK_SHIFT) == 0
    ng, nq = N_HEADS // hb, t // blk
    cast_specs = _side_cast_specs(cast, b * ng, nq, lambda i, g, q: (i * ng + g, q))
    return pl.pallas_call(
        functools.partial(_attn_prompt_kernel, blk=blk, hb=hb, n_cast=len(cast)),
        grid=(b, ng, nq),
        in_specs=[
            pl.BlockSpec((1, blk, hb * HEAD_W), lambda i, g, q: (i, q, g)),
            pl.BlockSpec((1, t, hb * HEAD_W), lambda i, g, q: (i, 0, g)),
            pl.BlockSpec((1, t, hb * V_HEAD), lambda i, g, q: (i, 0, g)),
            *cast_specs,
        ],
        out_specs=[pl.BlockSpec((1, blk, hb * V_HEAD), lambda i, g, q: (i, q, g)), *cast_specs],
        out_shape=[
            jax.ShapeDtypeStruct((b, t, N_HEADS * V_HEAD), BF16),
            *[jax.ShapeDtypeStruct(w.shape, BF16) for w in cast],
        ],
        scratch_shapes=[
            pltpu.VMEM((hb, blk, blk), F32),
            pltpu.VMEM((hb, blk, blk), BF16),
            pltpu.VMEM((hb, blk, LANES), F32),
            pltpu.VMEM((hb, blk, LANES), F32),
            pltpu.VMEM((hb, blk, LANES), F32),
            pltpu.VMEM((hb, blk, 2 * V_HEAD), F32),
        ],
        compiler_params=_params(("parallel", "parallel", "arbitrary"), 48),
        name="attn_prompt",
    )(q_cat, k_cat, v, *cast)


def _q_sample_kernel(ql_ref, wq_ref, wk_ref, qt_ref, qa_ref, qr_ref, *, kl):
    ql = ql_ref[...]
    qt = qt_ref[...]
    for h in range(N_HEADS):
        q = _dot(ql, wq_ref[:, h * HEAD_W:(h + 1) * HEAD_W]) * qt
        qn = q[:, :QK_NOPE].astype(BF16)
        qa_ref[:, h * kl:(h + 1) * kl] = _dot_t(qn, wk_ref[:, h * QK_NOPE:(h + 1) * QK_NOPE]).astype(BF16)
        y = q[:, QK_NOPE:]
        qr_ref[:, h * 2 * QK_ROPE:(h + 1) * 2 * QK_ROPE] = (y + pltpu.roll(y, QK_ROPE, axis=1)).astype(BF16)


def _q_sample(q_lat, wq_cat, w_uk, qt_tab, *, tm=512):
    n, ql = q_lat.shape
    kl = w_uk.shape[0]
    row = lambda i: (i, 0)
    fixed = lambda i: (0, 0)
    return pl.pallas_call(
        functools.partial(_q_sample_kernel, kl=kl),
        grid=(n // tm,),
        in_specs=[
            pl.BlockSpec((tm, ql), row),
            pl.BlockSpec(wq_cat.shape, fixed),
            pl.BlockSpec(w_uk.shape, fixed),
            pl.BlockSpec((tm, HEAD_W), row),
        ],
        out_specs=[
            pl.BlockSpec((tm, N_HEADS * kl), row),
            pl.BlockSpec((tm, N_HEADS * 2 * QK_ROPE), row),
        ],
        out_shape=[
            jax.ShapeDtypeStruct((n, N_HEADS * kl), BF16),
            jax.ShapeDtypeStruct((n, N_HEADS * 2 * QK_ROPE), BF16),
        ],
        compiler_params=_params(("parallel",), 48),
        name="q_sample",
    )(q_lat, wq_cat, w_uk, qt_tab)


def _attn_sample_kernel(qa_ref, qr_ref, cc_ref, ck_ref, nc_ref, nk_ref, wv_ref, o_ref, *, nb, ts, past, kl):
    rows = N_HEADS * ts
    q_chunk = (past + lax.broadcasted_iota(jnp.int32, (rows, 1), 0) % ts) >> CHUNK_SHIFT
    vis_c = (lax.broadcasted_iota(jnp.int32, (1, past), 1) >> CHUNK_SHIFT) <= q_chunk
    vis_n = ((past + lax.broadcasted_iota(jnp.int32, (1, ts), 1)) >> CHUNK_SHIFT) <= q_chunk
    for b in range(nb):
        tok = slice(b * ts, (b + 1) * ts)
        qs = jnp.concatenate([qa_ref[tok, h * kl:(h + 1) * kl] for h in range(N_HEADS)], axis=0)
        qr = jnp.concatenate(
            [qr_ref[tok, h * 2 * QK_ROPE:h * 2 * QK_ROPE + QK_ROPE] for h in range(N_HEADS)], axis=0)
        kc = cc_ref[b].astype(BF16)
        krc_t = ck_ref[b].astype(BF16)
        kn = nc_ref[tok, :].astype(BF16)
        krn = nk_ref[tok, :QK_ROPE].astype(BF16)
        s_c = jnp.where(vis_c, _dot_t(qs, kc) + _dot(qr, krc_t), NEG_INF)
        s_n = jnp.where(vis_n, _dot_t(qs, kn) + _dot_t(qr, krn), NEG_INF)
        m = jnp.maximum(jnp.max(s_c, axis=-1, keepdims=True), jnp.max(s_n, axis=-1, keepdims=True))
        p_c = jnp.exp2(s_c - m)
        p_n = jnp.exp2(s_n - m)
        l = jnp.sum(p_c, axis=-1, keepdims=True) + jnp.sum(p_n, axis=-1, keepdims=True)
        o_lat = ((_dot(p_c.astype(BF16), kc) + _dot(p_n.astype(BF16), kn)) * (1.0 / l)).astype(BF16)
        for h in range(N_HEADS):
            o_ref[tok, h * V_HEAD:(h + 1) * V_HEAD] = _dot(
                o_lat[h * ts:(h + 1) * ts], wv_ref[:, h * V_HEAD:(h + 1) * V_HEAD]).astype(BF16)


def _attn_sample(q_abs, q_rope, cache_ckv, cache_kr_t, ckv_new, kr_new, w_uv, *, ts, nb=2):
    b, past, kl = cache_ckv.shape
    row = lambda i: (i, 0)
    return pl.pallas_call(
        functools.partial(_attn_sample_kernel, nb=nb, ts=ts, past=past, kl=kl),
        grid=(b // nb,),
        in_specs=[
            pl.BlockSpec((nb * ts, q_abs.shape[1]), row),
            pl.BlockSpec((nb * ts, q_rope.shape[1]), row),
            pl.BlockSpec((nb, past, kl), lambda i: (i, 0, 0)),
            pl.BlockSpec((nb, cache_kr_t.shape[1], past), lambda i: (i, 0, 0)),
            pl.BlockSpec((nb * ts, kl), row),
            pl.BlockSpec((nb * ts, kr_new.shape[1]), row),
            pl.BlockSpec(w_uv.shape, lambda i: (0, 0)),
        ],
        out_specs=pl.BlockSpec((nb * ts, N_HEADS * V_HEAD), row),
        out_shape=jax.ShapeDtypeStruct((b * ts, N_HEADS * V_HEAD), BF16),
        compiler_params=_params(("parallel",), 48),
        name="attn_sample",
    )(q_abs, q_rope, cache_ckv, cache_kr_t, ckv_new, kr_new, w_uv)


def _mix_out_kernel(u_ref, pa_ref, ob_ref, h_ref, wga_ref, wgb_ref, wpo_ref, woa_ref, wout_ref, acc_ref):
    j = pl.program_id(1)
    u = u_ref[...]
    gate_a = jax.nn.sigmoid(_dot_t(u, wga_ref[...]))
    gate_b = jax.nn.sigmoid(_dot_t(u, wgb_ref[...]))
    a = _dot(pa_ref[...], wpo_ref[...])
    b = _dot(ob_ref[...], woa_ref[...])
    merged = (gate_a * a + gate_b * b).astype(BF16)

    @pl.when(j == 0)
    def _():
        acc_ref[...] = h_ref[...]

    acc_ref[...] += _dot(merged, wout_ref[...])


def _mix_out(u, pooled, o_attn, h, w_in_t, gate_row0, w_pool_out, w_o_attn, w_out, *, tm=512, tc=512):
    n, d = h.shape
    nc = d // tc
    row = lambda i, j: (i, 0)
    col = lambda i, j: (0, j)
    return pl.pallas_call(
        _mix_out_kernel,
        grid=(n // tm, nc),
        in_specs=[
            pl.BlockSpec((tm, d), row),
            pl.BlockSpec((tm, pooled.shape[1]), row),
            pl.BlockSpec((tm, o_attn.shape[1]), row),
            pl.BlockSpec((tm, d), row),
            pl.BlockSpec((pl.Element(tc), pl.Element(d)),
                         lambda i, j: (pl.multiple_of(gate_row0 + j * tc, BF16_ROWS), 0)),
            pl.BlockSpec((pl.Element(tc), pl.Element(d)),
                         lambda i, j: (pl.multiple_of(gate_row0 + d + j * tc, BF16_ROWS), 0)),
            pl.BlockSpec((w_pool_out.shape[0], tc), col),
            pl.BlockSpec((w_o_attn.shape[0], tc), col),
            pl.BlockSpec((tc, d), lambda i, j: (j, 0)),
        ],
        out_specs=pl.BlockSpec((tm, d), row),
        out_shape=jax.ShapeDtypeStruct((n, d), F32),
        compiler_params=_params(("parallel", "arbitrary"), 56),
        name="mix_out",
    )(u, pooled, o_attn, h, w_in_t, w_in_t, w_pool_out, w_o_attn, w_out)


def _rope_tables(pos):
    half = QK_ROPE // 2
    inv = ROPE_THETA ** (-jnp.arange(half, dtype=F32) * 2.0 / QK_ROPE)
    ang = pos.astype(F32)[:, None] * inv[None, :]
    c, s = jnp.cos(ang), jnp.sin(ang)
    cs = jnp.concatenate([c, c, -s, s], axis=-1)
    qt = (SM_SCALE * LOG2_E) * jnp.concatenate([jnp.ones((pos.shape[0], QK_NOPE), F32), cs], axis=-1)
    return cs, qt


def _dup_rope_cols(w):
    half = QK_ROPE // 2
    x1, x2 = w[..., :half], w[..., half:]
    return jnp.concatenate([x1, x2, x2, x1], axis=-1)


def kernel(x_prompt, x_sample, cache_ckv, cache_krope, state_pool, g_ffn1, w1_gate, w1_up, w1_down, g_mix, w_in, g_q_lat, g_kv_lat, w_uq, w_uk, w_uv, w_o_attn, w_pool, pool_scale, w_pool_out, w_out, g_ffn2, w2_gate, w2_up, w2_down, g_final):
    bp, tp, d = x_prompt.shape
    bs, ts, _ = x_sample.shape
    assert g_ffn1.shape[0] == 1, "single-layer stack only"
    assert ts >= POOL_HIST and tp >= POOL_HIST
    past = cache_ckv.shape[2]
    pw = pool_scale.shape[1]
    ql = g_q_lat.shape[1]
    kl = g_kv_lat.shape[1]
    o3 = pw + ql + kl
    o4 = o3 + QK_ROPE

    cs_p, qt_p = _rope_tables(jnp.arange(tp, dtype=jnp.int32))
    cs_s, qt_s = _rope_tables(past + jnp.arange(ts, dtype=jnp.int32))
    cs_s = jnp.tile(cs_s, (bs, 1))
    qt_s = jnp.tile(qt_s, (bs, 1))

    bf = lambda w: w[0].astype(BF16)
    vec = lambda g: g[0].reshape(1, -1)
    w_in_t = w_in[0].T.astype(BF16)
    w_kr = _dup_rope_cols(w_in_t[o3:o4].T).T
    wq_cat = jnp.concatenate(
        [w_uq[0][..., :QK_NOPE], _dup_rope_cols(w_uq[0][..., QK_NOPE:])], axis=-1
    ).reshape(ql, N_HEADS * HEAD_W).astype(BF16)
    wk = w_uk[0].reshape(kl, N_HEADS * QK_NOPE).astype(BF16)
    wv = w_uv[0].reshape(kl, N_HEADS * V_HEAD).astype(BF16)
    wpool = bf(w_pool)
    hist_s = jnp.pad(state_pool[0], ((0, 0), (POOL_PAD - POOL_HIST, 0), (0, 0)))
    g_last = g_final.reshape(1, -1)

    def in_proj(u, cs_tab, pool=None):
        return _inproj(u, w_in_t, w_kr, vec(g_q_lat), vec(g_kv_lat), cs_tab, pw=pw, ql=ql, kl=kl, pool=pool)


    h1_s, u_s, w1g, w1u, w1d = _ffn(
        x_sample.reshape(bs * ts, d), vec(g_ffn1), w1_gate[0], w1_up[0], w1_down[0], vec(g_mix),
        emit_normed=True, cast_weights=True)
    z, q_lat, ckv_s, kr_s = in_proj(u_s, cs_s)
    z_s = z.reshape(bs, ts, pw)
    pooled_s = _pool_sample(z_s, hist_s, wpool, vec(pool_scale), pos0=past)
    q_abs_s, q_rope_s = _q_sample(q_lat, wq_cat, wk, qt_s)

    h1, u = _ffn(x_prompt.reshape(bp * tp, d), vec(g_ffn1), w1g, w1u, w1d, vec(g_mix), emit_normed=True)
    z, q_lat, ckv_p, kr_p, pooled = in_proj(u, cs_p, pool=(wpool, vec(pool_scale), tp))
    z_p = z.reshape(bp, tp, pw)
    q_cat, k_cat, v = _qkv_prompt(q_lat, ckv_p, kr_p, wq_cat, wk, wv, qt_p)
    o, w2g, w2u, w2d, wpo, woa, wout = _attn_prompt(
        q_cat.reshape(bp, tp, -1), k_cat.reshape(bp, tp, -1), v.reshape(bp, tp, -1),
        cast=(w2_gate[0], w2_up[0], w2_down[0], w_pool_out[0], w_o_attn[0], w_out[0]))
    h2 = _mix_out(u, pooled, o.reshape(bp * tp, -1), h1, w_in_t, o4, wpo, woa, wout)
    (y_p,) = _ffn(h2, vec(g_ffn2), w2g, w2u, w2d, g_last, emit_normed=False)

    o_s = _attn_sample(q_abs_s, q_rope_s, cache_ckv[0], jnp.swapaxes(cache_krope[0], 1, 2), ckv_s, kr_s, wv, ts=ts)
    h2 = _mix_out(u_s, pooled_s, o_s, h1_s, w_in_t, o4, wpo, woa, wout)
    (y_s,) = _ffn(h2, vec(g_ffn2), w2g, w2u, w2d, g_last, emit_normed=False)

    return (
        y_p.reshape(bp, tp, d),
        y_s.reshape(bs, ts, d),
        ckv_p.reshape(1, bp, tp, kl),
        kr_p[:, :QK_ROPE].reshape(1, bp, tp, QK_ROPE),
        z_p[None, :, tp - POOL_HIST:],
        ckv_s.reshape(1, bs, ts, kl),
        kr_s[:, :QK_ROPE].reshape(1, bs, ts, QK_ROPE),
        z_s[None, :, ts - POOL_HIST:],
    )
```

```python
import functools

import jax
import jax.numpy as jnp
from jax import lax
from jax.experimental import pallas as pl
from jax.experimental.pallas import tpu as pltpu

F32 = jnp.float32
BF16 = jnp.bfloat16

CHUNK_SHIFT = 6
N_HEADS = 16
QK_NOPE = 128
QK_ROPE = 64
V_HEAD = 128
HEAD_W = 256
LANES = 128
BF16_ROWS = 16
STRIP = 32
POOL_WINDOWS = (2, 4, 8, 16)
POOL_HIST = 15
POOL_PAD = 16
ROPE_THETA = 10000.0
EPS = 1e-6
SM_SCALE = (QK_NOPE + QK_ROPE) ** -0.5
LOG2_E = 1.4426950408889634
NEG_INF = -1e30
MIB = 1024 * 1024


def _dot(a, b):
    return jnp.dot(a, b, preferred_element_type=F32)


def _dot_t(a, b):
    return lax.dot_general(a, b, (((1,), (1,)), ((), ())), preferred_element_type=F32)


def _rms(x, g):
    return x * lax.rsqrt(jnp.mean(x * x, axis=-1, keepdims=True) + EPS) * g


def _params(semantics, vmem_mib):
    return pltpu.CompilerParams(dimension_semantics=semantics, vmem_limit_bytes=vmem_mib * MIB)


def _ffn_kernel(x_ref, g_ref, wg_ref, wu_ref, wd_ref, gn_ref, acc_ref, *rest, emit_normed, cast_weights):
    rest = list(rest)
    u_ref = rest.pop(0) if emit_normed else None
    xn_ref = rest.pop()
    w_dst = rest
    j = pl.program_id(1)

    @pl.when(j == 0)
    def _():
        xn_ref[...] = _rms(x_ref[...], g_ref[...]).astype(BF16)
        acc_ref[...] = jnp.zeros_like(acc_ref)

    wg, wu, wd = wg_ref[...], wu_ref[...], wd_ref[...]
    if cast_weights:
        wg, wu, wd = wg.astype(BF16), wu.astype(BF16), wd.astype(BF16)
        for dst, w in zip(w_dst, (wg, wu, wd)):
            dst[...] = w
    xn = xn_ref[...]
    gate = _dot(xn, wg)
    up = _dot(xn, wu)
    act = (gate * jax.nn.sigmoid(gate) * up).astype(BF16)
    acc_ref[...] += _dot(act, wd)

    @pl.when(j == pl.num_programs(1) - 1)
    def _():
        h = x_ref[...] + 0.5 * acc_ref[...]
        if emit_normed:
            acc_ref[...] = h
            u_ref[...] = _rms(h, gn_ref[...]).astype(BF16)
        else:
            acc_ref[...] = _rms(h, gn_ref[...])


def _side_cast_specs(mats, n0, n1, to01):
    specs = []
    for w in mats:
        rows, cols = w.shape
        if rows % (n0 * BF16_ROWS) == 0 and cols % (n1 * LANES) == 0:
            specs.append(pl.BlockSpec((rows // n0, cols // n1), lambda *g: to01(*g)))
        else:
            assert rows % (n1 * BF16_ROWS) == 0 and cols % (n0 * LANES) == 0, w.shape
            specs.append(pl.BlockSpec((rows // n1, cols // n0), lambda *g: to01(*g)[::-1]))
    return specs


def _ffn(x, g, wg, wu, wd, g_next, *, emit_normed, cast_weights=False, tm=1024, tf=256):
    n, d = x.shape
    f = wg.shape[1]
    row = lambda i, j: (i, 0)
    assert not cast_weights or n == tm
    rows_mode = dict(pipeline_mode=pl.Buffered(1)) if n == tm else {}
    w_up_spec = pl.BlockSpec((d, tf), lambda i, j: (0, j))
    w_down_spec = pl.BlockSpec((tf, d), lambda i, j: (j, 0))
    out_shape = [jax.ShapeDtypeStruct((n, d), F32)]
    out_specs = [pl.BlockSpec((tm, d), row, **rows_mode)]
    if emit_normed:
        out_shape.append(jax.ShapeDtypeStruct((n, d), BF16))
        out_specs.append(pl.BlockSpec((tm, d), row, **rows_mode))
    if cast_weights:
        out_shape += [jax.ShapeDtypeStruct(w.shape, BF16) for w in (wg, wu, wd)]
        out_specs += [w_up_spec, w_up_spec, w_down_spec]
    return pl.pallas_call(
        functools.partial(_ffn_kernel, emit_normed=emit_normed, cast_weights=cast_weights),
        grid=(n // tm, f // tf),
        in_specs=[
            pl.BlockSpec((tm, d), row, **rows_mode),
            pl.BlockSpec((1, d), lambda i, j: (0, 0)),
            w_up_spec,
            w_up_spec,
            w_down_spec,
            pl.BlockSpec((1, d), lambda i, j: (0, 0)),
        ],
        out_specs=out_specs,
        out_shape=out_shape,
        scratch_shapes=[pltpu.VMEM((tm, d), BF16)],
        compiler_params=_params(("parallel", "arbitrary"), 62),
        name="ffn_norm" if emit_normed else "ffn_final",
    )(x, g, wg, wu, wd, g_next)


def _inproj_kernel(u_ref, w_ref, wkr_ref, gq_ref, gkv_ref, cs_ref, *rest, pw, ql, kl, seq_tiles):
    if seq_tiles:
        wpool_ref, scale_ref, z_ref, q_ref, ckv_ref, kr_ref, pooled_ref, ext_ref = rest

        @pl.when(pl.program_id(0) == 0)
        def _():
            ext_ref[...] = jnp.zeros_like(ext_ref)
    else:
        z_ref, q_ref, ckv_ref, kr_ref = rest

    u = u_ref[...]
    proj = _dot_t(u, w_ref[...])
    z_ref[...] = proj[:, :pw]
    if seq_tiles:
        tm = u.shape[0]
        t = pl.program_id(0) % seq_tiles
        ext_ref[0:POOL_PAD, :] = jnp.where(t == 0, 0.0, ext_ref[tm:tm + POOL_PAD, :])
        ext_ref[POOL_PAD:, :] = proj[:, :pw]
        _pool_rows(ext_ref[...], t * tm, wpool_ref, scale_ref, pooled_ref, 0, tm, 1)
    q_ref[...] = _rms(proj[:, pw:pw + ql], gq_ref[...]).astype(BF16)
    ckv_ref[...] = _rms(proj[:, pw + ql:pw + ql + kl], gkv_ref[...])
    y = _dot_t(u, wkr_ref[...]) * cs_ref[...]
    kr_ref[...] = y + pltpu.roll(y, QK_ROPE, axis=1)


def _inproj(u, w_in_t, w_kr, g_q, g_kv, cs_tab, *, pw, ql, kl, pool=None, tm=1024):
    n, d = u.shape
    period = cs_tab.shape[0] // tm
    row = lambda i: (i, 0)
    fixed = lambda i: (0, 0)
    in_specs = [
        pl.BlockSpec((tm, d), row),
        pl.BlockSpec((pw + ql + kl, d), fixed),
        pl.BlockSpec(w_kr.shape, fixed),
        pl.BlockSpec((1, ql), fixed),
        pl.BlockSpec((1, kl), fixed),
        pl.BlockSpec((tm, 2 * QK_ROPE), lambda i: (i % period, 0)),
    ]
    out_specs = [
        pl.BlockSpec((tm, pw), row),
        pl.BlockSpec((tm, ql), row),
        pl.BlockSpec((tm, kl), row),
        pl.BlockSpec((tm, 2 * QK_ROPE), row),
    ]
    out_shape = [
        jax.ShapeDtypeStruct((n, pw), F32),
        jax.ShapeDtypeStruct((n, ql), BF16),
        jax.ShapeDtypeStruct((n, kl), F32),
        jax.ShapeDtypeStruct((n, 2 * QK_ROPE), F32),
    ]
    operands = [u, w_in_t, w_kr, g_q, g_kv, cs_tab]
    scratch, seq_tiles = [], 0
    if pool is not None:
        w_pool, pool_scale, seq_len = pool
        assert seq_len % tm == 0
        seq_tiles = seq_len // tm
        in_specs += [pl.BlockSpec(w_pool.shape, lambda i: (0, 0, 0)), pl.BlockSpec((1, pw), fixed)]
        out_specs.append(pl.BlockSpec((tm, pw), row))
        out_shape.append(jax.ShapeDtypeStruct((n, pw), BF16))
        operands += [w_pool, pool_scale]
        scratch = [pltpu.VMEM((POOL_PAD + tm, pw), F32)]
    return pl.pallas_call(
        functools.partial(_inproj_kernel, pw=pw, ql=ql, kl=kl, seq_tiles=seq_tiles),
        grid=(n // tm,),
        in_specs=in_specs,
        out_specs=out_specs,
        out_shape=out_shape,
        scratch_shapes=scratch,
        compiler_params=_params(("arbitrary",) if pool is not None else ("parallel",), 56),
        name="in_proj",
    )(*operands)


def _pool_rows(zext, pos0, w_ref, scale_ref, out_ref, out_row0, seg_rows, n_seg):
    p = zext.shape[1]
    gc = p // len(POOL_WINDOWS)
    ext = POOL_PAD + seg_rows
    pos = pos0 + lax.broadcasted_iota(jnp.int32, (seg_rows, 1), 0)
    for g, w in enumerate(POOL_WINDOWS):
        cols = slice(g * gc, (g + 1) * gc)
        s = zext[:, cols]
        win = s
        step = 1
        while step < w:
            win = win + pltpu.roll(win, step, axis=0)
            step *= 2
        cnt = jnp.minimum(pos + 1, w).astype(F32)
        for b in range(n_seg):
            lo = b * ext + POOL_PAD
            pooled = win[lo:lo + seg_rows] / cnt - s[lo:lo + seg_rows]
            mixed = _dot(pooled.astype(BF16), w_ref[g]) * scale_ref[:, cols]
            out_ref[pl.ds(out_row0 + b * seg_rows, seg_rows), cols] = mixed.astype(BF16)


def _pool_sample_kernel(cur_ref, hist_ref, w_ref, scale_ref, out_ref, ext_ref, *, nb, ts, pos0):
    ext = POOL_PAD + ts
    for b in range(nb):
        ext_ref[b * ext:b * ext + POOL_PAD, :] = hist_ref[b]
        ext_ref[b * ext + POOL_PAD:(b + 1) * ext, :] = cur_ref[b]
    _pool_rows(ext_ref[...], pos0, w_ref, scale_ref, out_ref, 0, ts, nb)


def _pool_sample(z, hist, w_pool, pool_scale, *, pos0, nb=8):
    b, ts, p = z.shape
    return pl.pallas_call(
        functools.partial(_pool_sample_kernel, nb=nb, ts=ts, pos0=pos0),
        grid=(b // nb,),
        in_specs=[
            pl.BlockSpec((nb, ts, p), lambda i: (i, 0, 0)),
            pl.BlockSpec((nb, POOL_PAD, p), lambda i: (i, 0, 0)),
            pl.BlockSpec(w_pool.shape, lambda i: (0, 0, 0)),
            pl.BlockSpec((1, p), lambda i: (0, 0)),
        ],
        out_specs=pl.BlockSpec((nb * ts, p), lambda i: (i, 0)),
        out_shape=jax.ShapeDtypeStruct((b * ts, p), BF16),
        scratch_shapes=[pltpu.VMEM((nb * (POOL_PAD + ts), p), F32)],
        compiler_params=_params(("parallel",), 32),
        name="pool_sample",
    )(z, hist, w_pool, pool_scale)


def _qkv_prompt_kernel(ql_ref, ckv_ref, kr_ref, wq_ref, wk_ref, wv_ref, qt_ref, q_ref, k_ref, v_ref):
    ql = ql_ref[...]
    ckv = ckv_ref[...].astype(BF16)
    kr = kr_ref[...].astype(BF16)
    qt = qt_ref[...]
    for h in range(N_HEADS):
        q_ref[:, h * HEAD_W:(h + 1) * HEAD_W] = (_dot(ql, wq_ref[:, h * HEAD_W:(h + 1) * HEAD_W]) * qt).astype(BF16)
    kn = _dot(ckv, wk_ref[...])
    for h in range(N_HEADS):
        k_ref[:, h * HEAD_W:h * HEAD_W + QK_NOPE] = kn[:, h * QK_NOPE:(h + 1) * QK_NOPE].astype(BF16)
        k_ref[:, h * HEAD_W + QK_NOPE:(h + 1) * HEAD_W] = kr
    v_ref[...] = _dot(ckv, wv_ref[...]).astype(BF16)


def _qkv_prompt(q_lat, ckv, krblk, wq_cat, w_uk, w_uv, qt_tab, *, tm=512):
    n = q_lat.shape[0]
    period = qt_tab.shape[0] // tm
    row = lambda i: (i, 0)
    fixed = lambda i: (0, 0)
    return pl.pallas_call(
        _qkv_prompt_kernel,
        grid=(n // tm,),
        in_specs=[
            pl.BlockSpec((tm, q_lat.shape[1]), row),
            pl.BlockSpec((tm, ckv.shape[1]), row),
            pl.BlockSpec((tm, krblk.shape[1]), row),
            pl.BlockSpec(wq_cat.shape, fixed),
            pl.BlockSpec(w_uk.shape, fixed),
            pl.BlockSpec(w_uv.shape, fixed),
            pl.BlockSpec((tm, HEAD_W), lambda i: (i % period, 0)),
        ],
        out_specs=[
            pl.BlockSpec((tm, N_HEADS * HEAD_W), row),
            pl.BlockSpec((tm, N_HEADS * HEAD_W), row),
            pl.BlockSpec((tm, N_HEADS * V_HEAD), row),
        ],
        out_shape=[
            jax.ShapeDtypeStruct((n, N_HEADS * HEAD_W), BF16),
            jax.ShapeDtypeStruct((n, N_HEADS * HEAD_W), BF16),
            jax.ShapeDtypeStruct((n, N_HEADS * V_HEAD), BF16),
        ],
        compiler_params=_params(("parallel",), 48),
        name="qkv_prompt",
    )(q_lat, ckv, krblk, wq_cat, w_uk, w_uv, qt_tab)


def _softmax_strips(s_ref, p_ref, m_ref, mc_ref, a_ref, masked):
    blk = s_ref.shape[1]

    def strip(r0):
        if not masked:
            return s_ref[r0:r0 + STRIP, :], blk
        visible = ((r0 >> CHUNK_SHIFT) + 1) << CHUNK_SHIFT
        width = -(-visible // LANES) * LANES
        s = s_ref[r0:r0 + STRIP, :width]
        return jnp.where(lax.broadcasted_iota(jnp.int32, s.shape, 1) < visible, s, NEG_INF), width

    for r0 in range(0, blk, STRIP):
        s, _ = strip(r0)
        mc_ref[r0:r0 + STRIP, :] = jnp.broadcast_to(jnp.max(s, axis=-1, keepdims=True), (STRIP, LANES))
    m_old = m_ref[...]
    m_new = jnp.maximum(m_old, mc_ref[...])
    m_ref[...] = m_new
    a_ref[...] = jnp.exp2(m_old - m_new)
    for r0 in range(0, blk, STRIP):
        s, width = strip(r0)
        p_ref[r0:r0 + STRIP, :width] = jnp.exp2(s - jnp.tile(m_ref[r0:r0 + STRIP, :], (1, width // LANES))).astype(BF16)
        if width < blk:
            p_ref[r0:r0 + STRIP, width:] = jnp.zeros((STRIP, blk - width), BF16)


def _attn_prompt_kernel(q_ref, k_ref, v_ref, *rest, blk, hb, n_cast):
    cast_src, o_ref, cast_dst = rest[:n_cast], rest[n_cast], rest[n_cast + 1:2 * n_cast + 1]
    s_ref, p_ref, m_ref, mc_ref, a_ref, acc_ref = rest[2 * n_cast + 1:]
    qi = pl.program_id(2)
    ones = jnp.ones((blk, V_HEAD), BF16)
    m_ref[...] = jnp.full(m_ref.shape, NEG_INF, F32)
    acc_ref[...] = jnp.zeros_like(acc_ref)

    def block(off, masked):
        for h in range(hb):
            q = q_ref[0, :, h * HEAD_W:(h + 1) * HEAD_W]
            s_ref[h] = _dot_t(q, k_ref[0, pl.ds(off, blk), h * HEAD_W:(h + 1) * HEAD_W])
        for h in range(hb):
            _softmax_strips(s_ref.at[h], p_ref.at[h], m_ref.at[h], mc_ref.at[h], a_ref.at[h], masked)
        for h in range(hb):
            v = v_ref[0, pl.ds(off, blk), h * V_HEAD:(h + 1) * V_HEAD]
            pv = _dot(p_ref[h], jnp.concatenate([v, ones], axis=1))
            acc_ref[h] = jnp.tile(a_ref[h], (1, 2)) * acc_ref[h] + pv

    def below_diagonal(j, carry):
        block(pl.multiple_of(j * blk, blk), masked=False)
        return carry

    lax.fori_loop(0, qi, below_diagonal, 0)
    for src, dst in zip(cast_src, cast_dst):
        dst[...] = src[...].astype(BF16)
    block(pl.multiple_of(qi * blk, blk), masked=True)
    for h in range(hb):
        acc = acc_ref[h]
        o_ref[0, :, h * V_HEAD:(h + 1) * V_HEAD] = (acc[:, :V_HEAD] * (1.0 / acc[:, V_HEAD:])).astype(BF16)


def _attn_prompt(q_cat, k_cat, v, *, cast=(), blk=512, hb=4):
    b, t, _ = q_cat.shape
    assert STRIP <= (1 << CHUNK_SHIFT) and blk % (1 << CHUNK_SHIFT) == 0
    ng, nq = N_HEADS // hb, t // blk
    cast_specs = _side_cast_specs(cast, b * ng, nq, lambda i, g, q: (i * ng + g, q))
    return pl.pallas_call(
        functools.partial(_attn_prompt_kernel, blk=blk, hb=hb, n_cast=len(cast)),
        grid=(b, ng, nq),
        in_specs=[
            pl.BlockSpec((1, blk, hb * HEAD_W), lambda i, g, q: (i, q, g)),
            pl.BlockSpec((1, t, hb * HEAD_W), lambda i, g, q: (i, 0, g)),
            pl.BlockSpec((1, t, hb * V_HEAD), lambda i, g, q: (i, 0, g)),
            *cast_specs,
        ],
        out_specs=[pl.BlockSpec((1, blk, hb * V_HEAD), lambda i, g, q: (i, q, g)), *cast_specs],
        out_shape=[
            jax.ShapeDtypeStruct((b, t, N_HEADS * V_HEAD), BF16),
            *[jax.ShapeDtypeStruct(w.shape, BF16) for w in cast],
        ],
        scratch_shapes=[
            pltpu.VMEM((hb, blk, blk), F32),
            pltpu.VMEM((hb, blk, blk), BF16),
            pltpu.VMEM((hb, blk, LANES), F32),
            pltpu.VMEM((hb, blk, LANES), F32),
            pltpu.VMEM((hb, blk, LANES), F32),
            pltpu.VMEM((hb, blk, 2 * V_HEAD), F32),
        ],
        compiler_params=_params(("parallel", "parallel", "arbitrary"), 48),
        name="attn_prompt",
    )(q_cat, k_cat, v, *cast)


def _q_sample_kernel(ql_ref, wq_ref, wk_ref, qt_ref, qa_ref, qr_ref, *, kl):
    ql = ql_ref[...]
    qt = qt_ref[...]
    for h in range(N_HEADS):
        q = _dot(ql, wq_ref[:, h * HEAD_W:(h + 1) * HEAD_W]) * qt
        qn = q[:, :QK_NOPE].astype(BF16)
        qa_ref[:, h * kl:(h + 1) * kl] = _dot_t(qn, wk_ref[:, h * QK_NOPE:(h + 1) * QK_NOPE]).astype(BF16)
        y = q[:, QK_NOPE:]
        qr_ref[:, h * 2 * QK_ROPE:(h + 1) * 2 * QK_ROPE] = (y + pltpu.roll(y, QK_ROPE, axis=1)).astype(BF16)


def _q_sample(q_lat, wq_cat, w_uk, qt_tab, *, tm=512):
    n, ql = q_lat.shape
    kl = w_uk.shape[0]
    row = lambda i: (i, 0)
    fixed = lambda i: (0, 0)
    return pl.pallas_call(
        functools.partial(_q_sample_kernel, kl=kl),
        grid=(n // tm,),
        in_specs=[
            pl.BlockSpec((tm, ql), row),
            pl.BlockSpec(wq_cat.shape, fixed),
            pl.BlockSpec(w_uk.shape, fixed),
            pl.BlockSpec((tm, HEAD_W), row),
        ],
        out_specs=[
            pl.BlockSpec((tm, N_HEADS * kl), row),
            pl.BlockSpec((tm, N_HEADS * 2 * QK_ROPE), row),
        ],
        out_shape=[
            jax.ShapeDtypeStruct((n, N_HEADS * kl), BF16),
            jax.ShapeDtypeStruct((n, N_HEADS * 2 * QK_ROPE), BF16),
        ],
        compiler_params=_params(("parallel",), 48),
        name="q_sample",
    )(q_lat, wq_cat, w_uk, qt_tab)


def _attn_sample_kernel(qa_ref, qr_ref, cc_ref, ck_ref, nc_ref, nk_ref, wv_ref, o_ref, *, nb, ts, past, kl):
    rows = N_HEADS * ts
    q_chunk = (past + lax.broadcasted_iota(jnp.int32, (rows, 1), 0) % ts) >> CHUNK_SHIFT
    vis_c = (lax.broadcasted_iota(jnp.int32, (1, past), 1) >> CHUNK_SHIFT) <= q_chunk
    vis_n = ((past + lax.broadcasted_iota(jnp.int32, (1, ts), 1)) >> CHUNK_SHIFT) <= q_chunk
    for b in range(nb):
        tok = slice(b * ts, (b + 1) * ts)
        qs = jnp.concatenate([qa_ref[tok, h * kl:(h + 1) * kl] for h in range(N_HEADS)], axis=0)
        qr = jnp.concatenate(
            [qr_ref[tok, h * 2 * QK_ROPE:h * 2 * QK_ROPE + QK_ROPE] for h in range(N_HEADS)], axis=0)
        kc = cc_ref[b].astype(BF16)
        krc_t = ck_ref[b].astype(BF16)
        kn = nc_ref[tok, :].astype(BF16)
        krn = nk_ref[tok, :QK_ROPE].astype(BF16)
        s_c = jnp.where(vis_c, _dot_t(qs, kc) + _dot(qr, krc_t), NEG_INF)
        s_n = jnp.where(vis_n, _dot_t(qs, kn) + _dot_t(qr, krn), NEG_INF)
        m = jnp.maximum(jnp.max(s_c, axis=-1, keepdims=True), jnp.max(s_n, axis=-1, keepdims=True))
        p_c = jnp.exp2(s_c - m)
        p_n = jnp.exp2(s_n - m)
        l = jnp.sum(p_c, axis=-1, keepdims=True) + jnp.sum(p_n, axis=-1, keepdims=True)
        o_lat = ((_dot(p_c.astype(BF16), kc) + _dot(p_n.astype(BF16), kn)) * (1.0 / l)).astype(BF16)
        for h in range(N_HEADS):
            o_ref[tok, h * V_HEAD:(h + 1) * V_HEAD] = _dot(
                o_lat[h * ts:(h + 1) * ts], wv_ref[:, h * V_HEAD:(h + 1) * V_HEAD]).astype(BF16)


def _attn_sample(q_abs, q_rope, cache_ckv, cache_kr_t, ckv_new, kr_new, w_uv, *, ts, nb=2):
    b, past, kl = cache_ckv.shape
    row = lambda i: (i, 0)
    return pl.pallas_call(
        functools.partial(_attn_sample_kernel, nb=nb, ts=ts, past=past, kl=kl),
        grid=(b // nb,),
        in_specs=[
            pl.BlockSpec((nb * ts, q_abs.shape[1]), row),
            pl.BlockSpec((nb * ts, q_rope.shape[1]), row),
            pl.BlockSpec((nb, past, kl), lambda i: (i, 0, 0)),
            pl.BlockSpec((nb, cache_kr_t.shape[1], past), lambda i: (i, 0, 0)),
            pl.BlockSpec((nb * ts, kl), row),
            pl.BlockSpec((nb * ts, kr_new.shape[1]), row),
            pl.BlockSpec(w_uv.shape, lambda i: (0, 0)),
        ],
        out_specs=pl.BlockSpec((nb * ts, N_HEADS * V_HEAD), row),
        out_shape=jax.ShapeDtypeStruct((b * ts, N_HEADS * V_HEAD), BF16),
        compiler_params=_params(("parallel",), 48),
        name="attn_sample",
    )(q_abs, q_rope, cache_ckv, cache_kr_t, ckv_new, kr_new, w_uv)


def _mix_out_kernel(u_ref, pa_ref, ob_ref, h_ref, wga_ref, wgb_ref, wpo_ref, woa_ref, wprev_ref, wlast_ref,
                    acc_ref, mprev_ref):
    j = pl.program_id(1)
    last = pl.num_programs(1) - 1

    def merged():
        u = u_ref[...]
        gate_a = jax.nn.sigmoid(_dot_t(u, wga_ref[...]))
        gate_b = jax.nn.sigmoid(_dot_t(u, wgb_ref[...]))
        a = _dot(pa_ref[...], wpo_ref[...])
        b = _dot(ob_ref[...], woa_ref[...])
        return (gate_a * a + gate_b * b).astype(BF16)

    @pl.when(j == 0)
    def _():
        acc_ref[...] = h_ref[...]
        mprev_ref[...] = merged()

    @pl.when(jnp.logical_and(j > 0, j < last))
    def _():
        m = merged()
        acc_ref[...] += _dot(mprev_ref[...], wprev_ref[...])
        mprev_ref[...] = m

    @pl.when(j == last)
    def _():
        m = merged()
        acc_ref[...] += _dot(mprev_ref[...], wprev_ref[...]) + _dot(m, wlast_ref[...])


def _mix_out(u, pooled, o_attn, h, w_in_t, gate_row0, w_pool_out, w_o_attn, w_out, *, tm=512, tc=512):
    n, d = h.shape
    nc = d // tc
    assert nc >= 2
    row = lambda i, j: (i, 0)
    col = lambda i, j: (0, j)
    return pl.pallas_call(
        _mix_out_kernel,
        grid=(n // tm, nc),
        in_specs=[
            pl.BlockSpec((tm, d), row),
            pl.BlockSpec((tm, pooled.shape[1]), row),
            pl.BlockSpec((tm, o_attn.shape[1]), row),
            pl.BlockSpec((tm, d), row),
            pl.BlockSpec((pl.Element(tc), pl.Element(d)),
                         lambda i, j: (pl.multiple_of(gate_row0 + j * tc, BF16_ROWS), 0)),
            pl.BlockSpec((pl.Element(tc), pl.Element(d)),
                         lambda i, j: (pl.multiple_of(gate_row0 + d + j * tc, BF16_ROWS), 0)),
            pl.BlockSpec((w_pool_out.shape[0], tc), col),
            pl.BlockSpec((w_o_attn.shape[0], tc), col),
            pl.BlockSpec((tc, d), lambda i, j: (jnp.maximum(j - 1, 0), 0)),
            pl.BlockSpec((tc, d), lambda i, j: (nc - 1, 0)),
        ],
        out_specs=pl.BlockSpec((tm, d), row),
        out_shape=jax.ShapeDtypeStruct((n, d), F32),
        scratch_shapes=[pltpu.VMEM((tm, tc), BF16)],
        compiler_params=_params(("parallel", "arbitrary"), 58),
        name="mix_out",
    )(u, pooled, o_attn, h, w_in_t, w_in_t, w_pool_out, w_o_attn, w_out, w_out)


def _rope_tables(pos):
    half = QK_ROPE // 2
    inv = ROPE_THETA ** (-jnp.arange(half, dtype=F32) * 2.0 / QK_ROPE)
    ang = pos.astype(F32)[:, None] * inv[None, :]
    c, s = jnp.cos(ang), jnp.sin(ang)
    cs = jnp.concatenate([c, c, -s, s], axis=-1)
    qt = (SM_SCALE * LOG2_E) * jnp.concatenate([jnp.ones((pos.shape[0], QK_NOPE), F32), cs], axis=-1)
    return cs, qt


def _dup_rope_cols(w):
    half = QK_ROPE // 2
    x1, x2 = w[..., :half], w[..., half:]
    return jnp.concatenate([x1, x2, x2, x1], axis=-1)


def kernel(x_prompt, x_sample, cache_ckv, cache_krope, state_pool, g_ffn1, w1_gate, w1_up, w1_down, g_mix, w_in, g_q_lat, g_kv_lat, w_uq, w_uk, w_uv, w_o_attn, w_pool, pool_scale, w_pool_out, w_out, g_ffn2, w2_gate, w2_up, w2_down, g_final):
    bp, tp, d = x_prompt.shape
    bs, ts, _ = x_sample.shape
    assert g_ffn1.shape[0] == 1, "single-layer stack only"
    assert ts >= POOL_HIST and tp >= POOL_HIST
    past = cache_ckv.shape[2]
    pw = pool_scale.shape[1]
    ql = g_q_lat.shape[1]
    kl = g_kv_lat.shape[1]
    o3 = pw + ql + kl
    o4 = o3 + QK_ROPE

    cs_p, qt_p = _rope_tables(jnp.arange(tp, dtype=jnp.int32))
    cs_s, qt_s = _rope_tables(past + jnp.arange(ts, dtype=jnp.int32))
    cs_s = jnp.tile(cs_s, (bs, 1))
    qt_s = jnp.tile(qt_s, (bs, 1))

    bf = lambda w: w[0].astype(BF16)
    vec = lambda g: g[0].reshape(1, -1)
    w_in_t = w_in[0].T.astype(BF16)
    w_kr = _dup_rope_cols(w_in_t[o3:o4].T).T
    wq_cat = jnp.concatenate(
        [w_uq[0][..., :QK_NOPE], _dup_rope_cols(w_uq[0][..., QK_NOPE:])], axis=-1
    ).reshape(ql, N_HEADS * HEAD_W).astype(BF16)
    wk = w_uk[0].reshape(kl, N_HEADS * QK_NOPE).astype(BF16)
    wv = w_uv[0].reshape(kl, N_HEADS * V_HEAD).astype(BF16)
    wpool = bf(w_pool)
    hist_s = jnp.pad(state_pool[0], ((0, 0), (POOL_PAD - POOL_HIST, 0), (0, 0)))
    g_last = g_final.reshape(1, -1)

    def in_proj(u, cs_tab, pool=None):
        return _inproj(u, w_in_t, w_kr, vec(g_q_lat), vec(g_kv_lat), cs_tab, pw=pw, ql=ql, kl=kl, pool=pool)


    h1_s, u_s, w1g, w1u, w1d = _ffn(
        x_sample.reshape(bs * ts, d), vec(g_ffn1), w1_gate[0], w1_up[0], w1_down[0], vec(g_mix),
        emit_normed=True, cast_weights=True)
    z, q_lat, ckv_s, kr_s = in_proj(u_s, cs_s)
    z_s = z.reshape(bs, ts, pw)
    pooled_s = _pool_sample(z_s, hist_s, wpool, vec(pool_scale), pos0=past)
    q_abs_s, q_rope_s = _q_sample(q_lat, wq_cat, wk, qt_s)

    h1, u = _ffn(x_prompt.reshape(bp * tp, d), vec(g_ffn1), w1g, w1u, w1d, vec(g_mix), emit_normed=True)
    z, q_lat, ckv_p, kr_p, pooled = in_proj(u, cs_p, pool=(wpool, vec(pool_scale), tp))
    z_p = z.reshape(bp, tp, pw)
    q_cat, k_cat, v = _qkv_prompt(q_lat, ckv_p, kr_p, wq_cat, wk, wv, qt_p)
    o, w2g, w2u, w2d, wpo, woa, wout = _attn_prompt(
        q_cat.reshape(bp, tp, -1), k_cat.reshape(bp, tp, -1), v.reshape(bp, tp, -1),
        cast=(w2_gate[0], w2_up[0], w2_down[0], w_pool_out[0], w_o_attn[0], w_out[0]))
    h2 = _mix_out(u, pooled, o.reshape(bp * tp, -1), h1, w_in_t, o4, wpo, woa, wout)
    (y_p,) = _ffn(h2, vec(g_ffn2), w2g, w2u, w2d, g_last, emit_normed=False)

    o_s = _attn_sample(q_abs_s, q_rope_s, cache_ckv[0], jnp.swapaxes(cache_krope[0], 1, 2), ckv_s, kr_s, wv, ts=ts)
    h2 = _mix_out(u_s, pooled_s, o_s, h1_s, w_in_t, o4, wpo, woa, wout)
    (y_s,) = _ffn(h2, vec(g_ffn2), w2g, w2u, w2d, g_last, emit_normed=False)

    return (
        y_p.reshape(bp, tp, d),
        y_s.reshape(bs, ts, d),
        ckv_p.reshape(1, bp, tp, kl),
        kr_p[:, :QK_ROPE].reshape(1, bp, tp, QK_ROPE),
        z_p[None, :, tp - POOL_HIST:],
        ckv_s.reshape(1, bs, ts, kl),
        kr_s[:, :QK_ROPE].reshape(1, bs, ts, QK_ROPE),
        z_s[None, :, ts - POOL_HIST:],
    )
```

```python
import functools

import jax
import jax.numpy as jnp
from jax import lax
from jax.experimental import pallas as pl
from jax.experimental.pallas import tpu as pltpu

F32 = jnp.float32
BF16 = jnp.bfloat16

CHUNK_SHIFT = 6
N_HEADS = 16
QK_NOPE = 128
QK_ROPE = 64
V_HEAD = 128
HEAD_W = 256
LANES = 128
BF16_ROWS = 16
STRIP = 32
POOL_WINDOWS = (2, 4, 8, 16)
POOL_HIST = 15
POOL_PAD = 16
ROPE_THETA = 10000.0
EPS = 1e-6
SM_SCALE = (QK_NOPE + QK_ROPE) ** -0.5
LOG2_E = 1.4426950408889634
NEG_INF = -1e30
MIB = 1024 * 1024


def _dot(a, b):
    return jnp.dot(a, b, preferred_element_type=F32)


def _dot_t(a, b):
    return lax.dot_general(a, b, (((1,), (1,)), ((), ())), preferred_element_type=F32)


def _rms(x, g):
    return x * lax.rsqrt(jnp.mean(x * x, axis=-1, keepdims=True) + EPS) * g


def _params(semantics, vmem_mib):
    return pltpu.CompilerParams(dimension_semantics=semantics, vmem_limit_bytes=vmem_mib * MIB)


def _ffn_cast_kernel(x_ref, g_ref, wg_ref, wu_ref, wd_ref, gn_ref, acc_ref, *rest, emit_normed):
    rest = list(rest)
    u_ref = rest.pop(0) if emit_normed else None
    xn_ref = rest.pop()
    w_dst = rest
    j = pl.program_id(1)

    @pl.when(j == 0)
    def _():
        xn_ref[...] = _rms(x_ref[...], g_ref[...]).astype(BF16)
        acc_ref[...] = jnp.zeros_like(acc_ref)

    wg, wu, wd = wg_ref[...].astype(BF16), wu_ref[...].astype(BF16), wd_ref[...].astype(BF16)
    for dst, w in zip(w_dst, (wg, wu, wd)):
        dst[...] = w
    xn = xn_ref[...]
    gate = _dot(xn, wg)
    up = _dot(xn, wu)
    act = (gate * jax.nn.sigmoid(gate) * up).astype(BF16)
    acc_ref[...] += _dot(act, wd)

    @pl.when(j == pl.num_programs(1) - 1)
    def _():
        h = x_ref[...] + 0.5 * acc_ref[...]
        if emit_normed:
            acc_ref[...] = h
            u_ref[...] = _rms(h, gn_ref[...]).astype(BF16)
        else:
            acc_ref[...] = _rms(h, gn_ref[...])


def _ffn_pipelined_kernel(x_ref, g_ref, wg_ref, wu_ref, wdp_ref, wdl_ref, gn_ref, acc_ref, *rest, emit_normed):
    rest = list(rest)
    u_ref = rest.pop(0) if emit_normed else None
    xn_ref, aprev_ref = rest
    j = pl.program_id(1)
    last = pl.num_programs(1) - 1

    def act():
        xn = xn_ref[...]
        gate = _dot(xn, wg_ref[...])
        up = _dot(xn, wu_ref[...])
        return (gate * jax.nn.sigmoid(gate) * up).astype(BF16)

    @pl.when(j == 0)
    def _():
        xn_ref[...] = _rms(x_ref[...], g_ref[...]).astype(BF16)
        acc_ref[...] = jnp.zeros_like(acc_ref)
        aprev_ref[...] = act()

    @pl.when(jnp.logical_and(j > 0, j < last))
    def _():
        a = act()
        acc_ref[...] += _dot(aprev_ref[...], wdp_ref[...])
        aprev_ref[...] = a

    @pl.when(j == last)
    def _():
        a = act()
        acc = acc_ref[...] + _dot(aprev_ref[...], wdp_ref[...]) + _dot(a, wdl_ref[...])
        h = x_ref[...] + 0.5 * acc
        if emit_normed:
            acc_ref[...] = h
            u_ref[...] = _rms(h, gn_ref[...]).astype(BF16)
        else:
            acc_ref[...] = _rms(h, gn_ref[...])


def _side_cast_specs(mats, n0, n1, to01):
    specs = []
    for w in mats:
        rows, cols = w.shape
        if rows % (n0 * BF16_ROWS) == 0 and cols % (n1 * LANES) == 0:
            specs.append(pl.BlockSpec((rows // n0, cols // n1), lambda *g: to01(*g)))
        else:
            assert rows % (n1 * BF16_ROWS) == 0 and cols % (n0 * LANES) == 0, w.shape
            specs.append(pl.BlockSpec((rows // n1, cols // n0), lambda *g: to01(*g)[::-1]))
    return specs


def _ffn(x, g, wg, wu, wd, g_next, *, emit_normed, cast_weights=False, tm=1024, tf=256):
    n, d = x.shape
    f = wg.shape[1]
    row = lambda i, j: (i, 0)
    assert not cast_weights or n == tm
    rows_mode = dict(pipeline_mode=pl.Buffered(1)) if n == tm else {}
    w_up_spec = pl.BlockSpec((d, tf), lambda i, j: (0, j))
    w_down_spec = pl.BlockSpec((tf, d), lambda i, j: (j, 0))
    out_shape = [jax.ShapeDtypeStruct((n, d), F32)]
    out_specs = [pl.BlockSpec((tm, d), row, **rows_mode)]
    if emit_normed:
        out_shape.append(jax.ShapeDtypeStruct((n, d), BF16))
        out_specs.append(pl.BlockSpec((tm, d), row, **rows_mode))
    nf = f // tf
    if cast_weights:
        out_shape += [jax.ShapeDtypeStruct(w.shape, BF16) for w in (wg, wu, wd)]
        out_specs += [w_up_spec, w_up_spec, w_down_spec]
        body = functools.partial(_ffn_cast_kernel, emit_normed=emit_normed)
        w_down_specs, w_down, scratch = [w_down_spec], [wd], [pltpu.VMEM((tm, d), BF16)]
    else:
        assert nf >= 2
        body = functools.partial(_ffn_pipelined_kernel, emit_normed=emit_normed)
        w_down_specs = [pl.BlockSpec((tf, d), lambda i, j: (jnp.maximum(j - 1, 0), 0)),
                        pl.BlockSpec((tf, d), lambda i, j: (nf - 1, 0))]
        w_down, scratch = [wd, wd], [pltpu.VMEM((tm, d), BF16), pltpu.VMEM((tm, tf), BF16)]
    return pl.pallas_call(
        body,
        grid=(n // tm, nf),
        in_specs=[
            pl.BlockSpec((tm, d), row, **rows_mode),
            pl.BlockSpec((1, d), lambda i, j: (0, 0)),
            w_up_spec,
            w_up_spec,
            *w_down_specs,
            pl.BlockSpec((1, d), lambda i, j: (0, 0)),
        ],
        out_specs=out_specs,
        out_shape=out_shape,
        scratch_shapes=scratch,
        compiler_params=_params(("parallel", "arbitrary"), 62),
        name="ffn_norm" if emit_normed else "ffn_final",
    )(x, g, wg, wu, *w_down, g_next)


def _inproj_kernel(u_ref, w_ref, wkr_ref, gq_ref, gkv_ref, cs_ref, *rest, pw, ql, kl, seq_tiles):
    if seq_tiles:
        wpool_ref, scale_ref, z_ref, q_ref, ckv_ref, kr_ref, pooled_ref, ext_ref = rest

        @pl.when(pl.program_id(0) == 0)
        def _():
            ext_ref[...] = jnp.zeros_like(ext_ref)
    else:
        z_ref, q_ref, ckv_ref, kr_ref = rest

    u = u_ref[...]
    proj = _dot_t(u, w_ref[...])
    z_ref[...] = proj[:, :pw]
    if seq_tiles:
        tm = u.shape[0]
        t = pl.program_id(0) % seq_tiles
        ext_ref[0:POOL_PAD, :] = jnp.where(t == 0, 0.0, ext_ref[tm:tm + POOL_PAD, :])
        ext_ref[POOL_PAD:, :] = proj[:, :pw]
        _pool_rows(ext_ref[...], t * tm, wpool_ref, scale_ref, pooled_ref, 0, tm, 1)
    q_ref[...] = _rms(proj[:, pw:pw + ql], gq_ref[...]).astype(BF16)
    ckv_ref[...] = _rms(proj[:, pw + ql:pw + ql + kl], gkv_ref[...])
    y = _dot_t(u, wkr_ref[...]) * cs_ref[...]
    kr_ref[...] = y + pltpu.roll(y, QK_ROPE, axis=1)


def _inproj(u, w_in_t, w_kr, g_q, g_kv, cs_tab, *, pw, ql, kl, pool=None, tm=1024):
    n, d = u.shape
    period = cs_tab.shape[0] // tm
    row = lambda i: (i, 0)
    fixed = lambda i: (0, 0)
    in_specs = [
        pl.BlockSpec((tm, d), row),
        pl.BlockSpec((pw + ql + kl, d), fixed),
        pl.BlockSpec(w_kr.shape, fixed),
        pl.BlockSpec((1, ql), fixed),
        pl.BlockSpec((1, kl), fixed),
        pl.BlockSpec((tm, 2 * QK_ROPE), lambda i: (i % period, 0)),
    ]
    out_specs = [
        pl.BlockSpec((tm, pw), row),
        pl.BlockSpec((tm, ql), row),
        pl.BlockSpec((tm, kl), row),
        pl.BlockSpec((tm, 2 * QK_ROPE), row),
    ]
    out_shape = [
        jax.ShapeDtypeStruct((n, pw), F32),
        jax.ShapeDtypeStruct((n, ql), BF16),
        jax.ShapeDtypeStruct((n, kl), F32),
        jax.ShapeDtypeStruct((n, 2 * QK_ROPE), F32),
    ]
    operands = [u, w_in_t, w_kr, g_q, g_kv, cs_tab]
    scratch, seq_tiles = [], 0
    if pool is not None:
        w_pool, pool_scale, seq_len = pool
        assert seq_len % tm == 0
        seq_tiles = seq_len // tm
        in_specs += [pl.BlockSpec(w_pool.shape, lambda i: (0, 0, 0)), pl.BlockSpec((1, pw), fixed)]
        out_specs.append(pl.BlockSpec((tm, pw), row))
        out_shape.append(jax.ShapeDtypeStruct((n, pw), BF16))
        operands += [w_pool, pool_scale]
        scratch = [pltpu.VMEM((POOL_PAD + tm, pw), F32)]
    return pl.pallas_call(
        functools.partial(_inproj_kernel, pw=pw, ql=ql, kl=kl, seq_tiles=seq_tiles),
        grid=(n // tm,),
        in_specs=in_specs,
        out_specs=out_specs,
        out_shape=out_shape,
        scratch_shapes=scratch,
        compiler_params=_params(("arbitrary",) if pool is not None else ("parallel",), 56),
        name="in_proj",
    )(*operands)


def _pool_rows(zext, pos0, w_ref, scale_ref, out_ref, out_row0, seg_rows, n_seg):
    p = zext.shape[1]
    gc = p // len(POOL_WINDOWS)
    ext = POOL_PAD + seg_rows
    pos = pos0 + lax.broadcasted_iota(jnp.int32, (seg_rows, 1), 0)
    for g, w in enumerate(POOL_WINDOWS):
        cols = slice(g * gc, (g + 1) * gc)
        s = zext[:, cols]
        win = s
        step = 1
        while step < w:
            win = win + pltpu.roll(win, step, axis=0)
            step *= 2
        cnt = jnp.minimum(pos + 1, w).astype(F32)
        for b in range(n_seg):
            lo = b * ext + POOL_PAD
            pooled = win[lo:lo + seg_rows] / cnt - s[lo:lo + seg_rows]
            mixed = _dot(pooled.astype(BF16), w_ref[g]) * scale_ref[:, cols]
            out_ref[pl.ds(out_row0 + b * seg_rows, seg_rows), cols] = mixed.astype(BF16)


def _pool_sample_kernel(cur_ref, hist_ref, w_ref, scale_ref, out_ref, ext_ref, *, nb, ts, pos0):
    ext = POOL_PAD + ts
    for b in range(nb):
        ext_ref[b * ext:b * ext + POOL_PAD, :] = hist_ref[b]
        ext_ref[b * ext + POOL_PAD:(b + 1) * ext, :] = cur_ref[b]
    _pool_rows(ext_ref[...], pos0, w_ref, scale_ref, out_ref, 0, ts, nb)


def _pool_sample(z, hist, w_pool, pool_scale, *, pos0, nb=8):
    b, ts, p = z.shape
    return pl.pallas_call(
        functools.partial(_pool_sample_kernel, nb=nb, ts=ts, pos0=pos0),
        grid=(b // nb,),
        in_specs=[
            pl.BlockSpec((nb, ts, p), lambda i: (i, 0, 0)),
            pl.BlockSpec((nb, POOL_PAD, p), lambda i: (i, 0, 0)),
            pl.BlockSpec(w_pool.shape, lambda i: (0, 0, 0)),
            pl.BlockSpec((1, p), lambda i: (0, 0)),
        ],
        out_specs=pl.BlockSpec((nb * ts, p), lambda i: (i, 0)),
        out_shape=jax.ShapeDtypeStruct((b * ts, p), BF16),
        scratch_shapes=[pltpu.VMEM((nb * (POOL_PAD + ts), p), F32)],
        compiler_params=_params(("parallel",), 32),
        name="pool_sample",
    )(z, hist, w_pool, pool_scale)


def _qkv_prompt_kernel(ql_ref, ckv_ref, kr_ref, wq_ref, wk_ref, wv_ref, qt_ref, q_ref, k_ref, v_ref):
    ql = ql_ref[...]
    ckv = ckv_ref[...].astype(BF16)
    kr = kr_ref[...].astype(BF16)
    qt = qt_ref[...]
    for h in range(N_HEADS):
        q_ref[:, h * HEAD_W:(h + 1) * HEAD_W] = (_dot(ql, wq_ref[:, h * HEAD_W:(h + 1) * HEAD_W]) * qt).astype(BF16)
    kn = _dot(ckv, wk_ref[...])
    for h in range(N_HEADS):
        k_ref[:, h * HEAD_W:h * HEAD_W + QK_NOPE] = kn[:, h * QK_NOPE:(h + 1) * QK_NOPE].astype(BF16)
        k_ref[:, h * HEAD_W + QK_NOPE:(h + 1) * HEAD_W] = kr
    v_ref[...] = _dot(ckv, wv_ref[...]).astype(BF16)


def _qkv_prompt(q_lat, ckv, krblk, wq_cat, w_uk, w_uv, qt_tab, *, tm=512):
    n = q_lat.shape[0]
    period = qt_tab.shape[0] // tm
    row = lambda i: (i, 0)
    fixed = lambda i: (0, 0)
    return pl.pallas_call(
        _qkv_prompt_kernel,
        grid=(n // tm,),
        in_specs=[
            pl.BlockSpec((tm, q_lat.shape[1]), row),
            pl.BlockSpec((tm, ckv.shape[1]), row),
            pl.BlockSpec((tm, krblk.shape[1]), row),
            pl.BlockSpec(wq_cat.shape, fixed),
            pl.BlockSpec(w_uk.shape, fixed),
            pl.BlockSpec(w_uv.shape, fixed),
            pl.BlockSpec((tm, HEAD_W), lambda i: (i % period, 0)),
        ],
        out_specs=[
            pl.BlockSpec((tm, N_HEADS * HEAD_W), row),
            pl.BlockSpec((tm, N_HEADS * HEAD_W), row),
            pl.BlockSpec((tm, N_HEADS * V_HEAD), row),
        ],
        out_shape=[
            jax.ShapeDtypeStruct((n, N_HEADS * HEAD_W), BF16),
            jax.ShapeDtypeStruct((n, N_HEADS * HEAD_W), BF16),
            jax.ShapeDtypeStruct((n, N_HEADS * V_HEAD), BF16),
        ],
        compiler_params=_params(("parallel",), 48),
        name="qkv_prompt",
    )(q_lat, ckv, krblk, wq_cat, w_uk, w_uv, qt_tab)


def _softmax_strips(s_ref, p_ref, m_ref, mc_ref, a_ref, masked):
    blk = s_ref.shape[1]

    def strip(r0):
        if not masked:
            return s_ref[r0:r0 + STRIP, :], blk
        visible = ((r0 >> CHUNK_SHIFT) + 1) << CHUNK_SHIFT
        width = -(-visible // LANES) * LANES
        s = s_ref[r0:r0 + STRIP, :width]
        return jnp.where(lax.broadcasted_iota(jnp.int32, s.shape, 1) < visible, s, NEG_INF), width

    for r0 in range(0, blk, STRIP):
        s, _ = strip(r0)
        mc_ref[r0:r0 + STRIP, :] = jnp.broadcast_to(jnp.max(s, axis=-1, keepdims=True), (STRIP, LANES))
    m_old = m_ref[...]
    m_new = jnp.maximum(m_old, mc_ref[...])
    m_ref[...] = m_new
    a_ref[...] = jnp.exp2(m_old - m_new)
    for r0 in range(0, blk, STRIP):
        s, width = strip(r0)
        p_ref[r0:r0 + STRIP, :width] = jnp.exp2(s - jnp.tile(m_ref[r0:r0 + STRIP, :], (1, width // LANES))).astype(BF16)
        if width < blk:
            p_ref[r0:r0 + STRIP, width:] = jnp.zeros((STRIP, blk - width), BF16)


def _attn_prompt_kernel(q_ref, k_ref, v_ref, *rest, blk, hb, n_cast):
    cast_src, o_ref, cast_dst = rest[:n_cast], rest[n_cast], rest[n_cast + 1:2 * n_cast + 1]
    s_ref, p_ref, m_ref, mc_ref, a_ref, acc_ref = rest[2 * n_cast + 1:]
    qi = pl.program_id(2)
    ones = jnp.ones((blk, V_HEAD), BF16)
    m_ref[...] = jnp.full(m_ref.shape, NEG_INF, F32)
    acc_ref[...] = jnp.zeros_like(acc_ref)

    def block(off, masked):
        for h in range(hb):
            q = q_ref[0, :, h * HEAD_W:(h + 1) * HEAD_W]
            s_ref[h] = _dot_t(q, k_ref[0, pl.ds(off, blk), h * HEAD_W:(h + 1) * HEAD_W])
        for h in range(hb):
            _softmax_strips(s_ref.at[h], p_ref.at[h], m_ref.at[h], mc_ref.at[h], a_ref.at[h], masked)
        for h in range(hb):
            v = v_ref[0, pl.ds(off, blk), h * V_HEAD:(h + 1) * V_HEAD]
            pv = _dot(p_ref[h], jnp.concatenate([v, ones], axis=1))
            acc_ref[h] = jnp.tile(a_ref[h], (1, 2)) * acc_ref[h] + pv

    def below_diagonal(j, carry):
        block(pl.multiple_of(j * blk, blk), masked=False)
        return carry

    lax.fori_loop(0, qi, below_diagonal, 0)
    for src, dst in zip(cast_src, cast_dst):
        dst[...] = src[...].astype(BF16)
    block(pl.multiple_of(qi * blk, blk), masked=True)
    for h in range(hb):
        acc = acc_ref[h]
        o_ref[0, :, h * V_HEAD:(h + 1) * V_HEAD] = (acc[:, :V_HEAD] * (1.0 / acc[:, V_HEAD:])).astype(BF16)


def _attn_prompt(q_cat, k_cat, v, *, cast=(), blk=512, hb=4):
    b, t, _ = q_cat.shape
    assert STRIP <= (1 << CHUNK_SHIFT) and blk % (1 << CHUNK_SHIFT) == 0
    ng, nq = N_HEADS // hb, t // blk
    cast_specs = _side_cast_specs(cast, b * ng, nq, lambda i, g, q: (i * ng + g, q))
    return pl.pallas_call(
        functools.partial(_attn_prompt_kernel, blk=blk, hb=hb, n_cast=len(cast)),
        grid=(b, ng, nq),
        in_specs=[
            pl.BlockSpec((1, blk, hb * HEAD_W), lambda i, g, q: (i, q, g)),
            pl.BlockSpec((1, t, hb * HEAD_W), lambda i, g, q: (i, 0, g)),
            pl.BlockSpec((1, t, hb * V_HEAD), lambda i, g, q: (i, 0, g)),
            *cast_specs,
        ],
        out_specs=[pl.BlockSpec((1, blk, hb * V_HEAD), lambda i, g, q: (i, q, g)), *cast_specs],
        out_shape=[
            jax.ShapeDtypeStruct((b, t, N_HEADS * V_HEAD), BF16),
            *[jax.ShapeDtypeStruct(w.shape, BF16) for w in cast],
        ],
        scratch_shapes=[
            pltpu.VMEM((hb, blk, blk), F32),
            pltpu.VMEM((hb, blk, blk), BF16),
            pltpu.VMEM((hb, blk, LANES), F32),
            pltpu.VMEM((hb, blk, LANES), F32),
            pltpu.VMEM((hb, blk, LANES), F32),
            pltpu.VMEM((hb, blk, 2 * V_HEAD), F32),
        ],
        compiler_params=_params(("parallel", "parallel", "arbitrary"), 48),
        name="attn_prompt",
    )(q_cat, k_cat, v, *cast)


def _q_sample_kernel(ql_ref, wq_ref, wk_ref, qt_ref, qa_ref, qr_ref, *, kl):
    ql = ql_ref[...]
    qt = qt_ref[...]
    for h in range(N_HEADS):
        q = _dot(ql, wq_ref[:, h * HEAD_W:(h + 1) * HEAD_W]) * qt
        qn = q[:, :QK_NOPE].astype(BF16)
        qa_ref[:, h * kl:(h + 1) * kl] = _dot_t(qn, wk_ref[:, h * QK_NOPE:(h + 1) * QK_NOPE]).astype(BF16)
        y = q[:, QK_NOPE:]
        qr_ref[:, h * 2 * QK_ROPE:(h + 1) * 2 * QK_ROPE] = (y + pltpu.roll(y, QK_ROPE, axis=1)).astype(BF16)


def _q_sample(q_lat, wq_cat, w_uk, qt_tab, *, tm=512):
    n, ql = q_lat.shape
    kl = w_uk.shape[0]
    row = lambda i: (i, 0)
    fixed = lambda i: (0, 0)
    return pl.pallas_call(
        functools.partial(_q_sample_kernel, kl=kl),
        grid=(n // tm,),
        in_specs=[
            pl.BlockSpec((tm, ql), row),
            pl.BlockSpec(wq_cat.shape, fixed),
            pl.BlockSpec(w_uk.shape, fixed),
            pl.BlockSpec((tm, HEAD_W), row),
        ],
        out_specs=[
            pl.BlockSpec((tm, N_HEADS * kl), row),
            pl.BlockSpec((tm, N_HEADS * 2 * QK_ROPE), row),
        ],
        out_shape=[
            jax.ShapeDtypeStruct((n, N_HEADS * kl), BF16),
            jax.ShapeDtypeStruct((n, N_HEADS * 2 * QK_ROPE), BF16),
        ],
        compiler_params=_params(("parallel",), 48),
        name="q_sample",
    )(q_lat, wq_cat, w_uk, qt_tab)


def _attn_sample_kernel(qa_ref, qr_ref, cc_ref, ck_ref, nc_ref, nk_ref, wv_ref, o_ref, *, nb, ts, past, kl):
    rows = N_HEADS * ts
    q_chunk = (past + lax.broadcasted_iota(jnp.int32, (rows, 1), 0) % ts) >> CHUNK_SHIFT
    vis_c = (lax.broadcasted_iota(jnp.int32, (1, past), 1) >> CHUNK_SHIFT) <= q_chunk
    vis_n = ((past + lax.broadcasted_iota(jnp.int32, (1, ts), 1)) >> CHUNK_SHIFT) <= q_chunk
    for b in range(nb):
        tok = slice(b * ts, (b + 1) * ts)
        qs = jnp.concatenate([qa_ref[tok, h * kl:(h + 1) * kl] for h in range(N_HEADS)], axis=0)
        qr = jnp.concatenate(
            [qr_ref[tok, h * 2 * QK_ROPE:h * 2 * QK_ROPE + QK_ROPE] for h in range(N_HEADS)], axis=0)
        kc = cc_ref[b].astype(BF16)
        krc_t = ck_ref[b].astype(BF16)
        kn = nc_ref[tok, :].astype(BF16)
        krn = nk_ref[tok, :QK_ROPE].astype(BF16)
        s_c = jnp.where(vis_c, _dot_t(qs, kc) + _dot(qr, krc_t), NEG_INF)
        s_n = jnp.where(vis_n, _dot_t(qs, kn) + _dot_t(qr, krn), NEG_INF)
        m = jnp.maximum(jnp.max(s_c, axis=-1, keepdims=True), jnp.max(s_n, axis=-1, keepdims=True))
        p_c = jnp.exp2(s_c - m)
        p_n = jnp.exp2(s_n - m)
        l = jnp.sum(p_c, axis=-1, keepdims=True) + jnp.sum(p_n, axis=-1, keepdims=True)
        o_lat = ((_dot(p_c.astype(BF16), kc) + _dot(p_n.astype(BF16), kn)) * (1.0 / l)).astype(BF16)
        for h in range(N_HEADS):
            o_ref[tok, h * V_HEAD:(h + 1) * V_HEAD] = _dot(
                o_lat[h * ts:(h + 1) * ts], wv_ref[:, h * V_HEAD:(h + 1) * V_HEAD]).astype(BF16)


def _attn_sample(q_abs, q_rope, cache_ckv, cache_kr_t, ckv_new, kr_new, w_uv, *, ts, nb=2):
    b, past, kl = cache_ckv.shape
    row = lambda i: (i, 0)
    return pl.pallas_call(
        functools.partial(_attn_sample_kernel, nb=nb, ts=ts, past=past, kl=kl),
        grid=(b // nb,),
        in_specs=[
            pl.BlockSpec((nb * ts, q_abs.shape[1]), row),
            pl.BlockSpec((nb * ts, q_rope.shape[1]), row),
            pl.BlockSpec((nb, past, kl), lambda i: (i, 0, 0)),
            pl.BlockSpec((nb, cache_kr_t.shape[1], past), lambda i: (i, 0, 0)),
            pl.BlockSpec((nb * ts, kl), row),
            pl.BlockSpec((nb * ts, kr_new.shape[1]), row),
            pl.BlockSpec(w_uv.shape, lambda i: (0, 0)),
        ],
        out_specs=pl.BlockSpec((nb * ts, N_HEADS * V_HEAD), row),
        out_shape=jax.ShapeDtypeStruct((b * ts, N_HEADS * V_HEAD), BF16),
        compiler_params=_params(("parallel",), 48),
        name="attn_sample",
    )(q_abs, q_rope, cache_ckv, cache_kr_t, ckv_new, kr_new, w_uv)


def _mix_out_kernel(u_ref, pa_ref, ob_ref, h_ref, wga_ref, wgb_ref, wpo_ref, woa_ref, wprev_ref, wlast_ref,
                    acc_ref, mprev_ref):
    j = pl.program_id(1)
    last = pl.num_programs(1) - 1

    def merged():
        u = u_ref[...]
        gate_a = jax.nn.sigmoid(_dot_t(u, wga_ref[...]))
        gate_b = jax.nn.sigmoid(_dot_t(u, wgb_ref[...]))
        a = _dot(pa_ref[...], wpo_ref[...])
        b = _dot(ob_ref[...], woa_ref[...])
        return (gate_a * a + gate_b * b).astype(BF16)

    @pl.when(j == 0)
    def _():
        acc_ref[...] = h_ref[...]
        mprev_ref[...] = merged()

    @pl.when(jnp.logical_and(j > 0, j < last))
    def _():
        m = merged()
        acc_ref[...] += _dot(mprev_ref[...], wprev_ref[...])
        mprev_ref[...] = m

    @pl.when(j == last)
    def _():
        m = merged()
        acc_ref[...] += _dot(mprev_ref[...], wprev_ref[...]) + _dot(m, wlast_ref[...])


def _mix_out(u, pooled, o_attn, h, w_in_t, gate_row0, w_pool_out, w_o_attn, w_out, *, tm=512, tc=512):
    n, d = h.shape
    nc = d // tc
    assert nc >= 2
    row = lambda i, j: (i, 0)
    col = lambda i, j: (0, j)
    return pl.pallas_call(
        _mix_out_kernel,
        grid=(n // tm, nc),
        in_specs=[
            pl.BlockSpec((tm, d), row),
            pl.BlockSpec((tm, pooled.shape[1]), row),
            pl.BlockSpec((tm, o_attn.shape[1]), row),
            pl.BlockSpec((tm, d), row),
            pl.BlockSpec((pl.Element(tc), pl.Element(d)),
                         lambda i, j: (pl.multiple_of(gate_row0 + j * tc, BF16_ROWS), 0)),
            pl.BlockSpec((pl.Element(tc), pl.Element(d)),
                         lambda i, j: (pl.multiple_of(gate_row0 + d + j * tc, BF16_ROWS), 0)),
            pl.BlockSpec((w_pool_out.shape[0], tc), col),
            pl.BlockSpec((w_o_attn.shape[0], tc), col),
            pl.BlockSpec((tc, d), lambda i, j: (jnp.maximum(j - 1, 0), 0)),
            pl.BlockSpec((tc, d), lambda i, j: (nc - 1, 0)),
        ],
        out_specs=pl.BlockSpec((tm, d), row),
        out_shape=jax.ShapeDtypeStruct((n, d), F32),
        scratch_shapes=[pltpu.VMEM((tm, tc), BF16)],
        compiler_params=_params(("parallel", "arbitrary"), 58),
        name="mix_out",
    )(u, pooled, o_attn, h, w_in_t, w_in_t, w_pool_out, w_o_attn, w_out, w_out)


def _rope_tables(pos):
    half = QK_ROPE // 2
    inv = ROPE_THETA ** (-jnp.arange(half, dtype=F32) * 2.0 / QK_ROPE)
    ang = pos.astype(F32)[:, None] * inv[None, :]
    c, s = jnp.cos(ang), jnp.sin(ang)
    cs = jnp.concatenate([c, c, -s, s], axis=-1)
    qt = (SM_SCALE * LOG2_E) * jnp.concatenate([jnp.ones((pos.shape[0], QK_NOPE), F32), cs], axis=-1)
    return cs, qt


def _dup_rope_cols(w):
    half = QK_ROPE // 2
    x1, x2 = w[..., :half], w[..., half:]
    return jnp.concatenate([x1, x2, x2, x1], axis=-1)


def kernel(x_prompt, x_sample, cache_ckv, cache_krope, state_pool, g_ffn1, w1_gate, w1_up, w1_down, g_mix, w_in, g_q_lat, g_kv_lat, w_uq, w_uk, w_uv, w_o_attn, w_pool, pool_scale, w_pool_out, w_out, g_ffn2, w2_gate, w2_up, w2_down, g_final):
    bp, tp, d = x_prompt.shape
    bs, ts, _ = x_sample.shape
    assert g_ffn1.shape[0] == 1, "single-layer stack only"
    assert ts >= POOL_HIST and tp >= POOL_HIST
    past = cache_ckv.shape[2]
    pw = pool_scale.shape[1]
    ql = g_q_lat.shape[1]
    kl = g_kv_lat.shape[1]
    o3 = pw + ql + kl
    o4 = o3 + QK_ROPE

    cs_p, qt_p = _rope_tables(jnp.arange(tp, dtype=jnp.int32))
    cs_s, qt_s = _rope_tables(past + jnp.arange(ts, dtype=jnp.int32))
    cs_s = jnp.tile(cs_s, (bs, 1))
    qt_s = jnp.tile(qt_s, (bs, 1))

    bf = lambda w: w[0].astype(BF16)
    vec = lambda g: g[0].reshape(1, -1)
    w_in_t = w_in[0].T.astype(BF16)
    w_kr = _dup_rope_cols(w_in_t[o3:o4].T).T
    wq_cat = jnp.concatenate(
        [w_uq[0][..., :QK_NOPE], _dup_rope_cols(w_uq[0][..., QK_NOPE:])], axis=-1
    ).reshape(ql, N_HEADS * HEAD_W).astype(BF16)
    wk = w_uk[0].reshape(kl, N_HEADS * QK_NOPE).astype(BF16)
    wv = w_uv[0].reshape(kl, N_HEADS * V_HEAD).astype(BF16)
    wpool = bf(w_pool)
    hist_s = jnp.pad(state_pool[0], ((0, 0), (POOL_PAD - POOL_HIST, 0), (0, 0)))
    g_last = g_final.reshape(1, -1)

    def in_proj(u, cs_tab, pool=None):
        return _inproj(u, w_in_t, w_kr, vec(g_q_lat), vec(g_kv_lat), cs_tab, pw=pw, ql=ql, kl=kl, pool=pool)


    h1_s, u_s, w1g, w1u, w1d = _ffn(
        x_sample.reshape(bs * ts, d), vec(g_ffn1), w1_gate[0], w1_up[0], w1_down[0], vec(g_mix),
        emit_normed=True, cast_weights=True)
    z, q_lat, ckv_s, kr_s = in_proj(u_s, cs_s)
    z_s = z.reshape(bs, ts, pw)
    pooled_s = _pool_sample(z_s, hist_s, wpool, vec(pool_scale), pos0=past)
    q_abs_s, q_rope_s = _q_sample(q_lat, wq_cat, wk, qt_s)

    h1, u = _ffn(
        x_prompt.reshape(bp * tp, d), vec(g_ffn1), w1g, w1u, w1d, vec(g_mix), emit_normed=True, tm=512, tf=512)
    z, q_lat, ckv_p, kr_p, pooled = in_proj(u, cs_p, pool=(wpool, vec(pool_scale), tp))
    z_p = z.reshape(bp, tp, pw)
    q_cat, k_cat, v = _qkv_prompt(q_lat, ckv_p, kr_p, wq_cat, wk, wv, qt_p)
    o, w2g, w2u, w2d, wpo, woa, wout = _attn_prompt(
        q_cat.reshape(bp, tp, -1), k_cat.reshape(bp, tp, -1), v.reshape(bp, tp, -1),
        cast=(w2_gate[0], w2_up[0], w2_down[0], w_pool_out[0], w_o_attn[0], w_out[0]))
    h2 = _mix_out(u, pooled, o.reshape(bp * tp, -1), h1, w_in_t, o4, wpo, woa, wout)
    (y_p,) = _ffn(h2, vec(g_ffn2), w2g, w2u, w2d, g_last, emit_normed=False, tm=512, tf=512)

    o_s = _attn_sample(q_abs_s, q_rope_s, cache_ckv[0], jnp.swapaxes(cache_krope[0], 1, 2), ckv_s, kr_s, wv, ts=ts)
    h2 = _mix_out(u_s, pooled_s, o_s, h1_s, w_in_t, o4, wpo, woa, wout)
    (y_s,) = _ffn(h2, vec(g_ffn2), w2g, w2u, w2d, g_last, emit_normed=False)

    return (
        y_p.reshape(bp, tp, d),
        y_s.reshape(bs, ts, d),
        ckv_p.reshape(1, bp, tp, kl),
        kr_p[:, :QK_ROPE].reshape(1, bp, tp, QK_ROPE),
        z_p[None, :, tp - POOL_HIST:],
        ckv_s.reshape(1, bs, ts, kl),
        kr_s[:, :QK_ROPE].reshape(1, bs, ts, QK_ROPE),
        z_s[None, :, ts - POOL_HIST:],
    )
```

```python
import functools

import jax
import jax.numpy as jnp
from jax import lax
from jax.experimental import pallas as pl
from jax.experimental.pallas import tpu as pltpu

F32 = jnp.float32
BF16 = jnp.bfloat16

CHUNK_SHIFT = 6
N_HEADS = 16
QK_NOPE = 128
QK_ROPE = 64
V_HEAD = 128
HEAD_W = 256
LANES = 128
BF16_ROWS = 16
STRIP = 32
POOL_WINDOWS = (2, 4, 8, 16)
POOL_HIST = 15
POOL_PAD = 16
ROPE_THETA = 10000.0
EPS = 1e-6
SM_SCALE = (QK_NOPE + QK_ROPE) ** -0.5
LOG2_E = 1.4426950408889634
NEG_INF = -1e30
MIB = 1024 * 1024


def _dot(a, b):
    return jnp.dot(a, b, preferred_element_type=F32)


def _dot_t(a, b):
    return lax.dot_general(a, b, (((1,), (1,)), ((), ())), preferred_element_type=F32)


def _rms(x, g):
    return x * lax.rsqrt(jnp.mean(x * x, axis=-1, keepdims=True) + EPS) * g


def _params(semantics, vmem_mib):
    return pltpu.CompilerParams(dimension_semantics=semantics, vmem_limit_bytes=vmem_mib * MIB)


def _ffn_kernel(x_ref, g_ref, wg_ref, wu_ref, wd_ref, gn_ref, acc_ref, *rest, emit_normed, cast_weights):
    rest = list(rest)
    u_ref = rest.pop(0) if emit_normed else None
    xn_ref = rest.pop()
    w_dst = rest
    j = pl.program_id(1)

    @pl.when(j == 0)
    def _():
        xn_ref[...] = _rms(x_ref[...], g_ref[...]).astype(BF16)
        acc_ref[...] = jnp.zeros_like(acc_ref)

    wg, wu, wd = wg_ref[...], wu_ref[...], wd_ref[...]
    if cast_weights:
        wg, wu, wd = wg.astype(BF16), wu.astype(BF16), wd.astype(BF16)
        for dst, w in zip(w_dst, (wg, wu, wd)):
            dst[...] = w
    xn = xn_ref[...]
    gate = _dot(xn, wg)
    up = _dot(xn, wu)
    act = (gate * jax.nn.sigmoid(gate) * up).astype(BF16)
    acc_ref[...] += _dot(act, wd)

    @pl.when(j == pl.num_programs(1) - 1)
    def _():
        h = x_ref[...] + 0.5 * acc_ref[...]
        if emit_normed:
            acc_ref[...] = h
            u_ref[...] = _rms(h, gn_ref[...]).astype(BF16)
        else:
            acc_ref[...] = _rms(h, gn_ref[...])


def _side_cast_specs(mats, n0, n1, to01):
    specs = []
    for w in mats:
        rows, cols = w.shape
        if rows % (n0 * BF16_ROWS) == 0 and cols % (n1 * LANES) == 0:
            specs.append(pl.BlockSpec((rows // n0, cols // n1), lambda *g: to01(*g)))
        else:
            assert rows % (n1 * BF16_ROWS) == 0 and cols % (n0 * LANES) == 0, w.shape
            specs.append(pl.BlockSpec((rows // n1, cols // n0), lambda *g: to01(*g)[::-1]))
    return specs


def _ffn(x, g, wg, wu, wd, g_next, *, emit_normed, cast_weights=False, tm=1024, tf=256):
    n, d = x.shape
    f = wg.shape[1]
    row = lambda i, j: (i, 0)
    assert not cast_weights or n == tm
    rows_mode = dict(pipeline_mode=pl.Buffered(1)) if n == tm else {}
    w_up_spec = pl.BlockSpec((d, tf), lambda i, j: (0, j))
    w_down_spec = pl.BlockSpec((tf, d), lambda i, j: (j, 0))
    out_shape = [jax.ShapeDtypeStruct((n, d), F32)]
    out_specs = [pl.BlockSpec((tm, d), row, **rows_mode)]
    if emit_normed:
        out_shape.append(jax.ShapeDtypeStruct((n, d), BF16))
        out_specs.append(pl.BlockSpec((tm, d), row, **rows_mode))
    if cast_weights:
        out_shape += [jax.ShapeDtypeStruct(w.shape, BF16) for w in (wg, wu, wd)]
        out_specs += [w_up_spec, w_up_spec, w_down_spec]
    return pl.pallas_call(
        functools.partial(_ffn_kernel, emit_normed=emit_normed, cast_weights=cast_weights),
        grid=(n // tm, f // tf),
        in_specs=[
            pl.BlockSpec((tm, d), row, **rows_mode),
            pl.BlockSpec((1, d), lambda i, j: (0, 0)),
            w_up_spec,
            w_up_spec,
            w_down_spec,
            pl.BlockSpec((1, d), lambda i, j: (0, 0)),
        ],
        out_specs=out_specs,
        out_shape=out_shape,
        scratch_shapes=[pltpu.VMEM((tm, d), BF16)],
        compiler_params=_params(("parallel", "arbitrary"), 62),
        name="ffn_norm" if emit_normed else "ffn_final",
    )(x, g, wg, wu, wd, g_next)


def _inproj_kernel(u_ref, w_ref, wkr_ref, gq_ref, gkv_ref, cs_ref, *rest, pw, ql, kl, seq_tiles):
    if seq_tiles:
        wpool_ref, scale_ref, z_ref, q_ref, ckv_ref, kr_ref, pooled_ref, ext_ref = rest

        @pl.when(pl.program_id(0) == 0)
        def _():
            ext_ref[...] = jnp.zeros_like(ext_ref)
    else:
        z_ref, q_ref, ckv_ref, kr_ref = rest

    u = u_ref[...]
    proj = _dot_t(u, w_ref[...])
    if seq_tiles:
        z_ref[0] = proj[u.shape[0] - POOL_PAD:, :pw]
    else:
        z_ref[...] = proj[:, :pw]
    if seq_tiles:
        tm = u.shape[0]
        t = pl.program_id(0) % seq_tiles
        ext_ref[0:POOL_PAD, :] = jnp.where(t == 0, 0.0, ext_ref[tm:tm + POOL_PAD, :])
        ext_ref[POOL_PAD:, :] = proj[:, :pw]
        _pool_rows(ext_ref[...], t * tm, wpool_ref, scale_ref, pooled_ref, 0, tm, 1)
    q_ref[...] = _rms(proj[:, pw:pw + ql], gq_ref[...]).astype(BF16)
    ckv_ref[...] = _rms(proj[:, pw + ql:pw + ql + kl], gkv_ref[...])
    y = _dot_t(u, wkr_ref[...]) * cs_ref[...]
    kr_ref[...] = y + pltpu.roll(y, QK_ROPE, axis=1)


def _inproj(u, w_in_t, w_kr, g_q, g_kv, cs_tab, *, pw, ql, kl, pool=None, tm=1024):
    n, d = u.shape
    period = cs_tab.shape[0] // tm
    row = lambda i: (i, 0)
    fixed = lambda i: (0, 0)
    in_specs = [
        pl.BlockSpec((tm, d), row),
        pl.BlockSpec((pw + ql + kl, d), fixed),
        pl.BlockSpec(w_kr.shape, fixed),
        pl.BlockSpec((1, ql), fixed),
        pl.BlockSpec((1, kl), fixed),
        pl.BlockSpec((tm, 2 * QK_ROPE), lambda i: (i % period, 0)),
    ]
    out_specs = [
        pl.BlockSpec((tm, pw), row),
        pl.BlockSpec((tm, ql), row),
        pl.BlockSpec((tm, kl), row),
        pl.BlockSpec((tm, 2 * QK_ROPE), row),
    ]
    out_shape = [
        jax.ShapeDtypeStruct((n, pw), F32),
        jax.ShapeDtypeStruct((n, ql), BF16),
        jax.ShapeDtypeStruct((n, kl), F32),
        jax.ShapeDtypeStruct((n, 2 * QK_ROPE), F32),
    ]
    operands = [u, w_in_t, w_kr, g_q, g_kv, cs_tab]
    scratch, seq_tiles = [], 0
    if pool is not None:
        w_pool, pool_scale, seq_len = pool
        assert seq_len % tm == 0
        seq_tiles = seq_len // tm
        out_specs[0] = pl.BlockSpec((1, POOL_PAD, pw), lambda i: (i // seq_tiles, 0, 0))
        out_shape[0] = jax.ShapeDtypeStruct((n // seq_len, POOL_PAD, pw), F32)
        in_specs += [pl.BlockSpec(w_pool.shape, lambda i: (0, 0, 0)), pl.BlockSpec((1, pw), fixed)]
        out_specs.append(pl.BlockSpec((tm, pw), row))
        out_shape.append(jax.ShapeDtypeStruct((n, pw), BF16))
        operands += [w_pool, pool_scale]
        scratch = [pltpu.VMEM((POOL_PAD + tm, pw), F32)]
    return pl.pallas_call(
        functools.partial(_inproj_kernel, pw=pw, ql=ql, kl=kl, seq_tiles=seq_tiles),
        grid=(n // tm,),
        in_specs=in_specs,
        out_specs=out_specs,
        out_shape=out_shape,
        scratch_shapes=scratch,
        compiler_params=_params(("arbitrary",) if pool is not None else ("parallel",), 56),
        name="in_proj",
    )(*operands)


def _pool_rows(zext, pos0, w_ref, scale_ref, out_ref, out_row0, seg_rows, n_seg):
    p = zext.shape[1]
    gc = p // len(POOL_WINDOWS)
    ext = POOL_PAD + seg_rows
    pos = pos0 + lax.broadcasted_iota(jnp.int32, (seg_rows, 1), 0)
    for g, w in enumerate(POOL_WINDOWS):
        cols = slice(g * gc, (g + 1) * gc)
        s = zext[:, cols]
        win = s
        step = 1
        while step < w:
            win = win + pltpu.roll(win, step, axis=0)
            step *= 2
        cnt = jnp.minimum(pos + 1, w).astype(F32)
        for b in range(n_seg):
            lo = b * ext + POOL_PAD
            pooled = win[lo:lo + seg_rows] / cnt - s[lo:lo + seg_rows]
            mixed = _dot(pooled.astype(BF16), w_ref[g]) * scale_ref[:, cols]
            out_ref[pl.ds(out_row0 + b * seg_rows, seg_rows), cols] = mixed.astype(BF16)


def _pool_sample_kernel(cur_ref, hist_ref, w_ref, scale_ref, out_ref, ext_ref, *, nb, ts, pos0):
    ext = POOL_PAD + ts
    for b in range(nb):
        ext_ref[b * ext:b * ext + POOL_PAD, :] = hist_ref[b]
        ext_ref[b * ext + POOL_PAD:(b + 1) * ext, :] = cur_ref[b]
    _pool_rows(ext_ref[...], pos0, w_ref, scale_ref, out_ref, 0, ts, nb)


def _pool_sample(z, hist, w_pool, pool_scale, *, pos0, nb=8):
    b, ts, p = z.shape
    return pl.pallas_call(
        functools.partial(_pool_sample_kernel, nb=nb, ts=ts, pos0=pos0),
        grid=(b // nb,),
        in_specs=[
            pl.BlockSpec((nb, ts, p), lambda i: (i, 0, 0)),
            pl.BlockSpec((nb, POOL_PAD, p), lambda i: (i, 0, 0)),
            pl.BlockSpec(w_pool.shape, lambda i: (0, 0, 0)),
            pl.BlockSpec((1, p), lambda i: (0, 0)),
        ],
        out_specs=pl.BlockSpec((nb * ts, p), lambda i: (i, 0)),
        out_shape=jax.ShapeDtypeStruct((b * ts, p), BF16),
        scratch_shapes=[pltpu.VMEM((nb * (POOL_PAD + ts), p), F32)],
        compiler_params=_params(("parallel",), 32),
        name="pool_sample",
    )(z, hist, w_pool, pool_scale)


def _qkv_prompt_kernel(ql_ref, ckv_ref, kr_ref, wq_ref, wk_ref, wv_ref, qt_ref, q_ref, k_ref, krb_ref, v_ref):
    ql = ql_ref[...]
    ckv = ckv_ref[...].astype(BF16)
    qt = qt_ref[...]
    for h in range(N_HEADS):
        q_ref[:, h * HEAD_W:(h + 1) * HEAD_W] = (_dot(ql, wq_ref[:, h * HEAD_W:(h + 1) * HEAD_W]) * qt).astype(BF16)
    k_ref[...] = _dot(ckv, wk_ref[...]).astype(BF16)
    krb_ref[...] = kr_ref[...].astype(BF16)
    v_ref[...] = _dot(ckv, wv_ref[...]).astype(BF16)


def _qkv_prompt(q_lat, ckv, krblk, wq_cat, w_uk, w_uv, qt_tab, *, tm=512):
    n = q_lat.shape[0]
    period = qt_tab.shape[0] // tm
    row = lambda i: (i, 0)
    fixed = lambda i: (0, 0)
    return pl.pallas_call(
        _qkv_prompt_kernel,
        grid=(n // tm,),
        in_specs=[
            pl.BlockSpec((tm, q_lat.shape[1]), row),
            pl.BlockSpec((tm, ckv.shape[1]), row),
            pl.BlockSpec((tm, krblk.shape[1]), row),
            pl.BlockSpec(wq_cat.shape, fixed),
            pl.BlockSpec(w_uk.shape, fixed),
            pl.BlockSpec(w_uv.shape, fixed),
            pl.BlockSpec((tm, HEAD_W), lambda i: (i % period, 0)),
        ],
        out_specs=[
            pl.BlockSpec((tm, N_HEADS * HEAD_W), row),
            pl.BlockSpec((tm, N_HEADS * QK_NOPE), row),
            pl.BlockSpec((tm, krblk.shape[1]), row),
            pl.BlockSpec((tm, N_HEADS * V_HEAD), row),
        ],
        out_shape=[
            jax.ShapeDtypeStruct((n, N_HEADS * HEAD_W), BF16),
            jax.ShapeDtypeStruct((n, N_HEADS * QK_NOPE), BF16),
            jax.ShapeDtypeStruct((n, krblk.shape[1]), BF16),
            jax.ShapeDtypeStruct((n, N_HEADS * V_HEAD), BF16),
        ],
        compiler_params=_params(("parallel",), 48),
        name="qkv_prompt",
    )(q_lat, ckv, krblk, wq_cat, w_uk, w_uv, qt_tab)


def _softmax_strips(s_ref, p_ref, m_ref, mc_ref, a_ref, masked):
    blk = s_ref.shape[1]

    def strip(r0):
        if not masked:
            return s_ref[r0:r0 + STRIP, :], blk
        visible = ((r0 >> CHUNK_SHIFT) + 1) << CHUNK_SHIFT
        width = -(-visible // LANES) * LANES
        s = s_ref[r0:r0 + STRIP, :width]
        return jnp.where(lax.broadcasted_iota(jnp.int32, s.shape, 1) < visible, s, NEG_INF), width

    for r0 in range(0, blk, STRIP):
        s, _ = strip(r0)
        mc_ref[r0:r0 + STRIP, :] = jnp.broadcast_to(jnp.max(s, axis=-1, keepdims=True), (STRIP, LANES))
    m_old = m_ref[...]
    m_new = jnp.maximum(m_old, mc_ref[...])
    m_ref[...] = m_new
    a_ref[...] = jnp.exp2(m_old - m_new)
    for r0 in range(0, blk, STRIP):
        s, width = strip(r0)
        p_ref[r0:r0 + STRIP, :width] = jnp.exp2(s - jnp.tile(m_ref[r0:r0 + STRIP, :], (1, width // LANES))).astype(BF16)
        if width < blk:
            p_ref[r0:r0 + STRIP, width:] = jnp.zeros((STRIP, blk - width), BF16)


def _attn_prompt_kernel(q_ref, k_ref, kr_ref, v_ref, *rest, blk, hb, n_cast):
    cast_src, o_ref, cast_dst = rest[:n_cast], rest[n_cast], rest[n_cast + 1:2 * n_cast + 1]
    s_ref, p_ref, m_ref, mc_ref, a_ref, acc_ref = rest[2 * n_cast + 1:]
    qi = pl.program_id(2)
    ones = jnp.ones((blk, V_HEAD), BF16)
    m_ref[...] = jnp.full(m_ref.shape, NEG_INF, F32)
    acc_ref[...] = jnp.zeros_like(acc_ref)

    def block(off, masked):
        kr = kr_ref[0, pl.ds(off, blk), :]
        for h in range(hb):
            q = q_ref[0, :, h * HEAD_W:(h + 1) * HEAD_W]
            k = jnp.concatenate([k_ref[0, pl.ds(off, blk), h * QK_NOPE:(h + 1) * QK_NOPE], kr], axis=1)
            s_ref[h] = _dot_t(q, k)
        for h in range(hb):
            _softmax_strips(s_ref.at[h], p_ref.at[h], m_ref.at[h], mc_ref.at[h], a_ref.at[h], masked)
        for h in range(hb):
            v = v_ref[0, pl.ds(off, blk), h * V_HEAD:(h + 1) * V_HEAD]
            pv = _dot(p_ref[h], jnp.concatenate([v, ones], axis=1))
            acc_ref[h] = jnp.tile(a_ref[h], (1, 2)) * acc_ref[h] + pv

    def below_diagonal(j, carry):
        block(pl.multiple_of(j * blk, blk), masked=False)
        return carry

    lax.fori_loop(0, qi, below_diagonal, 0)
    for src, dst in zip(cast_src, cast_dst):
        dst[...] = src[...].astype(BF16)
    block(pl.multiple_of(qi * blk, blk), masked=True)
    for h in range(hb):
        acc = acc_ref[h]
        o_ref[0, :, h * V_HEAD:(h + 1) * V_HEAD] = (acc[:, :V_HEAD] * (1.0 / acc[:, V_HEAD:])).astype(BF16)


def _attn_prompt(q_cat, k_nope, k_rope, v, *, cast=(), blk=512, hb=4):
    b, t, _ = q_cat.shape
    assert STRIP <= (1 << CHUNK_SHIFT) and blk % (1 << CHUNK_SHIFT) == 0
    ng, nq = N_HEADS // hb, t // blk
    cast_specs = _side_cast_specs(cast, b * ng, nq, lambda i, g, q: (i * ng + g, q))
    return pl.pallas_call(
        functools.partial(_attn_prompt_kernel, blk=blk, hb=hb, n_cast=len(cast)),
        grid=(b, ng, nq),
        in_specs=[
            pl.BlockSpec((1, blk, hb * HEAD_W), lambda i, g, q: (i, q, g)),
            pl.BlockSpec((1, t, hb * QK_NOPE), lambda i, g, q: (i, 0, g)),
            pl.BlockSpec((1, t, k_rope.shape[2]), lambda i, g, q: (i, 0, 0)),
            pl.BlockSpec((1, t, hb * V_HEAD), lambda i, g, q: (i, 0, g)),
            *cast_specs,
        ],
        out_specs=[pl.BlockSpec((1, blk, hb * V_HEAD), lambda i, g, q: (i, q, g)), *cast_specs],
        out_shape=[
            jax.ShapeDtypeStruct((b, t, N_HEADS * V_HEAD), BF16),
            *[jax.ShapeDtypeStruct(w.shape, BF16) for w in cast],
        ],
        scratch_shapes=[
            pltpu.VMEM((hb, blk, blk), F32),
            pltpu.VMEM((hb, blk, blk), BF16),
            pltpu.VMEM((hb, blk, LANES), F32),
            pltpu.VMEM((hb, blk, LANES), F32),
            pltpu.VMEM((hb, blk, LANES), F32),
            pltpu.VMEM((hb, blk, 2 * V_HEAD), F32),
        ],
        compiler_params=_params(("parallel", "parallel", "arbitrary"), 48),
        name="attn_prompt",
    )(q_cat, k_nope, k_rope, v, *cast)


def _q_sample_kernel(ql_ref, wq_ref, wk_ref, qt_ref, qa_ref, qr_ref, *, kl):
    ql = ql_ref[...]
    qt = qt_ref[...]
    for h in range(N_HEADS):
        q = _dot(ql, wq_ref[:, h * HEAD_W:(h + 1) * HEAD_W]) * qt
        qn = q[:, :QK_NOPE].astype(BF16)
        qa_ref[:, h * kl:(h + 1) * kl] = _dot_t(qn, wk_ref[:, h * QK_NOPE:(h + 1) * QK_NOPE]).astype(BF16)
        y = q[:, QK_NOPE:]
        qr_ref[:, h * 2 * QK_ROPE:(h + 1) * 2 * QK_ROPE] = (y + pltpu.roll(y, QK_ROPE, axis=1)).astype(BF16)


def _q_sample(q_lat, wq_cat, w_uk, qt_tab, *, tm=512):
    n, ql = q_lat.shape
    kl = w_uk.shape[0]
    row = lambda i: (i, 0)
    fixed = lambda i: (0, 0)
    return pl.pallas_call(
        functools.partial(_q_sample_kernel, kl=kl),
        grid=(n // tm,),
        in_specs=[
            pl.BlockSpec((tm, ql), row),
            pl.BlockSpec(wq_cat.shape, fixed),
            pl.BlockSpec(w_uk.shape, fixed),
            pl.BlockSpec((tm, HEAD_W), row),
        ],
        out_specs=[
            pl.BlockSpec((tm, N_HEADS * kl), row),
            pl.BlockSpec((tm, N_HEADS * 2 * QK_ROPE), row),
        ],
        out_shape=[
            jax.ShapeDtypeStruct((n, N_HEADS * kl), BF16),
            jax.ShapeDtypeStruct((n, N_HEADS * 2 * QK_ROPE), BF16),
        ],
        compiler_params=_params(("parallel",), 48),
        name="q_sample",
    )(q_lat, wq_cat, w_uk, qt_tab)


def _attn_sample_kernel(qa_ref, qr_ref, cc_ref, ck_ref, nc_ref, nk_ref, wv_ref, o_ref, *, nb, ts, past, kl):
    rows = N_HEADS * ts
    q_chunk = (past + lax.broadcasted_iota(jnp.int32, (rows, 1), 0) % ts) >> CHUNK_SHIFT
    vis_c = (lax.broadcasted_iota(jnp.int32, (1, past), 1) >> CHUNK_SHIFT) <= q_chunk
    vis_n = ((past + lax.broadcasted_iota(jnp.int32, (1, ts), 1)) >> CHUNK_SHIFT) <= q_chunk
    for b in range(nb):
        tok = slice(b * ts, (b + 1) * ts)
        qs = jnp.concatenate([qa_ref[tok, h * kl:(h + 1) * kl] for h in range(N_HEADS)], axis=0)
        qr = jnp.concatenate(
            [qr_ref[tok, h * 2 * QK_ROPE:h * 2 * QK_ROPE + QK_ROPE] for h in range(N_HEADS)], axis=0)
        kc = cc_ref[b].astype(BF16)
        krc_t = ck_ref[b].astype(BF16)
        kn = nc_ref[tok, :].astype(BF16)
        krn = nk_ref[tok, :QK_ROPE].astype(BF16)
        s_c = jnp.where(vis_c, _dot_t(qs, kc) + _dot(qr, krc_t), NEG_INF)
        s_n = jnp.where(vis_n, _dot_t(qs, kn) + _dot_t(qr, krn), NEG_INF)
        m = jnp.maximum(jnp.max(s_c, axis=-1, keepdims=True), jnp.max(s_n, axis=-1, keepdims=True))
        p_c = jnp.exp2(s_c - m)
        p_n = jnp.exp2(s_n - m)
        l = jnp.sum(p_c, axis=-1, keepdims=True) + jnp.sum(p_n, axis=-1, keepdims=True)
        o_lat = ((_dot(p_c.astype(BF16), kc) + _dot(p_n.astype(BF16), kn)) * (1.0 / l)).astype(BF16)
        for h in range(N_HEADS):
            o_ref[tok, h * V_HEAD:(h + 1) * V_HEAD] = _dot(
                o_lat[h * ts:(h + 1) * ts], wv_ref[:, h * V_HEAD:(h + 1) * V_HEAD]).astype(BF16)


def _attn_sample(q_abs, q_rope, cache_ckv, cache_kr_t, ckv_new, kr_new, w_uv, *, ts, nb=2):
    b, past, kl = cache_ckv.shape
    row = lambda i: (i, 0)
    return pl.pallas_call(
        functools.partial(_attn_sample_kernel, nb=nb, ts=ts, past=past, kl=kl),
        grid=(b // nb,),
        in_specs=[
            pl.BlockSpec((nb * ts, q_abs.shape[1]), row),
            pl.BlockSpec((nb * ts, q_rope.shape[1]), row),
            pl.BlockSpec((nb, past, kl), lambda i: (i, 0, 0)),
            pl.BlockSpec((nb, cache_kr_t.shape[1], past), lambda i: (i, 0, 0)),
            pl.BlockSpec((nb * ts, kl), row),
            pl.BlockSpec((nb * ts, kr_new.shape[1]), row),
            pl.BlockSpec(w_uv.shape, lambda i: (0, 0)),
        ],
        out_specs=pl.BlockSpec((nb * ts, N_HEADS * V_HEAD), row),
        out_shape=jax.ShapeDtypeStruct((b * ts, N_HEADS * V_HEAD), BF16),
        compiler_params=_params(("parallel",), 48),
        name="attn_sample",
    )(q_abs, q_rope, cache_ckv, cache_kr_t, ckv_new, kr_new, w_uv)


def _mix_out_kernel(u_ref, pa_ref, ob_ref, h_ref, wga_ref, wgb_ref, wpo_ref, woa_ref, wprev_ref, wlast_ref,
                    acc_ref, mprev_ref):
    j = pl.program_id(1)
    last = pl.num_programs(1) - 1

    def merged():
        u = u_ref[...]
        gate_a = jax.nn.sigmoid(_dot_t(u, wga_ref[...]))
        gate_b = jax.nn.sigmoid(_dot_t(u, wgb_ref[...]))
        a = _dot(pa_ref[...], wpo_ref[...])
        b = _dot(ob_ref[...], woa_ref[...])
        return (gate_a * a + gate_b * b).astype(BF16)

    @pl.when(j == 0)
    def _():
        acc_ref[...] = h_ref[...]
        mprev_ref[...] = merged()

    @pl.when(jnp.logical_and(j > 0, j < last))
    def _():
        m = merged()
        acc_ref[...] += _dot(mprev_ref[...], wprev_ref[...])
        mprev_ref[...] = m

    @pl.when(j == last)
    def _():
        m = merged()
        acc_ref[...] += _dot(mprev_ref[...], wprev_ref[...]) + _dot(m, wlast_ref[...])


def _mix_out(u, pooled, o_attn, h, w_in_t, gate_row0, w_pool_out, w_o_attn, w_out, *, tm=512, tc=512):
    n, d = h.shape
    nc = d // tc
    assert nc >= 2
    row = lambda i, j: (i, 0)
    col = lambda i, j: (0, j)
    return pl.pallas_call(
        _mix_out_kernel,
        grid=(n // tm, nc),
        in_specs=[
            pl.BlockSpec((tm, d), row),
            pl.BlockSpec((tm, pooled.shape[1]), row),
            pl.BlockSpec((tm, o_attn.shape[1]), row),
            pl.BlockSpec((tm, d), row),
            pl.BlockSpec((pl.Element(tc), pl.Element(d)),
                         lambda i, j: (pl.multiple_of(gate_row0 + j * tc, BF16_ROWS), 0)),
            pl.BlockSpec((pl.Element(tc), pl.Element(d)),
                         lambda i, j: (pl.multiple_of(gate_row0 + d + j * tc, BF16_ROWS), 0)),
            pl.BlockSpec((w_pool_out.shape[0], tc), col),
            pl.BlockSpec((w_o_attn.shape[0], tc), col),
            pl.BlockSpec((tc, d), lambda i, j: (jnp.maximum(j - 1, 0), 0)),
            pl.BlockSpec((tc, d), lambda i, j: (nc - 1, 0)),
        ],
        out_specs=pl.BlockSpec((tm, d), row),
        out_shape=jax.ShapeDtypeStruct((n, d), F32),
        scratch_shapes=[pltpu.VMEM((tm, tc), BF16)],
        compiler_params=_params(("parallel", "arbitrary"), 58),
        name="mix_out",
    )(u, pooled, o_attn, h, w_in_t, w_in_t, w_pool_out, w_o_attn, w_out, w_out)


def _rope_tables(pos):
    half = QK_ROPE // 2
    inv = ROPE_THETA ** (-jnp.arange(half, dtype=F32) * 2.0 / QK_ROPE)
    ang = pos.astype(F32)[:, None] * inv[None, :]
    c, s = jnp.cos(ang), jnp.sin(ang)
    cs = jnp.concatenate([c, c, -s, s], axis=-1)
    qt = (SM_SCALE * LOG2_E) * jnp.concatenate([jnp.ones((pos.shape[0], QK_NOPE), F32), cs], axis=-1)
    return cs, qt


def _dup_rope_cols(w):
    half = QK_ROPE // 2
    x1, x2 = w[..., :half], w[..., half:]
    return jnp.concatenate([x1, x2, x2, x1], axis=-1)


def kernel(x_prompt, x_sample, cache_ckv, cache_krope, state_pool, g_ffn1, w1_gate, w1_up, w1_down, g_mix, w_in, g_q_lat, g_kv_lat, w_uq, w_uk, w_uv, w_o_attn, w_pool, pool_scale, w_pool_out, w_out, g_ffn2, w2_gate, w2_up, w2_down, g_final):
    bp, tp, d = x_prompt.shape
    bs, ts, _ = x_sample.shape
    assert g_ffn1.shape[0] == 1, "single-layer stack only"
    assert ts >= POOL_HIST and tp >= POOL_HIST
    past = cache_ckv.shape[2]
    pw = pool_scale.shape[1]
    ql = g_q_lat.shape[1]
    kl = g_kv_lat.shape[1]
    o3 = pw + ql + kl
    o4 = o3 + QK_ROPE

    cs_p, qt_p = _rope_tables(jnp.arange(tp, dtype=jnp.int32))
    cs_s, qt_s = _rope_tables(past + jnp.arange(ts, dtype=jnp.int32))
    cs_s = jnp.tile(cs_s, (bs, 1))
    qt_s = jnp.tile(qt_s, (bs, 1))

    bf = lambda w: w[0].astype(BF16)
    vec = lambda g: g[0].reshape(1, -1)
    w_in_t = w_in[0].T.astype(BF16)
    w_kr = _dup_rope_cols(w_in_t[o3:o4].T).T
    wq_cat = jnp.concatenate(
        [w_uq[0][..., :QK_NOPE], _dup_rope_cols(w_uq[0][..., QK_NOPE:])], axis=-1
    ).reshape(ql, N_HEADS * HEAD_W).astype(BF16)
    wk = w_uk[0].reshape(kl, N_HEADS * QK_NOPE).astype(BF16)
    wv = w_uv[0].reshape(kl, N_HEADS * V_HEAD).astype(BF16)
    wpool = bf(w_pool)
    hist_s = jnp.pad(state_pool[0], ((0, 0), (POOL_PAD - POOL_HIST, 0), (0, 0)))
    g_last = g_final.reshape(1, -1)

    def in_proj(u, cs_tab, pool=None):
        return _inproj(u, w_in_t, w_kr, vec(g_q_lat), vec(g_kv_lat), cs_tab, pw=pw, ql=ql, kl=kl, pool=pool)


    h1_s, u_s, w1g, w1u, w1d = _ffn(
        x_sample.reshape(bs * ts, d), vec(g_ffn1), w1_gate[0], w1_up[0], w1_down[0], vec(g_mix),
        emit_normed=True, cast_weights=True)
    z, q_lat, ckv_s, kr_s = in_proj(u_s, cs_s)
    z_s = z.reshape(bs, ts, pw)
    pooled_s = _pool_sample(z_s, hist_s, wpool, vec(pool_scale), pos0=past)
    q_abs_s, q_rope_s = _q_sample(q_lat, wq_cat, wk, qt_s)

    h1, u = _ffn(x_prompt.reshape(bp * tp, d), vec(g_ffn1), w1g, w1u, w1d, vec(g_mix), emit_normed=True)
    z_tail_p, q_lat, ckv_p, kr_p, pooled = in_proj(u, cs_p, pool=(wpool, vec(pool_scale), tp))
    q_cat, k_nope, k_rope, v = _qkv_prompt(q_lat, ckv_p, kr_p, wq_cat, wk, wv, qt_p)
    o, w2g, w2u, w2d, wpo, woa, wout = _attn_prompt(
        q_cat.reshape(bp, tp, -1), k_nope.reshape(bp, tp, -1), k_rope.reshape(bp, tp, -1), v.reshape(bp, tp, -1),
        cast=(w2_gate[0], w2_up[0], w2_down[0], w_pool_out[0], w_o_attn[0], w_out[0]))
    h2 = _mix_out(u, pooled, o.reshape(bp * tp, -1), h1, w_in_t, o4, wpo, woa, wout)
    (y_p,) = _ffn(h2, vec(g_ffn2), w2g, w2u, w2d, g_last, emit_normed=False)

    o_s = _attn_sample(q_abs_s, q_rope_s, cache_ckv[0], jnp.swapaxes(cache_krope[0], 1, 2), ckv_s, kr_s, wv, ts=ts)
    h2 = _mix_out(u_s, pooled_s, o_s, h1_s, w_in_t, o4, wpo, woa, wout)
    (y_s,) = _ffn(h2, vec(g_ffn2), w2g, w2u, w2d, g_last, emit_normed=False, tf=512)

    return (
        y_p.reshape(bp, tp, d),
        y_s.reshape(bs, ts, d),
        ckv_p.reshape(1, bp, tp, kl),
        kr_p[:, :QK_ROPE].reshape(1, bp, tp, QK_ROPE),
        z_tail_p[None, :, POOL_PAD - POOL_HIST:],
        ckv_s.reshape(1, bs, ts, kl),
        kr_s[:, :QK_ROPE].reshape(1, bs, ts, QK_ROPE),
        z_s[None, :, ts - POOL_HIST:],
    )
```

```python
import functools

import jax
import jax.numpy as jnp
from jax import lax
from jax.experimental import pallas as pl
from jax.experimental.pallas import tpu as pltpu

F32 = jnp.float32
BF16 = jnp.bfloat16

CHUNK_SHIFT = 6
N_HEADS = 16
QK_NOPE = 128
QK_ROPE = 64
V_HEAD = 128
HEAD_W = 256
LANES = 128
BF16_ROWS = 16
STRIP = 32
HEAD_SPLIT = 2
POOL_WINDOWS = (2, 4, 8, 16)
POOL_HIST = 15
POOL_PAD = 16
ROPE_THETA = 10000.0
EPS = 1e-6
SM_SCALE = (QK_NOPE + QK_ROPE) ** -0.5
LOG2_E = 1.4426950408889634
NEG_INF = -1e30
MIB = 1024 * 1024


def _dot(a, b):
    return jnp.dot(a, b, preferred_element_type=F32)


def _dot_t(a, b):
    return lax.dot_general(a, b, (((1,), (1,)), ((), ())), preferred_element_type=F32)


def _rms(x, g):
    return x * lax.rsqrt(jnp.mean(x * x, axis=-1, keepdims=True) + EPS) * g


def _params(semantics, vmem_mib):
    return pltpu.CompilerParams(dimension_semantics=semantics, vmem_limit_bytes=vmem_mib * MIB)


def _ffn_kernel(x_ref, g_ref, wg_ref, wu_ref, wd_ref, gn_ref, acc_ref, *rest, emit_normed, cast_weights):
    rest = list(rest)
    u_ref = rest.pop(0) if emit_normed else None
    xn_ref = rest.pop()
    w_dst = rest
    j = pl.program_id(1)

    @pl.when(j == 0)
    def _():
        xn_ref[...] = _rms(x_ref[...], g_ref[...]).astype(BF16)
        acc_ref[...] = jnp.zeros_like(acc_ref)

    wg, wu, wd = wg_ref[...], wu_ref[...], wd_ref[...]
    if cast_weights:
        wg, wu, wd = wg.astype(BF16), wu.astype(BF16), wd.astype(BF16)
        for dst, w in zip(w_dst, (wg, wu, wd)):
            dst[...] = w
    xn = xn_ref[...]
    gate = _dot(xn, wg)
    up = _dot(xn, wu)
    act = (gate * jax.nn.sigmoid(gate) * up).astype(BF16)
    acc_ref[...] += _dot(act, wd)

    @pl.when(j == pl.num_programs(1) - 1)
    def _():
        h = x_ref[...] + 0.5 * acc_ref[...]
        if emit_normed:
            acc_ref[...] = h
            u_ref[...] = _rms(h, gn_ref[...]).astype(BF16)
        else:
            acc_ref[...] = _rms(h, gn_ref[...])


def _side_cast_specs(mats, n0, n1, to01):
    specs = []
    for w in mats:
        rows, cols = w.shape
        if rows % (n0 * BF16_ROWS) == 0 and cols % (n1 * LANES) == 0:
            specs.append(pl.BlockSpec((rows // n0, cols // n1), lambda *g: to01(*g)))
        else:
            assert rows % (n1 * BF16_ROWS) == 0 and cols % (n0 * LANES) == 0, w.shape
            specs.append(pl.BlockSpec((rows // n1, cols // n0), lambda *g: to01(*g)[::-1]))
    return specs


def _ffn(x, g, wg, wu, wd, g_next, *, emit_normed, cast_weights=False, tm=1024, tf=256):
    n, d = x.shape
    f = wg.shape[1]
    row = lambda i, j: (i, 0)
    assert not cast_weights or n == tm
    rows_mode = dict(pipeline_mode=pl.Buffered(1)) if n == tm else {}
    w_up_spec = pl.BlockSpec((d, tf), lambda i, j: (0, j))
    w_down_spec = pl.BlockSpec((tf, d), lambda i, j: (j, 0))
    out_shape = [jax.ShapeDtypeStruct((n, d), F32)]
    out_specs = [pl.BlockSpec((tm, d), row, **rows_mode)]
    if emit_normed:
        out_shape.append(jax.ShapeDtypeStruct((n, d), BF16))
        out_specs.append(pl.BlockSpec((tm, d), row, **rows_mode))
    if cast_weights:
        out_shape += [jax.ShapeDtypeStruct(w.shape, BF16) for w in (wg, wu, wd)]
        out_specs += [w_up_spec, w_up_spec, w_down_spec]
    return pl.pallas_call(
        functools.partial(_ffn_kernel, emit_normed=emit_normed, cast_weights=cast_weights),
        grid=(n // tm, f // tf),
        in_specs=[
            pl.BlockSpec((tm, d), row, **rows_mode),
            pl.BlockSpec((1, d), lambda i, j: (0, 0)),
            w_up_spec,
            w_up_spec,
            w_down_spec,
            pl.BlockSpec((1, d), lambda i, j: (0, 0)),
        ],
        out_specs=out_specs,
        out_shape=out_shape,
        scratch_shapes=[pltpu.VMEM((tm, d), BF16)],
        compiler_params=_params(("parallel", "arbitrary"), 62),
        name="ffn_norm" if emit_normed else "ffn_final",
    )(x, g, wg, wu, wd, g_next)


def _inproj_kernel(u_ref, w_ref, wkr_ref, gq_ref, gkv_ref, cs_ref, *rest, pw, ql, kl, seq_tiles):
    if seq_tiles:
        wpool_ref, scale_ref, z_ref, q_ref, ckv_ref, kr_ref, pooled_ref, ext_ref = rest

        @pl.when(pl.program_id(0) == 0)
        def _():
            ext_ref[...] = jnp.zeros_like(ext_ref)
    else:
        z_ref, q_ref, ckv_ref, kr_ref = rest

    u = u_ref[...]
    proj = _dot_t(u, w_ref[...])
    if seq_tiles:
        z_ref[0] = proj[u.shape[0] - POOL_PAD:, :pw]
    else:
        z_ref[...] = proj[:, :pw]
    if seq_tiles:
        tm = u.shape[0]
        t = pl.program_id(0) % seq_tiles
        ext_ref[0:POOL_PAD, :] = jnp.where(t == 0, 0.0, ext_ref[tm:tm + POOL_PAD, :])
        ext_ref[POOL_PAD:, :] = proj[:, :pw]
        _pool_rows(ext_ref[...], t * tm, wpool_ref, scale_ref, pooled_ref, 0, tm, 1)
    q_ref[...] = _rms(proj[:, pw:pw + ql], gq_ref[...]).astype(BF16)
    ckv_ref[...] = _rms(proj[:, pw + ql:pw + ql + kl], gkv_ref[...])
    y = _dot_t(u, wkr_ref[...]) * cs_ref[...]
    kr_ref[...] = y + pltpu.roll(y, QK_ROPE, axis=1)


def _inproj(u, w_in_t, w_kr, g_q, g_kv, cs_tab, *, pw, ql, kl, pool=None, tm=1024):
    n, d = u.shape
    period = cs_tab.shape[0] // tm
    row = lambda i: (i, 0)
    fixed = lambda i: (0, 0)
    in_specs = [
        pl.BlockSpec((tm, d), row),
        pl.BlockSpec((pw + ql + kl, d), fixed),
        pl.BlockSpec(w_kr.shape, fixed),
        pl.BlockSpec((1, ql), fixed),
        pl.BlockSpec((1, kl), fixed),
        pl.BlockSpec((tm, 2 * QK_ROPE), lambda i: (i % period, 0)),
    ]
    out_specs = [
        pl.BlockSpec((tm, pw), row),
        pl.BlockSpec((tm, ql), row),
        pl.BlockSpec((tm, kl), row),
        pl.BlockSpec((tm, 2 * QK_ROPE), row),
    ]
    out_shape = [
        jax.ShapeDtypeStruct((n, pw), F32),
        jax.ShapeDtypeStruct((n, ql), BF16),
        jax.ShapeDtypeStruct((n, kl), F32),
        jax.ShapeDtypeStruct((n, 2 * QK_ROPE), F32),
    ]
    operands = [u, w_in_t, w_kr, g_q, g_kv, cs_tab]
    scratch, seq_tiles = [], 0
    if pool is not None:
        w_pool, pool_scale, seq_len = pool
        assert seq_len % tm == 0
        seq_tiles = seq_len // tm
        out_specs[0] = pl.BlockSpec((1, POOL_PAD, pw), lambda i: (i // seq_tiles, 0, 0))
        out_shape[0] = jax.ShapeDtypeStruct((n // seq_len, POOL_PAD, pw), F32)
        in_specs += [pl.BlockSpec(w_pool.shape, lambda i: (0, 0, 0)), pl.BlockSpec((1, pw), fixed)]
        out_specs.append(pl.BlockSpec((tm, pw), row))
        out_shape.append(jax.ShapeDtypeStruct((n, pw), BF16))
        operands += [w_pool, pool_scale]
        scratch = [pltpu.VMEM((POOL_PAD + tm, pw), F32)]
    return pl.pallas_call(
        functools.partial(_inproj_kernel, pw=pw, ql=ql, kl=kl, seq_tiles=seq_tiles),
        grid=(n // tm,),
        in_specs=in_specs,
        out_specs=out_specs,
        out_shape=out_shape,
        scratch_shapes=scratch,
        compiler_params=_params(("arbitrary",) if pool is not None else ("parallel",), 56),
        name="in_proj",
    )(*operands)


def _pool_rows(zext, pos0, w_ref, scale_ref, out_ref, out_row0, seg_rows, n_seg):
    p = zext.shape[1]
    gc = p // len(POOL_WINDOWS)
    ext = POOL_PAD + seg_rows
    pos = pos0 + lax.broadcasted_iota(jnp.int32, (seg_rows, 1), 0)
    for g, w in enumerate(POOL_WINDOWS):
        cols = slice(g * gc, (g + 1) * gc)
        s = zext[:, cols]
        win = s
        step = 1
        while step < w:
            win = win + pltpu.roll(win, step, axis=0)
            step *= 2
        cnt = jnp.minimum(pos + 1, w).astype(F32)
        for b in range(n_seg):
            lo = b * ext + POOL_PAD
            pooled = win[lo:lo + seg_rows] / cnt - s[lo:lo + seg_rows]
            mixed = _dot(pooled.astype(BF16), w_ref[g]) * scale_ref[:, cols]
            out_ref[pl.ds(out_row0 + b * seg_rows, seg_rows), cols] = mixed.astype(BF16)


def _pool_sample_kernel(cur_ref, hist_ref, w_ref, scale_ref, out_ref, ext_ref, *, nb, ts, pos0):
    ext = POOL_PAD + ts
    for b in range(nb):
        ext_ref[b * ext:b * ext + POOL_PAD, :] = hist_ref[b]
        ext_ref[b * ext + POOL_PAD:(b + 1) * ext, :] = cur_ref[b]
    _pool_rows(ext_ref[...], pos0, w_ref, scale_ref, out_ref, 0, ts, nb)


def _pool_sample(z, hist, w_pool, pool_scale, *, pos0, nb=8):
    b, ts, p = z.shape
    return pl.pallas_call(
        functools.partial(_pool_sample_kernel, nb=nb, ts=ts, pos0=pos0),
        grid=(b // nb,),
        in_specs=[
            pl.BlockSpec((nb, ts, p), lambda i: (i, 0, 0)),
            pl.BlockSpec((nb, POOL_PAD, p), lambda i: (i, 0, 0)),
            pl.BlockSpec(w_pool.shape, lambda i: (0, 0, 0)),
            pl.BlockSpec((1, p), lambda i: (0, 0)),
        ],
        out_specs=pl.BlockSpec((nb * ts, p), lambda i: (i, 0)),
        out_shape=jax.ShapeDtypeStruct((b * ts, p), BF16),
        scratch_shapes=[pltpu.VMEM((nb * (POOL_PAD + ts), p), F32)],
        compiler_params=_params(("parallel",), 32),
        name="pool_sample",
    )(z, hist, w_pool, pool_scale)


def _qkv_prompt_kernel(ql_ref, ckv_ref, kr_ref, wq_ref, wk_ref, wv_ref, qt_ref, q_ref, k_ref, krb_ref, v_ref):
    ql = ql_ref[...]
    ckv = ckv_ref[...].astype(BF16)
    qt = qt_ref[...]
    for h in range(N_HEADS):
        q_ref[:, h * HEAD_W:(h + 1) * HEAD_W] = (_dot(ql, wq_ref[:, h * HEAD_W:(h + 1) * HEAD_W]) * qt).astype(BF16)
    k_ref[...] = _dot(ckv, wk_ref[...]).astype(BF16)
    krb_ref[...] = kr_ref[...].astype(BF16)
    v_ref[...] = _dot(ckv, wv_ref[...]).astype(BF16)


def _qkv_prompt(q_lat, ckv, krblk, wq_cat, w_uk, w_uv, qt_tab, *, tm=512):
    n = q_lat.shape[0]
    period = qt_tab.shape[0] // tm
    row = lambda i: (i, 0)
    fixed = lambda i: (0, 0)
    return pl.pallas_call(
        _qkv_prompt_kernel,
        grid=(n // tm,),
        in_specs=[
            pl.BlockSpec((tm, q_lat.shape[1]), row),
            pl.BlockSpec((tm, ckv.shape[1]), row),
            pl.BlockSpec((tm, krblk.shape[1]), row),
            pl.BlockSpec(wq_cat.shape, fixed),
            pl.BlockSpec(w_uk.shape, fixed),
            pl.BlockSpec(w_uv.shape, fixed),
            pl.BlockSpec((tm, HEAD_W), lambda i: (i % period, 0)),
        ],
        out_specs=[
            pl.BlockSpec((tm, N_HEADS * HEAD_W), row),
            pl.BlockSpec((tm, N_HEADS * QK_NOPE), row),
            pl.BlockSpec((tm, krblk.shape[1]), row),
            pl.BlockSpec((tm, N_HEADS * V_HEAD), row),
        ],
        out_shape=[
            jax.ShapeDtypeStruct((n, N_HEADS * HEAD_W), BF16),
            jax.ShapeDtypeStruct((n, N_HEADS * QK_NOPE), BF16),
            jax.ShapeDtypeStruct((n, krblk.shape[1]), BF16),
            jax.ShapeDtypeStruct((n, N_HEADS * V_HEAD), BF16),
        ],
        compiler_params=_params(("parallel",), 48),
        name="qkv_prompt",
    )(q_lat, ckv, krblk, wq_cat, w_uk, w_uv, qt_tab)


def _softmax_strips(s_ref, p_ref, m_ref, mc_ref, a_ref, masked):
    blk = s_ref.shape[1]

    def strip(r0):
        if not masked:
            return s_ref[r0:r0 + STRIP, :], blk
        visible = ((r0 >> CHUNK_SHIFT) + 1) << CHUNK_SHIFT
        width = -(-visible // LANES) * LANES
        s = s_ref[r0:r0 + STRIP, :width]
        return jnp.where(lax.broadcasted_iota(jnp.int32, s.shape, 1) < visible, s, NEG_INF), width

    for r0 in range(0, blk, STRIP):
        s, _ = strip(r0)
        mc_ref[r0:r0 + STRIP, :] = jnp.broadcast_to(jnp.max(s, axis=-1, keepdims=True), (STRIP, LANES))
    m_old = m_ref[...]
    m_new = jnp.maximum(m_old, mc_ref[...])
    m_ref[...] = m_new
    a_ref[...] = jnp.exp2(m_old - m_new)
    for r0 in range(0, blk, STRIP):
        s, width = strip(r0)
        p_ref[r0:r0 + STRIP, :width] = jnp.exp2(s - jnp.tile(m_ref[r0:r0 + STRIP, :], (1, width // LANES))).astype(BF16)
        if width < blk:
            p_ref[r0:r0 + STRIP, width:] = jnp.zeros((STRIP, blk - width), BF16)


def _attn_prompt_kernel(q_ref, k_ref, kr_ref, v_ref, *rest, blk, hb, n_cast):
    cast_src, o_ref, cast_dst = rest[:n_cast], rest[n_cast], rest[n_cast + 1:2 * n_cast + 1]
    s_ref, p_ref, m_ref, mc_ref, a_ref, acc_ref = rest[2 * n_cast + 1:]
    qi = pl.program_id(2)
    ones = jnp.ones((blk, V_HEAD), BF16)
    m_ref[...] = jnp.full(m_ref.shape, NEG_INF, F32)
    acc_ref[...] = jnp.zeros_like(acc_ref)

    def block(off, masked):
        kr = kr_ref[0, pl.ds(off, blk), :]
        for h in range(hb):
            q = q_ref[0, :, h * HEAD_W:(h + 1) * HEAD_W]
            k = jnp.concatenate([k_ref[0, pl.ds(off, blk), h * QK_NOPE:(h + 1) * QK_NOPE], kr], axis=1)
            s_ref[h] = _dot_t(q, k)
        for h in range(hb):
            _softmax_strips(s_ref.at[h], p_ref.at[h], m_ref.at[h], mc_ref.at[h], a_ref.at[h], masked)
        for h in range(hb):
            v = v_ref[0, pl.ds(off, blk), h * V_HEAD:(h + 1) * V_HEAD]
            pv = _dot(p_ref[h], jnp.concatenate([v, ones], axis=1))
            acc_ref[h] = jnp.tile(a_ref[h], (1, 2)) * acc_ref[h] + pv

    def below_diagonal(j, carry):
        block(pl.multiple_of(j * blk, blk), masked=False)
        return carry

    lax.fori_loop(0, qi, below_diagonal, 0)
    for src, dst in zip(cast_src, cast_dst):
        dst[...] = src[...].astype(BF16)
    block(pl.multiple_of(qi * blk, blk), masked=True)
    for h in range(hb):
        acc = acc_ref[h]
        o_ref[0, :, h * V_HEAD:(h + 1) * V_HEAD] = (acc[:, :V_HEAD] * (1.0 / acc[:, V_HEAD:])).astype(BF16)


def _attn_prompt(q_cat, k_nope, k_rope, v, *, cast=(), blk=512, hb=4):
    b, t, _ = q_cat.shape
    assert STRIP <= (1 << CHUNK_SHIFT) and blk % (1 << CHUNK_SHIFT) == 0
    ng, nq = N_HEADS // hb, t // blk
    cast_specs = _side_cast_specs(cast, b * ng, nq, lambda i, g, q: (i * ng + g, q))
    return pl.pallas_call(
        functools.partial(_attn_prompt_kernel, blk=blk, hb=hb, n_cast=len(cast)),
        grid=(b, ng, nq),
        in_specs=[
            pl.BlockSpec((1, blk, hb * HEAD_W), lambda i, g, q: (i, q, g)),
            pl.BlockSpec((1, t, hb * QK_NOPE), lambda i, g, q: (i, 0, g)),
            pl.BlockSpec((1, t, k_rope.shape[2]), lambda i, g, q: (i, 0, 0)),
            pl.BlockSpec((1, t, hb * V_HEAD), lambda i, g, q: (i, 0, g)),
            *cast_specs,
        ],
        out_specs=[pl.BlockSpec((1, blk, hb * V_HEAD), lambda i, g, q: (i, q, g)), *cast_specs],
        out_shape=[
            jax.ShapeDtypeStruct((b, t, N_HEADS * V_HEAD), BF16),
            *[jax.ShapeDtypeStruct(w.shape, BF16) for w in cast],
        ],
        scratch_shapes=[
            pltpu.VMEM((hb, blk, blk), F32),
            pltpu.VMEM((hb, blk, blk), BF16),
            pltpu.VMEM((hb, blk, LANES), F32),
            pltpu.VMEM((hb, blk, LANES), F32),
            pltpu.VMEM((hb, blk, LANES), F32),
            pltpu.VMEM((hb, blk, 2 * V_HEAD), F32),
        ],
        compiler_params=_params(("parallel", "parallel", "arbitrary"), 48),
        name="attn_prompt",
    )(q_cat, k_nope, k_rope, v, *cast)


def _q_sample_kernel(ql_ref, wq_ref, wk_ref, qt_ref, qa_ref, qr_ref, *, kl):
    ql = ql_ref[...]
    qt = qt_ref[...]
    for h in range(N_HEADS):
        q = _dot(ql, wq_ref[:, h * HEAD_W:(h + 1) * HEAD_W]) * qt
        qn = q[:, :QK_NOPE].astype(BF16)
        qa_ref[:, h * kl:(h + 1) * kl] = _dot_t(qn, wk_ref[:, h * QK_NOPE:(h + 1) * QK_NOPE]).astype(BF16)
        y = q[:, QK_NOPE:]
        qr_ref[:, h * 2 * QK_ROPE:(h + 1) * 2 * QK_ROPE] = (y + pltpu.roll(y, QK_ROPE, axis=1)).astype(BF16)


def _q_sample(q_lat, wq_cat, w_uk, qt_tab, *, tm=512):
    n, ql = q_lat.shape
    kl = w_uk.shape[0]
    row = lambda i: (i, 0)
    fixed = lambda i: (0, 0)
    return pl.pallas_call(
        functools.partial(_q_sample_kernel, kl=kl),
        grid=(n // tm,),
        in_specs=[
            pl.BlockSpec((tm, ql), row),
            pl.BlockSpec(wq_cat.shape, fixed),
            pl.BlockSpec(w_uk.shape, fixed),
            pl.BlockSpec((tm, HEAD_W), row),
        ],
        out_specs=[
            pl.BlockSpec((tm, N_HEADS * kl), row),
            pl.BlockSpec((tm, N_HEADS * 2 * QK_ROPE), row),
        ],
        out_shape=[
            jax.ShapeDtypeStruct((n, N_HEADS * kl), BF16),
            jax.ShapeDtypeStruct((n, N_HEADS * 2 * QK_ROPE), BF16),
        ],
        compiler_params=_params(("parallel",), 48),
        name="q_sample",
    )(q_lat, wq_cat, w_uk, qt_tab)


def _attn_sample_kernel(qa_ref, qr_ref, cc_ref, ck_ref, nc_ref, nk_ref, wv_ref, o_ref, *, nb, ts, past, kl):
    hh = N_HEADS // HEAD_SPLIT
    rows = hh * ts
    q_chunk = (past + lax.broadcasted_iota(jnp.int32, (rows, 1), 0) % ts) >> CHUNK_SHIFT
    vis_c = (lax.broadcasted_iota(jnp.int32, (1, past), 1) >> CHUNK_SHIFT) <= q_chunk
    vis_n = ((past + lax.broadcasted_iota(jnp.int32, (1, ts), 1)) >> CHUNK_SHIFT) <= q_chunk
    for b, h0 in [(b, h0) for b in range(nb) for h0 in range(0, N_HEADS, hh)]:
        tok = slice(b * ts, (b + 1) * ts)
        heads = range(h0, h0 + hh)
        qs = jnp.concatenate([qa_ref[tok, h * kl:(h + 1) * kl] for h in heads], axis=0)
        qr = jnp.concatenate(
            [qr_ref[tok, h * 2 * QK_ROPE:h * 2 * QK_ROPE + QK_ROPE] for h in heads], axis=0)
        kc = cc_ref[b].astype(BF16)
        krc_t = ck_ref[b].astype(BF16)
        kn = nc_ref[tok, :].astype(BF16)
        krn = nk_ref[tok, :QK_ROPE].astype(BF16)
        s_c = jnp.where(vis_c, _dot_t(qs, kc) + _dot(qr, krc_t), NEG_INF)
        s_n = jnp.where(vis_n, _dot_t(qs, kn) + _dot_t(qr, krn), NEG_INF)
        m = jnp.maximum(jnp.max(s_c, axis=-1, keepdims=True), jnp.max(s_n, axis=-1, keepdims=True))
        p_c = jnp.exp2(s_c - m)
        p_n = jnp.exp2(s_n - m)
        l = jnp.sum(p_c, axis=-1, keepdims=True) + jnp.sum(p_n, axis=-1, keepdims=True)
        o_lat = ((_dot(p_c.astype(BF16), kc) + _dot(p_n.astype(BF16), kn)) * (1.0 / l)).astype(BF16)
        for n, h in enumerate(heads):
            o_ref[tok, h * V_HEAD:(h + 1) * V_HEAD] = _dot(
                o_lat[n * ts:(n + 1) * ts], wv_ref[:, h * V_HEAD:(h + 1) * V_HEAD]).astype(BF16)


def _attn_sample(q_abs, q_rope, cache_ckv, cache_kr_t, ckv_new, kr_new, w_uv, *, ts, nb=2):
    b, past, kl = cache_ckv.shape
    row = lambda i: (i, 0)
    return pl.pallas_call(
        functools.partial(_attn_sample_kernel, nb=nb, ts=ts, past=past, kl=kl),
        grid=(b // nb,),
        in_specs=[
            pl.BlockSpec((nb * ts, q_abs.shape[1]), row),
            pl.BlockSpec((nb * ts, q_rope.shape[1]), row),
            pl.BlockSpec((nb, past, kl), lambda i: (i, 0, 0)),
            pl.BlockSpec((nb, cache_kr_t.shape[1], past), lambda i: (i, 0, 0)),
            pl.BlockSpec((nb * ts, kl), row),
            pl.BlockSpec((nb * ts, kr_new.shape[1]), row),
            pl.BlockSpec(w_uv.shape, lambda i: (0, 0)),
        ],
        out_specs=pl.BlockSpec((nb * ts, N_HEADS * V_HEAD), row),
        out_shape=jax.ShapeDtypeStruct((b * ts, N_HEADS * V_HEAD), BF16),
        compiler_params=_params(("parallel",), 48),
        name="attn_sample",
    )(q_abs, q_rope, cache_ckv, cache_kr_t, ckv_new, kr_new, w_uv)


def _mix_out_kernel(u_ref, pa_ref, ob_ref, h_ref, wga_ref, wgb_ref, wpo_ref, woa_ref, wprev_ref, wlast_ref,
                    acc_ref, mprev_ref):
    j = pl.program_id(1)
    last = pl.num_programs(1) - 1

    def merged():
        u = u_ref[...]
        gate_a = jax.nn.sigmoid(_dot_t(u, wga_ref[...]))
        gate_b = jax.nn.sigmoid(_dot_t(u, wgb_ref[...]))
        a = _dot(pa_ref[...], wpo_ref[...])
        b = _dot(ob_ref[...], woa_ref[...])
        return (gate_a * a + gate_b * b).astype(BF16)

    @pl.when(j == 0)
    def _():
        acc_ref[...] = h_ref[...]
        mprev_ref[...] = merged()

    @pl.when(jnp.logical_and(j > 0, j < last))
    def _():
        m = merged()
        acc_ref[...] += _dot(mprev_ref[...], wprev_ref[...])
        mprev_ref[...] = m

    @pl.when(j == last)
    def _():
        m = merged()
        acc_ref[...] += _dot(mprev_ref[...], wprev_ref[...]) + _dot(m, wlast_ref[...])


def _mix_out(u, pooled, o_attn, h, w_in_t, gate_row0, w_pool_out, w_o_attn, w_out, *, tm=512, tc=512):
    n, d = h.shape
    nc = d // tc
    assert nc >= 2
    row = lambda i, j: (i, 0)
    col = lambda i, j: (0, j)
    return pl.pallas_call(
        _mix_out_kernel,
        grid=(n // tm, nc),
        in_specs=[
            pl.BlockSpec((tm, d), row),
            pl.BlockSpec((tm, pooled.shape[1]), row),
            pl.BlockSpec((tm, o_attn.shape[1]), row),
            pl.BlockSpec((tm, d), row),
            pl.BlockSpec((pl.Element(tc), pl.Element(d)),
                         lambda i, j: (pl.multiple_of(gate_row0 + j * tc, BF16_ROWS), 0)),
            pl.BlockSpec((pl.Element(tc), pl.Element(d)),
                         lambda i, j: (pl.multiple_of(gate_row0 + d + j * tc, BF16_ROWS), 0)),
            pl.BlockSpec((w_pool_out.shape[0], tc), col),
            pl.BlockSpec((w_o_attn.shape[0], tc), col),
            pl.BlockSpec((tc, d), lambda i, j: (jnp.maximum(j - 1, 0), 0)),
            pl.BlockSpec((tc, d), lambda i, j: (nc - 1, 0)),
        ],
        out_specs=pl.BlockSpec((tm, d), row),
        out_shape=jax.ShapeDtypeStruct((n, d), F32),
        scratch_shapes=[pltpu.VMEM((tm, tc), BF16)],
        compiler_params=_params(("parallel", "arbitrary"), 58),
        name="mix_out",
    )(u, pooled, o_attn, h, w_in_t, w_in_t, w_pool_out, w_o_attn, w_out, w_out)


def _rope_tables(pos):
    half = QK_ROPE // 2
    inv = ROPE_THETA ** (-jnp.arange(half, dtype=F32) * 2.0 / QK_ROPE)
    ang = pos.astype(F32)[:, None] * inv[None, :]
    c, s = jnp.cos(ang), jnp.sin(ang)
    cs = jnp.concatenate([c, c, -s, s], axis=-1)
    qt = (SM_SCALE * LOG2_E) * jnp.concatenate([jnp.ones((pos.shape[0], QK_NOPE), F32), cs], axis=-1)
    return cs, qt


def _dup_rope_cols(w):
    half = QK_ROPE // 2
    x1, x2 = w[..., :half], w[..., half:]
    return jnp.concatenate([x1, x2, x2, x1], axis=-1)


def kernel(x_prompt, x_sample, cache_ckv, cache_krope, state_pool, g_ffn1, w1_gate, w1_up, w1_down, g_mix, w_in, g_q_lat, g_kv_lat, w_uq, w_uk, w_uv, w_o_attn, w_pool, pool_scale, w_pool_out, w_out, g_ffn2, w2_gate, w2_up, w2_down, g_final):
    bp, tp, d = x_prompt.shape
    bs, ts, _ = x_sample.shape
    assert g_ffn1.shape[0] == 1, "single-layer stack only"
    assert ts >= POOL_HIST and tp >= POOL_HIST
    past = cache_ckv.shape[2]
    pw = pool_scale.shape[1]
    ql = g_q_lat.shape[1]
    kl = g_kv_lat.shape[1]
    o3 = pw + ql + kl
    o4 = o3 + QK_ROPE

    cs_p, qt_p = _rope_tables(jnp.arange(tp, dtype=jnp.int32))
    cs_s, qt_s = _rope_tables(past + jnp.arange(ts, dtype=jnp.int32))
    cs_s = jnp.tile(cs_s, (bs, 1))
    qt_s = jnp.tile(qt_s, (bs, 1))

    bf = lambda w: w[0].astype(BF16)
    vec = lambda g: g[0].reshape(1, -1)
    w_in_t = w_in[0].T.astype(BF16)
    w_kr = _dup_rope_cols(w_in_t[o3:o4].T).T
    wq_cat = jnp.concatenate(
        [w_uq[0][..., :QK_NOPE], _dup_rope_cols(w_uq[0][..., QK_NOPE:])], axis=-1
    ).reshape(ql, N_HEADS * HEAD_W).astype(BF16)
    wk = w_uk[0].reshape(kl, N_HEADS * QK_NOPE).astype(BF16)
    wv = w_uv[0].reshape(kl, N_HEADS * V_HEAD).astype(BF16)
    wpool = bf(w_pool)
    hist_s = jnp.pad(state_pool[0], ((0, 0), (POOL_PAD - POOL_HIST, 0), (0, 0)))
    g_last = g_final.reshape(1, -1)

    def in_proj(u, cs_tab, pool=None):
        return _inproj(u, w_in_t, w_kr, vec(g_q_lat), vec(g_kv_lat), cs_tab, pw=pw, ql=ql, kl=kl, pool=pool)


    h1_s, u_s, w1g, w1u, w1d = _ffn(
        x_sample.reshape(bs * ts, d), vec(g_ffn1), w1_gate[0], w1_up[0], w1_down[0], vec(g_mix),
        emit_normed=True, cast_weights=True)
    z, q_lat, ckv_s, kr_s = in_proj(u_s, cs_s)
    z_s = z.reshape(bs, ts, pw)
    pooled_s = _pool_sample(z_s, hist_s, wpool, vec(pool_scale), pos0=past)
    q_abs_s, q_rope_s = _q_sample(q_lat, wq_cat, wk, qt_s)

    h1, u = _ffn(x_prompt.reshape(bp * tp, d), vec(g_ffn1), w1g, w1u, w1d, vec(g_mix), emit_normed=True)
    z_tail_p, q_lat, ckv_p, kr_p, pooled = in_proj(u, cs_p, pool=(wpool, vec(pool_scale), tp))
    q_cat, k_nope, k_rope, v = _qkv_prompt(q_lat, ckv_p, kr_p, wq_cat, wk, wv, qt_p)
    o, w2g, w2u, w2d, wpo, woa, wout = _attn_prompt(
        q_cat.reshape(bp, tp, -1), k_nope.reshape(bp, tp, -1), k_rope.reshape(bp, tp, -1), v.reshape(bp, tp, -1),
        cast=(w2_gate[0], w2_up[0], w2_down[0], w_pool_out[0], w_o_attn[0], w_out[0]))
    h2 = _mix_out(u, pooled, o.reshape(bp * tp, -1), h1, w_in_t, o4, wpo, woa, wout)
    (y_p,) = _ffn(h2, vec(g_ffn2), w2g, w2u, w2d, g_last, emit_normed=False)

    o_s = _attn_sample(q_abs_s, q_rope_s, cache_ckv[0], jnp.swapaxes(cache_krope[0], 1, 2), ckv_s, kr_s, wv, ts=ts)
    h2 = _mix_out(u_s, pooled_s, o_s, h1_s, w_in_t, o4, wpo, woa, wout)
    (y_s,) = _ffn(h2, vec(g_ffn2), w2g, w2u, w2d, g_last, emit_normed=False, tf=512)

    return (
        y_p.reshape(bp, tp, d),
        y_s.reshape(bs, ts, d),
        ckv_p.reshape(1, bp, tp, kl),
        kr_p[:, :QK_ROPE].reshape(1, bp, tp, QK_ROPE),
        z_tail_p[None, :, POOL_PAD - POOL_HIST:],
        ckv_s.reshape(1, bs, ts, kl),
        kr_s[:, :QK_ROPE].reshape(1, bs, ts, QK_ROPE),
        z_s[None, :, ts - POOL_HIST:],
    )
```
